```python
import jax, jax.numpy as jnp
from jax import lax
import numpy as np

D_MODEL = 2048
BATCH = 4
SEQ = 2048
DEPTH = 2
DEC_BATCH = 128
DEC_SEQ = 4
PAST_LEN = 16384
PAGE_SIZE = 128

N_EVEN = (DEPTH + 1) // 2
N_ODD = DEPTH // 2
D_A = D_MODEL // 2
N_GROUPS_A = 8
CONV_A = 3
D_B = D_MODEL // 2
N_HEADS_B = 8
HEAD_B = D_B // N_HEADS_B
CHUNK = 128
D_C = D_MODEL // 2
N_GROUPS_C = 8
CONV_C = 31
D_D = D_MODEL // 2
POOL_WINDOWS = (2, 4, 8, 16)
N_POOL = len(POOL_WINDOWS)
GROUP_D = D_D // N_POOL
MAX_WINDOW = max(POOL_WINDOWS)
W_IN_AB = 3 * D_A + 2 * D_B
W_IN_CD = 2 * D_C + D_D
D_FF = 5632
N_EXPERTS = 8
TOP_K = 2
D_FF_EXPERT = 7168
EPS = 1e-6

kernel_name = "hybrid_shortconv_chunkmlp_conformer_pool_decoder_step"


def rmsnorm(x, g):
    xf = x.astype(jnp.float32)
    y = xf * lax.rsqrt(jnp.mean(xf * xf, axis=-1, keepdims=True) + EPS)
    return (y * g.astype(jnp.float32)).astype(x.dtype)


def layernorm(x, g, b):
    xf = x.astype(jnp.float32)
    mu = jnp.mean(xf, axis=-1, keepdims=True)
    xc = xf - mu
    y = xc * lax.rsqrt(jnp.mean(xc * xc, axis=-1, keepdims=True) + EPS)
    return (y * g.astype(jnp.float32) + b.astype(jnp.float32)).astype(x.dtype)


def depthwise_causal_conv(x_pad, w):
    c = x_pad.shape[-1]
    return lax.conv_general_dilated(
        x_pad, w[:, None, :].astype(x_pad.dtype), window_strides=(1,), padding='VALID',
        dimension_numbers=('NWC', 'WIO', 'NWC'), feature_group_count=c)


def swiglu(h, w1, w3, w2):
    return (jax.nn.silu(h @ w1) * (h @ w3)) @ w2


def chunk_spatial_mix(v_n, w_s, b_s):
    n, l, h, dh = v_n.shape
    c = min(l, CHUNK)
    n_chunks = -(-l // c)
    lp = n_chunks * c
    v_p = jnp.pad(v_n, ((0, 0), (0, lp - l), (0, 0), (0, 0))).reshape(n, n_chunks, c, h, dh)
    mask = jnp.tril(jnp.ones((c, c), dtype=bool))
    w = jnp.where(mask[None], w_s[:, :c, :c], jnp.zeros((), w_s.dtype))
    out = jnp.einsum('hij,ncjhd->ncihd', w, v_p) + b_s[:, :c].T[None, None, :, :, None]
    return out.reshape(n, lp, h, dh)[:, :l]


def mixer_conv_chunkmlp(h, buf_a, w_in, conv_a, ln_b_g, ln_b_b, w_s, b_s, w_out):
    n, l, _ = h.shape
    proj = h @ w_in
    bg, cg, xin, u, v = jnp.split(proj, [D_A, 2 * D_A, 3 * D_A, 3 * D_A + D_B], axis=-1)
    a_pad = jnp.concatenate([buf_a.astype(h.dtype), cg * xin], axis=1)
    a_out = bg * depthwise_causal_conv(a_pad, conv_a)
    u = jax.nn.gelu(u)
    v = jax.nn.gelu(v).reshape(n, l, N_HEADS_B, HEAD_B)
    v_n = layernorm(v, ln_b_g.reshape(N_HEADS_B, HEAD_B), ln_b_b.reshape(N_HEADS_B, HEAD_B))
    b_out = u * chunk_spatial_mix(v_n, w_s, b_s).reshape(n, l, D_B)
    y = jnp.concatenate([a_out, b_out], axis=-1) @ w_out
    last_chunk_start = ((l - 1) // CHUNK) * CHUNK
    v_rows = v_n[:, last_chunk_start:].reshape(n, l - last_chunk_start, D_B)
    return y, a_pad[:, -(CONV_A - 1):], v_rows


def multi_scale_pool(d_pad, start_pos):
    n, lpad, c = d_pad.shape
    l = lpad - (MAX_WINDOW - 1)
    csum = jnp.cumsum(d_pad.astype(jnp.float32), axis=1)
    csum = jnp.concatenate([jnp.zeros((n, 1, c), jnp.float32), csum], axis=1)
    end = csum[:, MAX_WINDOW:]
    pos = start_pos + jnp.arange(l, dtype=jnp.int32)
    groups = []
    for gi, w in enumerate(POOL_WINDOWS):
        sl = slice(gi * GROUP_D, (gi + 1) * GROUP_D)
        start = csum[:, MAX_WINDOW - w:MAX_WINDOW - w + l, sl]
        cnt = jnp.minimum(w, pos + 1).astype(jnp.float32)[None, :, None]
        groups.append((end[..., sl] - start) / cnt)
    return jnp.stack(groups, axis=2)


def mixer_conformer_pool(h, buf_c, buf_d, start_pos, w_in, conv_c, conv_c_b, ln_c_g, ln_c_b,
                         pool_w, pool_scale, w_out):
    n, l, _ = h.shape
    proj = h @ w_in
    ga, gb, p = jnp.split(proj, [D_C, 2 * D_C], axis=-1)
    g = ga * jax.nn.sigmoid(gb)
    c_pad = jnp.concatenate([buf_c.astype(h.dtype), g], axis=1)
    c = depthwise_causal_conv(c_pad, conv_c) + conv_c_b
    c_out = jax.nn.silu(layernorm(c, ln_c_g, ln_c_b))
    d_pad = jnp.concatenate([buf_d.astype(h.dtype), p], axis=1)
    diff = multi_scale_pool(d_pad, start_pos) - p.astype(jnp.float32).reshape(n, l, N_POOL, GROUP_D)
    d_out = jnp.einsum('nlgc,gcd->nlgd', diff.astype(h.dtype), pool_w).reshape(n, l, D_D) * pool_scale
    y = jnp.concatenate([c_out, d_out], axis=-1) @ w_out
    return y, c_pad[:, -(CONV_C - 1):], d_pad[:, -(MAX_WINDOW - 1):]


def moe_swiglu(h, router, w1, w3, w2):
    logits = (h @ router).astype(jnp.float32)
    top_v, top_i = lax.top_k(logits, TOP_K)
    gates = jax.nn.softmax(top_v, axis=-1)
    comb = jnp.sum(jax.nn.one_hot(top_i, N_EXPERTS, dtype=jnp.float32) * gates[..., None], axis=-2)
    y = jnp.zeros_like(h)
    for e in range(N_EXPERTS):
        y = y + comb[..., e:e + 1].astype(h.dtype) * swiglu(h, w1[e], w3[e], w2[e])
    return y


def run_trunk(x, buf_a, buf_c, buf_d, start_pos, norm_mix_g, norm_ffn_g, norm_final_g,
              w_in_ab, conv_a, ln_b_g, ln_b_b, w_s, b_s, w_out_ab, ffn_w1, ffn_w3, ffn_w2,
              w_in_cd, conv_c, conv_c_b, ln_c_g, ln_c_b, pool_w, pool_scale, w_out_cd,
              router, exp_w1, exp_w3, exp_w2):
    new_a, new_v, new_c, new_d = [], [], [], []
    for layer in range(DEPTH):
        j = layer // 2
        h = rmsnorm(x, norm_mix_g[layer])
        if layer % 2 == 0:
            mix, a_state, v_rows = mixer_conv_chunkmlp(h, buf_a[j], w_in_ab[j], conv_a[j], ln_b_g[j],
                                                       ln_b_b[j], w_s[j], b_s[j], w_out_ab[j])
            new_a.append(a_state)
            new_v.append(v_rows)
            x = x + mix
            x = x + swiglu(rmsnorm(x, norm_ffn_g[layer]), ffn_w1[j], ffn_w3[j], ffn_w2[j])
        else:
            mix, c_state, d_state = mixer_conformer_pool(h, buf_c[j], buf_d[j], start_pos, w_in_cd[j],
                                                         conv_c[j], conv_c_b[j], ln_c_g[j], ln_c_b[j],
                                                         pool_w[j], pool_scale[j], w_out_cd[j])
            new_c.append(c_state)
            new_d.append(d_state)
            x = x + mix
            x = x + moe_swiglu(rmsnorm(x, norm_ffn_g[layer]), router[j], exp_w1[j], exp_w3[j], exp_w2[j])
    y = rmsnorm(x, norm_final_g)
    return y, jnp.stack(new_a), jnp.stack(new_v), jnp.stack(new_c), jnp.stack(new_d)


def setup_inputs(seed: int = 0) -> dict:
    key = jax.random.key(seed)
    ks = jax.random.split(key, 32)

    def nrm(k, shape, scale):
        return jax.random.normal(k, shape, jnp.float32) * scale

    return {
        "x_prompt": nrm(ks[0], (BATCH, SEQ, D_MODEL), 1.0),
        "x_sample": nrm(ks[1], (DEC_BATCH, DEC_SEQ, D_MODEL), 1.0),
        "state_conv_a": nrm(ks[2], (N_EVEN, DEC_BATCH, CONV_A - 1, D_A), 1.0),
        "state_conv_c": nrm(ks[3], (N_ODD, DEC_BATCH, CONV_C - 1, D_C), 1.0),
        "state_pool_d": nrm(ks[4], (N_ODD, DEC_BATCH, MAX_WINDOW - 1, D_D), 1.0),
        "norm_mix_g": 1.0 + nrm(ks[5], (DEPTH, D_MODEL), 0.02),
        "norm_ffn_g": 1.0 + nrm(ks[6], (DEPTH, D_MODEL), 0.02),
        "norm_final_g": 1.0 + nrm(ks[7], (D_MODEL,), 0.02),
        "w_in_ab": nrm(ks[8], (N_EVEN, D_MODEL, W_IN_AB), D_MODEL ** -0.5),
        "conv_a": nrm(ks[9], (N_EVEN, CONV_A, D_A), CONV_A ** -0.5),
        "ln_b_g": 1.0 + nrm(ks[10], (N_EVEN, D_B), 0.02),
        "ln_b_b": nrm(ks[11], (N_EVEN, D_B), 0.02),
        "w_s": nrm(ks[12], (N_EVEN, N_HEADS_B, CHUNK, CHUNK), CHUNK ** -0.5),
        "b_s": 1.0 + nrm(ks[13], (N_EVEN, N_HEADS_B, CHUNK), 0.02),
        "w_out_ab": nrm(ks[14], (N_EVEN, D_A + D_B, D_MODEL), (D_A + D_B) ** -0.5),
        "ffn_w1": nrm(ks[15], (N_EVEN, D_MODEL, D_FF), D_MODEL ** -0.5),
        "ffn_w3": nrm(ks[16], (N_EVEN, D_MODEL, D_FF), D_MODEL ** -0.5),
        "ffn_w2": nrm(ks[17], (N_EVEN, D_FF, D_MODEL), D_FF ** -0.5),
        "w_in_cd": nrm(ks[18], (N_ODD, D_MODEL, W_IN_CD), D_MODEL ** -0.5),
        "conv_c": nrm(ks[19], (N_ODD, CONV_C, D_C), CONV_C ** -0.5),
        "conv_c_b": nrm(ks[20], (N_ODD, D_C), 0.01),
        "ln_c_g": 1.0 + nrm(ks[21], (N_ODD, D_C), 0.02),
        "ln_c_b": nrm(ks[22], (N_ODD, D_C), 0.02),
        "pool_w": nrm(ks[23], (N_ODD, N_POOL, GROUP_D, GROUP_D), GROUP_D ** -0.5),
        "pool_scale": 1.0 + nrm(ks[24], (N_ODD, D_D), 0.02),
        "w_out_cd": nrm(ks[25], (N_ODD, D_C + D_D, D_MODEL), (D_C + D_D) ** -0.5),
        "router": nrm(ks[26], (N_ODD, D_MODEL, N_EXPERTS), D_MODEL ** -0.5),
        "exp_w1": nrm(ks[27], (N_ODD, N_EXPERTS, D_MODEL, D_FF_EXPERT), D_MODEL ** -0.5),
        "exp_w3": nrm(ks[28], (N_ODD, N_EXPERTS, D_MODEL, D_FF_EXPERT), D_MODEL ** -0.5),
        "exp_w2": nrm(ks[29], (N_ODD, N_EXPERTS, D_FF_EXPERT, D_MODEL), D_FF_EXPERT ** -0.5),
    }


def reference(x_prompt, x_sample, state_conv_a, state_conv_c, state_pool_d,
              norm_mix_g, norm_ffn_g, norm_final_g,
              w_in_ab, conv_a, ln_b_g, ln_b_b, w_s, b_s, w_out_ab, ffn_w1, ffn_w3, ffn_w2,
              w_in_cd, conv_c, conv_c_b, ln_c_g, ln_c_b, pool_w, pool_scale, w_out_cd,
              router, exp_w1, exp_w3, exp_w2):
    params = (norm_mix_g, norm_ffn_g, norm_final_g,
              w_in_ab, conv_a, ln_b_g, ln_b_b, w_s, b_s, w_out_ab, ffn_w1, ffn_w3, ffn_w2,
              w_in_cd, conv_c, conv_c_b, ln_c_g, ln_c_b, pool_w, pool_scale, w_out_cd,
              router, exp_w1, exp_w3, exp_w2)
    dt = x_prompt.dtype
    zero_a = jnp.zeros((N_EVEN, BATCH, CONV_A - 1, D_A), dt)
    zero_c = jnp.zeros((N_ODD, BATCH, CONV_C - 1, D_C), dt)
    zero_d = jnp.zeros((N_ODD, BATCH, MAX_WINDOW - 1, D_D), dt)
    y_prompt, new_conv_a_prompt, new_v_b_prompt, new_conv_c_prompt, new_pool_d_prompt = run_trunk(
        x_prompt, zero_a, zero_c, zero_d, 0, *params)
    y_sample, new_conv_a_sample, new_v_b_sample, new_conv_c_sample, new_pool_d_sample = run_trunk(
        x_sample, state_conv_a, state_conv_c, state_pool_d, PAST_LEN, *params)
    return (y_prompt, y_sample,
            new_conv_a_prompt, new_conv_a_sample,
            new_v_b_prompt, new_v_b_sample,
            new_conv_c_prompt, new_conv_c_sample,
            new_pool_d_prompt, new_pool_d_sample)
```

```python
import functools

import jax
import jax.numpy as jnp
from jax import lax
from jax.experimental import pallas as pl
from jax.experimental.pallas import tpu as pltpu

F32 = jnp.float32
BF16 = jnp.bfloat16
I32 = jnp.int32

EPS = 1e-6
PAST_LEN = 16384
CHUNK = 128
HEAD_B = 128
POOL_WINDOWS = (2, 4, 8, 16)
TOP_K = 2

VMEM_LIMIT_BYTES = 60 * 1024 * 1024
SUBLANES = 8
LANES = 128


def _params(*sem):
    return pltpu.CompilerParams(dimension_semantics=sem, vmem_limit_bytes=VMEM_LIMIT_BYTES)


def _pick(n, candidates):
    for c in candidates:
        if n % c == 0:
            return c
    raise ValueError(f"no tile in {candidates} divides {n}")


def _rms(x, g):
    return x * lax.rsqrt(jnp.mean(x * x, axis=-1, keepdims=True) + EPS) * g


def _layernorm(x, g, b):
    mu = jnp.mean(x, axis=-1, keepdims=True)
    xc = x - mu
    return xc * lax.rsqrt(jnp.mean(xc * xc, axis=-1, keepdims=True) + EPS) * g + b


def _silu(x):
    return x * jax.nn.sigmoid(x)


def _dot(a, b):
    return jnp.dot(a, b, preferred_element_type=F32)


def _norm_proj_kernel(x_ref, g_ref, w_ref, o_ref, h_ref):
    @pl.when(pl.program_id(1) == 0)
    def _():
        h_ref[...] = _rms(x_ref[...], g_ref[...]).astype(BF16)

    o_ref[...] = _dot(h_ref[...], w_ref[...].astype(BF16))


def _norm_proj(x, g, w, tm, tn):
    m, d = x.shape
    n = w.shape[1]
    return pl.pallas_call(
        _norm_proj_kernel,
        grid=(m // tm, n // tn),
        in_specs=[pl.BlockSpec((tm, d), lambda i, j: (i, 0)),
                  pl.BlockSpec((1, d), lambda i, j: (0, 0)),
                  pl.BlockSpec((d, tn), lambda i, j: (0, j))],
        out_specs=pl.BlockSpec((tm, tn), lambda i, j: (i, j)),
        out_shape=jax.ShapeDtypeStruct((m, n), F32),
        scratch_shapes=[pltpu.VMEM((tm, d), BF16)],
        compiler_params=_params("parallel", "arbitrary"),
        name="norm_proj",
    )(x, g.reshape(1, d), w)


def _proj_res_kernel(a_ref, w_ref, r_ref, o_ref):
    o_ref[...] = r_ref[...] + _dot(a_ref[...], w_ref[...].astype(BF16))


def _proj_res(a, w, res, tm, tn):
    m, k = a.shape
    n = w.shape[1]
    return pl.pallas_call(
        _proj_res_kernel,
        grid=(m // tm, n // tn),
        in_specs=[pl.BlockSpec((tm, k), lambda i, j: (i, 0)),
                  pl.BlockSpec((k, tn), lambda i, j: (0, j)),
                  pl.BlockSpec((tm, tn), lambda i, j: (i, j))],
        out_specs=pl.BlockSpec((tm, tn), lambda i, j: (i, j)),
        out_shape=jax.ShapeDtypeStruct((m, n), F32),
        compiler_params=_params("parallel", "arbitrary"),
        name="proj_res",
    )(a, w, res)


def _ffn_kernel(x_ref, g_ref, w1_ref, w3_ref, w2_ref, o_ref, h_ref, *, n_col_chunks):
    @pl.when(pl.program_id(1) == 0)
    def _():
        x = x_ref[...]
        h_ref[...] = _rms(x, g_ref[...]).astype(BF16)
        o_ref[...] = x

    h = h_ref[...]
    a = _dot(h, w1_ref[...].astype(BF16))
    b = _dot(h, w3_ref[...].astype(BF16))
    act = (_silu(a) * b).astype(BF16)
    d = o_ref.shape[1]
    cw = d // n_col_chunks
    for c in range(n_col_chunks):
        cols = slice(c * cw, (c + 1) * cw)
        o_ref[:, cols] += _dot(act, w2_ref[:, cols].astype(BF16))


def _ffn(x, g, w1, w3, w2, tm, tf):
    m, d = x.shape
    f = w1.shape[1]
    return pl.pallas_call(
        functools.partial(_ffn_kernel, n_col_chunks=max(1, d // 512)),
        grid=(m // tm, f // tf),
        in_specs=[pl.BlockSpec((tm, d), lambda i, j: (i, 0)),
                  pl.BlockSpec((1, d), lambda i, j: (0, 0)),
                  pl.BlockSpec((d, tf), lambda i, j: (0, j)),
                  pl.BlockSpec((d, tf), lambda i, j: (0, j)),
                  pl.BlockSpec((tf, d), lambda i, j: (j, 0))],
        out_specs=pl.BlockSpec((tm, d), lambda i, j: (i, 0)),
        out_shape=jax.ShapeDtypeStruct((m, d), F32),
        scratch_shapes=[pltpu.VMEM((tm, d), BF16)],
        compiler_params=_params("parallel", "arbitrary"),
        name="ffn",
    )(x, g.reshape(1, d), w1, w3, w2)


def _mix_ab_prompt_kernel(bg_ref, cg_ref, xin_ref, u_ref, v_ref, cgh_ref, xinh_ref,
                          cw_ref, lng_ref, lnb_ref, ws_ref, bsb_ref,
                          o_ref, astate_ref, vstate_ref, abuf_ref):
    t = pl.program_id(1)
    t_rows, da = bg_ref.shape
    n_heads = ws_ref.shape[0]
    halo = SUBLANES

    a = cg_ref[...] * xin_ref[...]
    prev = jnp.where(t == 0, 0.0, cgh_ref[...] * xinh_ref[...])
    abuf_ref[0:halo, :] = prev
    abuf_ref[halo:halo + t_rows, :] = a
    conv = (cw_ref[0:1, :] * abuf_ref[halo - 2:halo - 2 + t_rows, :]
            + cw_ref[1:2, :] * abuf_ref[halo - 1:halo - 1 + t_rows, :]
            + cw_ref[2:3, :] * a)
    o_ref[:, 0:da] = (bg_ref[...] * conv).astype(BF16)
    astate_ref[0] = abuf_ref[halo + t_rows - 2:halo + t_rows, :]

    row = lax.broadcasted_iota(I32, (CHUNK, CHUNK), 0)
    col = lax.broadcasted_iota(I32, (CHUNK, CHUNK), 1)
    lower = row >= col
    for h in range(n_heads):
        cols = slice(h * HEAD_B, (h + 1) * HEAD_B)
        vn = _layernorm(jax.nn.gelu(v_ref[:, cols]), lng_ref[:, cols], lnb_ref[:, cols])
        vstate_ref[0, :, cols] = vn[t_rows - CHUNK:t_rows]
        w = jnp.where(lower, ws_ref[h], 0.0).astype(BF16)
        for c in range(t_rows // CHUNK):
            rows = slice(c * CHUNK, (c + 1) * CHUNK)
            mix = _dot(w, vn[rows].astype(BF16)) + bsb_ref[h]
            o_ref[rows, da + h * HEAD_B:da + (h + 1) * HEAD_B] = (
                jax.nn.gelu(u_ref[rows, cols]) * mix).astype(BF16)


def _mix_ab_prompt(proj, conv_a, ln_g, ln_b, w_s, b_s, n_seq, seq_len, t_rows):
    m = proj.shape[0]
    da = conv_a.shape[1]
    db = ln_g.shape[0]
    n_heads = w_s.shape[0]
    tiles = seq_len // t_rows
    halo_blocks = t_rows // SUBLANES

    def col_spec(c):
        return pl.BlockSpec((t_rows, da), lambda b, t: (b * tiles + t, c))

    def halo_spec(c):
        return pl.BlockSpec(
            (SUBLANES, da),
            lambda b, t: (jnp.maximum((b * tiles + t) * halo_blocks - 1, 0), c))

    def full(shape):
        return pl.BlockSpec(shape, lambda b, t: (0,) * len(shape))

    bsb = jnp.broadcast_to(b_s[:, :CHUNK, None], (n_heads, CHUNK, HEAD_B))
    return pl.pallas_call(
        _mix_ab_prompt_kernel,
        grid=(n_seq, tiles),
        in_specs=[col_spec(0), col_spec(1), col_spec(2), col_spec(3), col_spec(4),
                  halo_spec(1), halo_spec(2),
                  full((conv_a.shape[0], da)), full((1, db)), full((1, db)),
                  full((n_heads, CHUNK, CHUNK)), full((n_heads, CHUNK, HEAD_B))],
        out_specs=[pl.BlockSpec((t_rows, da + db), lambda b, t: (b * tiles + t, 0)),
                   pl.BlockSpec((1, 2, da), lambda b, t: (b, 0, 0)),
                   pl.BlockSpec((1, CHUNK, db), lambda b, t: (b, 0, 0))],
        out_shape=[jax.ShapeDtypeStruct((m, da + db), BF16),
                   jax.ShapeDtypeStruct((n_seq, 2, da), F32),
                   jax.ShapeDtypeStruct((n_seq, CHUNK, db), F32)],
        scratch_shapes=[pltpu.VMEM((SUBLANES + t_rows, da), F32)],
        compiler_params=_params("parallel", "arbitrary"),
        name="mix_ab_prompt",
    )(proj, proj, proj, proj, proj, proj, proj, conv_a, ln_g.reshape(1, db), ln_b.reshape(1, db),
      w_s[:, :CHUNK, :CHUNK], bsb)


def _mix_ab_sample_kernel(mixed_hbm_ref, bg_ref, cg_ref, xin_ref, u_ref, v_ref, st_ref,
                          cw_ref, lng_ref, lnb_ref, wsl_ref, bsl_ref,
                          o_ref, astate_ref, vstate_ref, *, dl, nseq):
    del mixed_hbm_ref
    da = bg_ref.shape[1]
    n_heads = lng_ref.shape[1] // HEAD_B
    n_state = st_ref.shape[0]

    def slab(l):
        return slice(l * nseq, (l + 1) * nseq)

    a_pad = [st_ref[k] for k in range(n_state)]
    a_pad += [cg_ref[slab(l), :] * xin_ref[slab(l), :] for l in range(dl)]
    for l in range(dl):
        conv = (cw_ref[0:1, :] * a_pad[l] + cw_ref[1:2, :] * a_pad[l + 1]
                + cw_ref[2:3, :] * a_pad[l + 2])
        o_ref[slab(l), 0:da] = (bg_ref[slab(l), :] * conv).astype(BF16)
    for k in range(n_state):
        astate_ref[k] = a_pad[dl + k]

    for h in range(n_heads):
        cols = slice(h * HEAD_B, (h + 1) * HEAD_B)
        vn = _layernorm(jax.nn.gelu(v_ref[:, cols]), lng_ref[:, cols], lnb_ref[:, cols])
        vstate_ref[:, cols] = vn
        for i in range(dl):
            mix = bsl_ref[i:i + 1, cols]
            for j in range(i + 1):
                mix = mix + wsl_ref[i * dl + j:i * dl + j + 1, cols] * vn[slab(j)]
            o_ref[slab(i), da + h * HEAD_B:da + (h + 1) * HEAD_B] = (
                jax.nn.gelu(u_ref[slab(i), cols]) * mix).astype(BF16)


def _mix_ab_sample(mixed, proj, state_lm, conv_a, ln_g, ln_b, w_s, b_s, row0, dl, nseq):
    ms = dl * nseq
    da = conv_a.shape[1]
    db = ln_g.shape[0]
    n_heads = w_s.shape[0]
    rb = row0 // ms

    def col_spec(c):
        return pl.BlockSpec((ms, da), lambda i: (rb, c))

    def full(shape):
        return pl.BlockSpec(shape, lambda i: (0,) * len(shape))

    wsl = jnp.repeat(jnp.transpose(w_s[:, :dl, :dl], (1, 2, 0)).reshape(dl * dl, n_heads), HEAD_B, axis=1)
    bsl = jnp.repeat(jnp.transpose(b_s[:, :dl], (1, 0)), HEAD_B, axis=1)
    return pl.pallas_call(
        functools.partial(_mix_ab_sample_kernel, dl=dl, nseq=nseq),
        grid=(1,),
        in_specs=[pl.BlockSpec(memory_space=pl.ANY),
                  col_spec(0), col_spec(1), col_spec(2), col_spec(3), col_spec(4),
                  full(state_lm.shape), full((conv_a.shape[0], da)), full((1, db)), full((1, db)),
                  full((dl * dl, db)), full((dl, db))],
        out_specs=[pl.BlockSpec((ms, da + db), lambda i: (rb, 0)),
                   full(state_lm.shape),
                   full((ms, db))],
        out_shape=[jax.ShapeDtypeStruct(mixed.shape, BF16),
                   jax.ShapeDtypeStruct(state_lm.shape, F32),
                   jax.ShapeDtypeStruct((ms, db), F32)],
        input_output_aliases={0: 0},
        compiler_params=_params("arbitrary"),
        name="mix_ab_sample",
    )(mixed, proj, proj, proj, proj, proj, state_lm, conv_a, ln_g.reshape(1, db), ln_b.reshape(1, db),
      wsl, bsl)


def _mix_cd_prompt_kernel(ga_ref, gb_ref, p_ref, gah_ref, gbh_ref, ph_ref,
                          cw_ref, cb_ref, lng_ref, lnb_ref, pw_ref, ps_ref,
                          o_ref, cstate_ref, dstate_ref, gbuf_ref, pbuf_ref, cbuf_ref):
    t = pl.program_id(1)
    t_rows, dc = ga_ref.shape
    dd = p_ref.shape[1]
    k_c = cw_ref.shape[0]
    c_halo = gah_ref.shape[0]
    d_halo = ph_ref.shape[0]
    gd = pw_ref.shape[1]

    g = ga_ref[...] * jax.nn.sigmoid(gb_ref[...])
    gbuf_ref[0:c_halo, :] = jnp.where(t == 0, 0.0, gah_ref[...] * jax.nn.sigmoid(gbh_ref[...]))
    gbuf_ref[c_halo:c_halo + t_rows, :] = g
    first = c_halo - (k_c - 1)
    for cblk in range(dc // LANES):
        cols = slice(cblk * LANES, (cblk + 1) * LANES)
        acc = cb_ref[:, cols] + cw_ref[k_c - 1:k_c, cols] * gbuf_ref[c_halo:c_halo + t_rows, cols]
        for k in range(k_c - 1):
            acc = acc + cw_ref[k:k + 1, cols] * gbuf_ref[first + k:first + k + t_rows, cols]
        cbuf_ref[:, cols] = acc
    o_ref[:, 0:dc] = _silu(_layernorm(cbuf_ref[...], lng_ref[...], lnb_ref[...])).astype(BF16)
    cstate_ref[0] = gbuf_ref[c_halo + t_rows - (k_c - 1):c_halo + t_rows, :]

    pbuf_ref[0:d_halo, :] = jnp.where(t == 0, 0.0, ph_ref[...])
    pbuf_ref[d_halo:d_halo + t_rows, :] = p_ref[...]
    pos = t * t_rows + lax.broadcasted_iota(I32, (t_rows, 1), 0)
    for gi, w in enumerate(POOL_WINDOWS):
        cols = slice(gi * gd, (gi + 1) * gd)
        cur = p_ref[:, cols]
        s = cur
        for k in range(1, w):
            s = s + pbuf_ref[d_halo - k:d_halo - k + t_rows, cols]
        cnt = jnp.minimum(w, pos + 1).astype(F32)
        diff = (s / cnt - cur).astype(BF16)
        d = _dot(diff, pw_ref[gi].astype(BF16)) * ps_ref[:, cols]
        o_ref[:, dc + gi * gd:dc + (gi + 1) * gd] = d.astype(BF16)
    n_keep = dstate_ref.shape[1]
    dstate_ref[0] = pbuf_ref[d_halo + t_rows - n_keep:d_halo + t_rows, :]


def _mix_cd_prompt(proj, conv_c, conv_c_b, ln_g, ln_b, pool_w, pool_scale, n_seq, seq_len, t_rows):
    m = proj.shape[0]
    k_c, dc = conv_c.shape
    dd = pool_scale.shape[0]
    n_pool, gd = pool_w.shape[0], pool_w.shape[1]
    tiles = seq_len // t_rows
    c_halo = -(-(k_c - 1) // SUBLANES) * SUBLANES
    d_halo = -(-(max(POOL_WINDOWS) - 1) // SUBLANES) * SUBLANES
    assert t_rows % c_halo == 0 and t_rows % d_halo == 0 and dc == dd

    def col_spec(c):
        return pl.BlockSpec((t_rows, dc), lambda b, t: (b * tiles + t, c))

    def halo_spec(c, rows):
        per = t_rows // rows
        return pl.BlockSpec((rows, dc),
                            lambda b, t: (jnp.maximum((b * tiles + t) * per - 1, 0), c))

    def full(shape):
        return pl.BlockSpec(shape, lambda b, t: (0,) * len(shape))

    return pl.pallas_call(
        _mix_cd_prompt_kernel,
        grid=(n_seq, tiles),
        in_specs=[col_spec(0), col_spec(1), col_spec(2),
                  halo_spec(0, c_halo), halo_spec(1, c_halo), halo_spec(2, d_halo),
                  full((k_c, dc)), full((1, dc)), full((1, dc)), full((1, dc)),
                  full((n_pool, gd, gd)), full((1, dd))],
        out_specs=[pl.BlockSpec((t_rows, dc + dd), lambda b, t: (b * tiles + t, 0)),
                   pl.BlockSpec((1, k_c - 1, dc), lambda b, t: (b, 0, 0)),
                   pl.BlockSpec((1, max(POOL_WINDOWS) - 1, dd), lambda b, t: (b, 0, 0))],
        out_shape=[jax.ShapeDtypeStruct((m, dc + dd), BF16),
                   jax.ShapeDtypeStruct((n_seq, k_c - 1, dc), F32),
                   jax.ShapeDtypeStruct((n_seq, max(POOL_WINDOWS) - 1, dd), F32)],
        scratch_shapes=[pltpu.VMEM((c_halo + t_rows, dc), F32),
                        pltpu.VMEM((d_halo + t_rows, dd), F32),
                        pltpu.VMEM((t_rows, dc), F32)],
        compiler_params=_params("parallel", "arbitrary"),
        name="mix_cd_prompt",
    )(proj, proj, proj, proj, proj, proj, conv_c, conv_c_b.reshape(1, dc), ln_g.reshape(1, dc),
      ln_b.reshape(1, dc), pool_w, pool_scale.reshape(1, dd))


def _mix_cd_sample_kernel(mixed_hbm_ref, ga_ref, gb_ref, p_ref, stc_ref, std_ref,
                          cw_ref, cb_ref, lng_ref, lnb_ref, pw_ref, ps_ref,
                          o_ref, g_ref, cbuf_ref, dbuf_ref, *, dl, nseq, start_pos):
    del mixed_hbm_ref
    gi = pl.program_id(0)
    n_groups = pl.num_programs(0)
    gd = ga_ref.shape[1]
    k_c = cw_ref.shape[0]
    n_c = stc_ref.shape[0]
    n_d = std_ref.shape[0]

    def slab(l):
        return slice(l * nseq, (l + 1) * nseq)

    g = ga_ref[...] * jax.nn.sigmoid(gb_ref[...])
    g_ref[...] = g
    c_pad = [stc_ref[k] for k in range(n_c)] + [g[slab(l)] for l in range(dl)]
    d_pad = [std_ref[k] for k in range(n_d)] + [p_ref[slab(l), :] for l in range(dl)]
    for l in range(dl):
        acc = cb_ref[...] + cw_ref[0:1, :] * c_pad[l]
        for k in range(1, k_c):
            acc = acc + cw_ref[k:k + 1, :] * c_pad[l + k]
        cbuf_ref[gi, slab(l), :] = acc

        cur = d_pad[n_d + l]
        s = cur
        pooled = jnp.zeros_like(cur)
        for k in range(1, max(POOL_WINDOWS) + 1):
            if k in POOL_WINDOWS:
                cnt = float(min(k, start_pos + l + 1))
                pooled = jnp.where(gi == POOL_WINDOWS.index(k), s / cnt, pooled)
            if k < max(POOL_WINDOWS):
                s = s + d_pad[n_d + l - k]
        diff = (pooled - cur).astype(BF16)
        dbuf_ref[gi, slab(l), :] = (_dot(diff, pw_ref[0].astype(BF16)) * ps_ref[...]).astype(BF16)

    @pl.when(gi == n_groups - 1)
    def _():
        n_blk = cbuf_ref.shape[0]
        dc = n_blk * gd
        total = cbuf_ref[0].sum(axis=-1, keepdims=True)
        for b in range(1, n_blk):
            total = total + cbuf_ref[b].sum(axis=-1, keepdims=True)
        mu = total / dc
        var = None
        for b in range(n_blk):
            xc = cbuf_ref[b] - mu
            sq = (xc * xc).sum(axis=-1, keepdims=True)
            var = sq if var is None else var + sq
        rstd = lax.rsqrt(var / dc + EPS)
        for b in range(n_blk):
            cols = slice(b * gd, (b + 1) * gd)
            y = (cbuf_ref[b] - mu) * rstd * lng_ref[:, cols] + lnb_ref[:, cols]
            o_ref[:, cols] = _silu(y).astype(BF16)
            o_ref[:, dc + b * gd:dc + (b + 1) * gd] = dbuf_ref[b]


def _mix_cd_sample(mixed, proj, stc_lm, std_lm, conv_c, conv_c_b, ln_g, ln_b, pool_w, pool_scale,
                   row0, dl, nseq):
    ms = dl * nseq
    k_c, dc = conv_c.shape
    dd = pool_scale.shape[0]
    n_pool, gd = pool_w.shape[0], pool_w.shape[1]
    assert dc == dd == n_pool * gd and n_pool == len(POOL_WINDOWS)
    rb = row0 // ms

    return pl.pallas_call(
        functools.partial(_mix_cd_sample_kernel, dl=dl, nseq=nseq, start_pos=PAST_LEN),
        grid=(n_pool,),
        in_specs=[pl.BlockSpec(memory_space=pl.ANY),
                  pl.BlockSpec((ms, gd), lambda i: (rb, i)),
                  pl.BlockSpec((ms, gd), lambda i: (rb, n_pool + i)),
                  pl.BlockSpec((ms, gd), lambda i: (rb, 2 * n_pool + i)),
                  pl.BlockSpec((stc_lm.shape[0], nseq, gd), lambda i: (0, 0, i)),
                  pl.BlockSpec((std_lm.shape[0], nseq, gd), lambda i: (0, 0, i)),
                  pl.BlockSpec((k_c, gd), lambda i: (0, i)),
                  pl.BlockSpec((1, gd), lambda i: (0, i)),
                  pl.BlockSpec((1, dc), lambda i: (0, 0)),
                  pl.BlockSpec((1, dc), lambda i: (0, 0)),
                  pl.BlockSpec((1, gd, gd), lambda i: (i, 0, 0)),
                  pl.BlockSpec((1, gd), lambda i: (0, i))],
        out_specs=[pl.BlockSpec((ms, dc + dd), lambda i: (rb, 0)),
                   pl.BlockSpec((ms, gd), lambda i: (0, i))],
        out_shape=[jax.ShapeDtypeStruct(mixed.shape, BF16),
                   jax.ShapeDtypeStruct((ms, dc), F32)],
        scratch_shapes=[pltpu.VMEM((n_pool, ms, gd), F32),
                        pltpu.VMEM((n_pool, ms, gd), BF16)],
        input_output_aliases={0: 0},
        compiler_params=_params("arbitrary"),
        name="mix_cd_sample",
    )(mixed, proj, proj, proj, stc_lm, std_lm, conv_c, conv_c_b.reshape(1, dc), ln_g.reshape(1, dc),
      ln_b.reshape(1, dc), pool_w, pool_scale.reshape(1, dd))


def _router_kernel(x_ref, g_ref, r_ref, idx_ref, gate_ref):
    h = _rms(x_ref[...], g_ref[...])
    h_hi = h.astype(BF16)
    h_lo = (h - h_hi.astype(F32)).astype(BF16)
    r = r_ref[...]
    r_hi = r.astype(BF16)
    r_lo = (r - r_hi.astype(F32)).astype(BF16)
    logits = _dot(h_hi, r_hi) + (_dot(h_hi, r_lo) + _dot(h_lo, r_hi))
    n_exp = float(logits.shape[1])
    lane = lax.broadcasted_iota(I32, logits.shape, 1).astype(F32)
    m1 = jnp.max(logits, axis=-1, keepdims=True)
    i1 = jnp.min(jnp.where(logits == m1, lane, n_exp), axis=-1, keepdims=True)
    rest = jnp.where(lane == i1, -jnp.inf, logits)
    m2 = jnp.max(rest, axis=-1, keepdims=True)
    i2 = jnp.min(jnp.where(rest == m2, lane, n_exp), axis=-1, keepdims=True)
    e = jnp.exp(m2 - m1)
    two = lax.broadcasted_iota(I32, idx_ref.shape, 1)
    idx_ref[...] = jnp.where(two == 0, i1, i2).astype(I32)
    gate_ref[...] = jnp.where(two == 0, 1.0, e) / (1.0 + e)


def _router(x, g, router, tm):
    m, d = x.shape
    n_exp = router.shape[1]
    return pl.pallas_call(
        _router_kernel,
        grid=(m // tm,),
        in_specs=[pl.BlockSpec((tm, d), lambda i: (i, 0)),
                  pl.BlockSpec((1, d), lambda i: (0, 0)),
                  pl.BlockSpec((d, n_exp), lambda i: (0, 0))],
        out_specs=[pl.BlockSpec((tm, TOP_K), lambda i: (i, 0)),
                   pl.BlockSpec((tm, TOP_K), lambda i: (i, 0))],
        out_shape=[jax.ShapeDtypeStruct((m, TOP_K), I32),
                   jax.ShapeDtypeStruct((m, TOP_K), F32)],
        compiler_params=_params("parallel"),
        name="router",
    )(x, g.reshape(1, d), router)


def _dispatch_kernel(tile_ids_ref, n_tiles_ref, src_ref, x_hbm_ref, g_ref, o_ref, buf_ref, sem_ref, *, sub):
    k = pl.program_id(0)
    n_valid = n_tiles_ref[0]

    def row_copy(step, r, slot):
        tok = src_ref[tile_ids_ref[step] * sub + r]
        return pltpu.make_async_copy(x_hbm_ref.at[pl.ds(tok, 1)], buf_ref.at[slot, pl.ds(r, 1)],
                                     sem_ref.at[slot])

    def start(step, slot):
        def body(r, carry):
            row_copy(step, r, slot).start()
            return carry
        lax.fori_loop(0, sub, body, 0)

    @pl.when(jnp.logical_and(k == 0, n_valid > 0))
    def _():
        start(0, 0)

    @pl.when(k + 1 < n_valid)
    def _():
        start(k + 1, (k + 1) % 2)

    @pl.when(k < n_valid)
    def _():
        slot = k % 2
        pltpu.make_async_copy(x_hbm_ref.at[pl.ds(0, sub)], buf_ref.at[slot], sem_ref.at[slot]).wait()
        o_ref[...] = _rms(buf_ref[slot], g_ref[...]).astype(BF16)


def _dispatch(x, g, tile_ids, n_tiles, src, n_slots, sub):
    m, d = x.shape
    max_tiles = tile_ids.shape[0]
    return pl.pallas_call(
        functools.partial(_dispatch_kernel, sub=sub),
        grid_spec=pltpu.PrefetchScalarGridSpec(
            num_scalar_prefetch=3,
            grid=(max_tiles,),
            in_specs=[pl.BlockSpec(memory_space=pl.ANY),
                      pl.BlockSpec((1, d), lambda k, ids, nt, src: (0, 0))],
            out_specs=pl.BlockSpec((sub, d), lambda k, ids, nt, src: (ids[k], 0)),
            scratch_shapes=[pltpu.VMEM((2, sub, d), F32), pltpu.SemaphoreType.DMA((2,))]),
        out_shape=jax.ShapeDtypeStruct((n_slots, d), BF16),
        compiler_params=_params("arbitrary"),
        name="moe_dispatch",
    )(tile_ids, n_tiles, src, x, g.reshape(1, d))


def _experts_kernel(bexp_ref, nsub_ref, nused_ref, xs_ref, w1_ref, w3_ref, w2_ref, o_ref,
                    w1b_ref, w3b_ref, w2b_ref, *, sub):
    s = pl.program_id(0)
    j = pl.program_id(1)

    @pl.when(s < nused_ref[0])
    def _():
        @pl.when(j == 0)
        def _():
            o_ref[...] = jnp.zeros_like(o_ref)

        w1b_ref[...] = w1_ref[0].astype(BF16)
        w3b_ref[...] = w3_ref[0].astype(BF16)
        w2b_ref[...] = w2_ref[0].astype(BF16)

        def body(i, carry):
            rows = pl.ds(pl.multiple_of(i * sub, sub), sub)
            x = xs_ref[rows, :]
            a = _dot(x, w1b_ref[...])
            b = _dot(x, w3b_ref[...])
            act = (_silu(a) * b).astype(BF16)
            o_ref[rows, :] += _dot(act, w2b_ref[...])
            return carry
        lax.fori_loop(0, nsub_ref[s], body, 0)


def _experts(xs, w1, w3, w2, bexp, nsub, nused, blk, sub, tf):
    n_slots, d = xs.shape
    n_blocks = n_slots // blk
    f = w1.shape[2]
    n_f = f // tf

    def last(s, nu):
        return jnp.minimum(s, jnp.maximum(nu[0] - 1, 0))

    def jj(s, j, nu):
        return jnp.where(s < nu[0], j, n_f - 1)

    return pl.pallas_call(
        functools.partial(_experts_kernel, sub=sub),
        grid_spec=pltpu.PrefetchScalarGridSpec(
            num_scalar_prefetch=3,
            grid=(n_blocks, n_f),
            in_specs=[pl.BlockSpec((blk, d), lambda s, j, be, ns, nu: (last(s, nu), 0)),
                      pl.BlockSpec((1, d, tf), lambda s, j, be, ns, nu: (be[s], 0, jj(s, j, nu))),
                      pl.BlockSpec((1, d, tf), lambda s, j, be, ns, nu: (be[s], 0, jj(s, j, nu))),
                      pl.BlockSpec((1, tf, d), lambda s, j, be, ns, nu: (be[s], jj(s, j, nu), 0))],
            out_specs=pl.BlockSpec((blk, d), lambda s, j, be, ns, nu: (last(s, nu), 0)),
            scratch_shapes=[pltpu.VMEM((d, tf), BF16), pltpu.VMEM((d, tf), BF16),
                            pltpu.VMEM((tf, d), BF16)]),
        out_shape=jax.ShapeDtypeStruct((n_slots, d), F32),
        compiler_params=_params("arbitrary", "arbitrary"),
        name="moe_experts",
    )(bexp, nsub, nused, xs, w1, w3, w2)


def _combine_kernel(pos_ref, x_ref, gate_ref, g_ref, ys_hbm_ref, o_ref, buf_ref, sem_ref, *, tm):
    i = pl.program_id(0)
    n = pl.num_programs(0)

    def start(step, slot):
        def body(r, carry):
            for kk in range(TOP_K):
                p = pos_ref[(step * tm + r) * TOP_K + kk]
                pltpu.make_async_copy(ys_hbm_ref.at[pl.ds(p, 1)], buf_ref.at[slot, kk, pl.ds(r, 1)],
                                      sem_ref.at[slot]).start()
            return carry
        lax.fori_loop(0, tm, body, 0)

    @pl.when(i == 0)
    def _():
        start(0, 0)

    @pl.when(i + 1 < n)
    def _():
        start(i + 1, (i + 1) % 2)

    slot = i % 2
    for kk in range(TOP_K):
        pltpu.make_async_copy(ys_hbm_ref.at[pl.ds(0, tm)], buf_ref.at[slot, kk], sem_ref.at[slot]).wait()
    y = x_ref[...]
    for kk in range(TOP_K):
        y = y + gate_ref[:, kk:kk + 1] * buf_ref[slot, kk]
    o_ref[...] = _rms(y, g_ref[...])


def _combine(x, gates, pos, ys, g, tm):
    m, d = x.shape
    return pl.pallas_call(
        functools.partial(_combine_kernel, tm=tm),
        grid_spec=pltpu.PrefetchScalarGridSpec(
            num_scalar_prefetch=1,
            grid=(m // tm,),
            in_specs=[pl.BlockSpec((tm, d), lambda i, pos: (i, 0)),
                      pl.BlockSpec((tm, TOP_K), lambda i, pos: (i, 0)),
                      pl.BlockSpec((1, d), lambda i, pos: (0, 0)),
                      pl.BlockSpec(memory_space=pl.ANY)],
            out_specs=pl.BlockSpec((tm, d), lambda i, pos: (i, 0)),
            scratch_shapes=[pltpu.VMEM((2, TOP_K, tm, d), F32), pltpu.SemaphoreType.DMA((2,))]),
        out_shape=jax.ShapeDtypeStruct((m, d), F32),
        compiler_params=_params("arbitrary"),
        name="moe_combine",
    )(pos, x, gates, g.reshape(1, d), ys)


def _moe_plan(top_i, n_exp, blk, sub, n_blocks, max_tiles):
    m = top_i.shape[0]
    e_flat = top_i.reshape(-1)
    onehot = (e_flat[:, None] == jnp.arange(n_exp, dtype=I32)[None, :]).astype(I32)
    csum = jnp.cumsum(onehot, axis=0)
    rank = jnp.sum((csum - 1) * onehot, axis=1)
    counts = csum[-1]
    blocks_per = (counts + blk - 1) // blk
    block_end = jnp.cumsum(blocks_per)
    block_start = block_end - blocks_per
    n_used = block_end[-1]
    pos = block_start[e_flat] * blk + rank
    src = jnp.zeros((n_blocks * blk,), I32).at[pos].set(jnp.arange(m * TOP_K, dtype=I32) // TOP_K)

    s = jnp.arange(n_blocks, dtype=I32)
    bexp = jnp.minimum(jnp.sum((s[:, None] >= block_end[None, :]).astype(I32), axis=1), n_exp - 1)
    valid = jnp.clip(counts[bexp] - (s - block_start[bexp]) * blk, 0, blk)
    valid = jnp.where(s < n_used, valid, 0)
    nsub = (valid + sub - 1) // sub
    last_exp = bexp[jnp.maximum(n_used - 1, 0)]
    bexp = jnp.where(s < n_used, bexp, last_exp)

    per_blk = blk // sub
    t = jnp.arange(n_blocks * per_blk, dtype=I32)
    tile_valid = (t % per_blk) < nsub[t // per_blk]
    n_tiles = jnp.sum(tile_valid.astype(I32))
    order = jnp.argsort(jnp.logical_not(tile_valid), stable=True).astype(I32)[:max_tiles]
    last_tile = order[jnp.maximum(n_tiles - 1, 0)]
    tile_ids = jnp.where(jnp.arange(max_tiles, dtype=I32) < n_tiles, order, last_tile)
    return pos, src, bexp, nsub, n_used.reshape(1), tile_ids, n_tiles.reshape(1)


def _moe(x, g_ffn, g_final, router, w1, w3, w2, blk, sub, tf, tm_r, tm_c):
    m, d = x.shape
    n_exp = router.shape[1]
    top_i, gates = _router(x, g_ffn, router, tm_r)
    n_blocks = (m * TOP_K + n_exp * (blk - 1)) // blk
    max_tiles = (m * TOP_K + n_exp * (sub - 1)) // sub
    pos, src, bexp, nsub, nused, tile_ids, n_tiles = _moe_plan(top_i, n_exp, blk, sub, n_blocks, max_tiles)
    xs = _dispatch(x, g_ffn, tile_ids, n_tiles, src, n_blocks * blk, sub)
    ys = _experts(xs, w1, w3, w2, bexp, nsub, nused, blk, sub, tf)
    return _combine(x, gates, pos, ys, g_final, tm_c)


def kernel(x_prompt, x_sample, state_conv_a, state_conv_c, state_pool_d, norm_mix_g, norm_ffn_g, norm_final_g, w_in_ab, conv_a, ln_b_g, ln_b_b, w_s, b_s, w_out_ab, ffn_w1, ffn_w3, ffn_w2, w_in_cd, conv_c, conv_c_b, ln_c_g, ln_c_b, pool_w, pool_scale, w_out_cd, router, exp_w1, exp_w3, exp_w2):
    n_seq, seq_len, d = x_prompt.shape
    nseq_s, dl, _ = x_sample.shape
    mp, ms = n_seq * seq_len, nseq_s * dl
    m = mp + ms
    assert norm_mix_g.shape[0] == 2, "two layers: conv/chunk-MLP + dense FFN, then conformer/pool + MoE"
    assert seq_len % CHUNK == 0 and dl <= CHUNK and mp % ms == 0

    tm = _pick(m, (1088, 544, 512, 256, 128, 64))
    tm_ffn = _pick(m, (544, 512, 256, 128, 64))
    t_rows = _pick(seq_len, (256, 128))
    tn = 512
    tf = _pick(ffn_w1.shape[2], (512, 256))
    tf_e = 256
    if m * TOP_K >= 8 * 1280:
        blk, sub = 1280, 256
    else:
        blk, sub = 256, 128
    tm_r = _pick(m, (544, 512, 256, 128, 64))
    tm_c = _pick(m, (256, 128, 64))

    x = jnp.concatenate([x_prompt.reshape(mp, d),
                         jnp.transpose(x_sample, (1, 0, 2)).reshape(ms, d)], axis=0)

    proj = _norm_proj(x, norm_mix_g[0], w_in_ab[0], tm, tn)
    mixed, a_p, v_p = _mix_ab_prompt(proj, conv_a[0], ln_b_g[0], ln_b_b[0], w_s[0], b_s[0],
                                     n_seq, seq_len, t_rows)
    mixed, a_s, v_s = _mix_ab_sample(mixed, proj, jnp.transpose(state_conv_a[0], (1, 0, 2)),
                                     conv_a[0], ln_b_g[0], ln_b_b[0], w_s[0], b_s[0], mp, dl, nseq_s)
    x = _proj_res(mixed, w_out_ab[0], x, tm, tn)
    x = _ffn(x, norm_ffn_g[0], ffn_w1[0], ffn_w3[0], ffn_w2[0], tm_ffn, tf)

    proj = _norm_proj(x, norm_mix_g[1], w_in_cd[0], tm, tn)
    mixed, c_p, d_p = _mix_cd_prompt(proj, conv_c[0], conv_c_b[0], ln_c_g[0], ln_c_b[0], pool_w[0],
                                     pool_scale[0], n_seq, seq_len, t_rows)
    mixed, g_s = _mix_cd_sample(mixed, proj, jnp.transpose(state_conv_c[0], (1, 0, 2)),
                                jnp.transpose(state_pool_d[0], (1, 0, 2)), conv_c[0], conv_c_b[0],
                                ln_c_g[0], ln_c_b[0], pool_w[0], pool_scale[0], mp, dl, nseq_s)
    x = _proj_res(mixed, w_out_cd[0], x, tm, tn)
    y = _moe(x, norm_ffn_g[1], norm_final_g, router[0], exp_w1[0], exp_w3[0], exp_w2[0],
             blk, sub, tf_e, tm_r, tm_c)

    dc = conv_c.shape[2]
    y_prompt = y[:mp].reshape(n_seq, seq_len, d)
    y_sample = jnp.transpose(y[mp:].reshape(dl, nseq_s, d), (1, 0, 2))
    new_a_s = jnp.transpose(a_s, (1, 0, 2))
    new_v_s = jnp.transpose(v_s.reshape(dl, nseq_s, -1), (1, 0, 2))
    g_rows = jnp.transpose(g_s.reshape(dl, nseq_s, dc), (1, 0, 2))
    p_rows = jnp.transpose(proj[mp:, 2 * dc:].reshape(dl, nseq_s, -1), (1, 0, 2))
    n_c = state_conv_c.shape[2]
    n_d = state_pool_d.shape[2]
    new_c_s = jnp.concatenate([state_conv_c[0], g_rows], axis=1)[:, -n_c:]
    new_d_s = jnp.concatenate([state_pool_d[0], p_rows], axis=1)[:, -n_d:]
    return (y_prompt, y_sample,
            a_p[None], new_a_s[None],
            v_p[None], new_v_s[None],
            c_p[None], new_c_s[None],
            d_p[None], new_d_s[None])
```

```python
import functools

import jax
import jax.numpy as jnp
from jax import lax
from jax.experimental import pallas as pl
from jax.experimental.pallas import tpu as pltpu

F32 = jnp.float32
BF16 = jnp.bfloat16
I32 = jnp.int32

EPS = 1e-6
PAST_LEN = 16384
CHUNK = 128
HEAD_B = 128
POOL_WINDOWS = (2, 4, 8, 16)
TOP_K = 2

VMEM_LIMIT_BYTES = 60 * 1024 * 1024
SUBLANES = 8
LANES = 128


def _params(*sem):
    return pltpu.CompilerParams(dimension_semantics=sem, vmem_limit_bytes=VMEM_LIMIT_BYTES)


def _pick(n, candidates):
    for c in candidates:
        if n % c == 0:
            return c
    raise ValueError(f"no tile in {candidates} divides {n}")


def _rms(x, g):
    return x * lax.rsqrt(jnp.mean(x * x, axis=-1, keepdims=True) + EPS) * g


def _layernorm(x, g, b):
    mu = jnp.mean(x, axis=-1, keepdims=True)
    xc = x - mu
    return xc * lax.rsqrt(jnp.mean(xc * xc, axis=-1, keepdims=True) + EPS) * g + b


def _silu(x):
    return x * jax.nn.sigmoid(x)


def _dot(a, b):
    return jnp.dot(a, b, preferred_element_type=F32)


def _mix_ab_prompt_kernel(bg_ref, cg_ref, xin_ref, u_ref, v_ref, cgh_ref, xinh_ref,
                          cw_ref, lng_ref, lnb_ref, ws_ref, bsb_ref,
                          o_ref, astate_ref, vstate_ref, abuf_ref):
    t = pl.program_id(1)
    t_rows, da = bg_ref.shape
    n_heads = ws_ref.shape[0]
    halo = SUBLANES

    a = cg_ref[...] * xin_ref[...]
    prev = jnp.where(t == 0, 0.0, cgh_ref[...] * xinh_ref[...])
    abuf_ref[0:halo, :] = prev
    abuf_ref[halo:halo + t_rows, :] = a
    conv = (cw_ref[0:1, :] * abuf_ref[halo - 2:halo - 2 + t_rows, :]
            + cw_ref[1:2, :] * abuf_ref[halo - 1:halo - 1 + t_rows, :]
            + cw_ref[2:3, :] * a)
    o_ref[:, 0:da] = (bg_ref[...] * conv).astype(BF16)
    astate_ref[0] = abuf_ref[halo + t_rows - 2:halo + t_rows, :]

    row = lax.broadcasted_iota(I32, (CHUNK, CHUNK), 0)
    col = lax.broadcasted_iota(I32, (CHUNK, CHUNK), 1)
    lower = row >= col
    for h in range(n_heads):
        cols = slice(h * HEAD_B, (h + 1) * HEAD_B)
        vn = _layernorm(jax.nn.gelu(v_ref[:, cols]), lng_ref[:, cols], lnb_ref[:, cols])
        vstate_ref[0, :, cols] = vn[t_rows - CHUNK:t_rows]
        w = jnp.where(lower, ws_ref[h], 0.0).astype(BF16)
        for c in range(t_rows // CHUNK):
            rows = slice(c * CHUNK, (c + 1) * CHUNK)
            mix = _dot(w, vn[rows].astype(BF16)) + bsb_ref[h]
            o_ref[rows, da + h * HEAD_B:da + (h + 1) * HEAD_B] = (
                jax.nn.gelu(u_ref[rows, cols]) * mix).astype(BF16)


def _mix_ab_prompt(proj, conv_a, ln_g, ln_b, w_s, b_s, n_seq, seq_len, t_rows):
    m = proj.shape[0]
    da = conv_a.shape[1]
    db = ln_g.shape[0]
    n_heads = w_s.shape[0]
    tiles = seq_len // t_rows
    halo_blocks = t_rows // SUBLANES

    def col_spec(c):
        return pl.BlockSpec((t_rows, da), lambda b, t: (b * tiles + t, c))

    def halo_spec(c):
        return pl.BlockSpec(
            (SUBLANES, da),
            lambda b, t: (jnp.maximum((b * tiles + t) * halo_blocks - 1, 0), c))

    def full(shape):
        return pl.BlockSpec(shape, lambda b, t: (0,) * len(shape))

    bsb = jnp.broadcast_to(b_s[:, :CHUNK, None], (n_heads, CHUNK, HEAD_B))
    return pl.pallas_call(
        _mix_ab_prompt_kernel,
        grid=(n_seq, tiles),
        in_specs=[col_spec(0), col_spec(1), col_spec(2), col_spec(3), col_spec(4),
                  halo_spec(1), halo_spec(2),
                  full((conv_a.shape[0], da)), full((1, db)), full((1, db)),
                  full((n_heads, CHUNK, CHUNK)), full((n_heads, CHUNK, HEAD_B))],
        out_specs=[pl.BlockSpec((t_rows, da + db), lambda b, t: (b * tiles + t, 0)),
                   pl.BlockSpec((1, 2, da), lambda b, t: (b, 0, 0)),
                   pl.BlockSpec((1, CHUNK, db), lambda b, t: (b, 0, 0))],
        out_shape=[jax.ShapeDtypeStruct((m, da + db), BF16),
                   jax.ShapeDtypeStruct((n_seq, 2, da), F32),
                   jax.ShapeDtypeStruct((n_seq, CHUNK, db), F32)],
        scratch_shapes=[pltpu.VMEM((SUBLANES + t_rows, da), F32)],
        compiler_params=_params("parallel", "arbitrary"),
        name="mix_ab_prompt",
    )(proj, proj, proj, proj, proj, proj, proj, conv_a, ln_g.reshape(1, db), ln_b.reshape(1, db),
      w_s[:, :CHUNK, :CHUNK], bsb)


def _mix_ab_sample_kernel(mixed_hbm_ref, bg_ref, cg_ref, xin_ref, u_ref, v_ref, st_ref,
                          cw_ref, lng_ref, lnb_ref, wsl_ref, bsl_ref,
                          o_ref, astate_ref, vstate_ref, *, dl, nseq):
    del mixed_hbm_ref
    da = bg_ref.shape[1]
    n_heads = lng_ref.shape[1] // HEAD_B
    n_state = st_ref.shape[0]

    def slab(l):
        return slice(l * nseq, (l + 1) * nseq)

    a_pad = [st_ref[k] for k in range(n_state)]
    a_pad += [cg_ref[slab(l), :] * xin_ref[slab(l), :] for l in range(dl)]
    for l in range(dl):
        conv = (cw_ref[0:1, :] * a_pad[l] + cw_ref[1:2, :] * a_pad[l + 1]
                + cw_ref[2:3, :] * a_pad[l + 2])
        o_ref[slab(l), 0:da] = (bg_ref[slab(l), :] * conv).astype(BF16)
    for k in range(n_state):
        astate_ref[k] = a_pad[dl + k]

    for h in range(n_heads):
        cols = slice(h * HEAD_B, (h + 1) * HEAD_B)
        vn = _layernorm(jax.nn.gelu(v_ref[:, cols]), lng_ref[:, cols], lnb_ref[:, cols])
        vstate_ref[:, cols] = vn
        for i in range(dl):
            mix = bsl_ref[i:i + 1, cols]
            for j in range(i + 1):
                mix = mix + wsl_ref[i * dl + j:i * dl + j + 1, cols] * vn[slab(j)]
            o_ref[slab(i), da + h * HEAD_B:da + (h + 1) * HEAD_B] = (
                jax.nn.gelu(u_ref[slab(i), cols]) * mix).astype(BF16)


def _mix_ab_sample(mixed, proj, state_lm, conv_a, ln_g, ln_b, w_s, b_s, row0, dl, nseq):
    ms = dl * nseq
    da = conv_a.shape[1]
    db = ln_g.shape[0]
    n_heads = w_s.shape[0]
    rb = row0 // ms

    def col_spec(c):
        return pl.BlockSpec((ms, da), lambda i: (rb, c))

    def full(shape):
        return pl.BlockSpec(shape, lambda i: (0,) * len(shape))

    wsl = jnp.repeat(jnp.transpose(w_s[:, :dl, :dl], (1, 2, 0)).reshape(dl * dl, n_heads), HEAD_B, axis=1)
    bsl = jnp.repeat(jnp.transpose(b_s[:, :dl], (1, 0)), HEAD_B, axis=1)
    return pl.pallas_call(
        functools.partial(_mix_ab_sample_kernel, dl=dl, nseq=nseq),
        grid=(1,),
        in_specs=[pl.BlockSpec(memory_space=pl.ANY),
                  col_spec(0), col_spec(1), col_spec(2), col_spec(3), col_spec(4),
                  full(state_lm.shape), full((conv_a.shape[0], da)), full((1, db)), full((1, db)),
                  full((dl * dl, db)), full((dl, db))],
        out_specs=[pl.BlockSpec((ms, da + db), lambda i: (rb, 0)),
                   full(state_lm.shape),
                   full((ms, db))],
        out_shape=[jax.ShapeDtypeStruct(mixed.shape, BF16),
                   jax.ShapeDtypeStruct(state_lm.shape, F32),
                   jax.ShapeDtypeStruct((ms, db), F32)],
        input_output_aliases={0: 0},
        compiler_params=_params("arbitrary"),
        name="mix_ab_sample",
    )(mixed, proj, proj, proj, proj, proj, state_lm, conv_a, ln_g.reshape(1, db), ln_b.reshape(1, db),
      wsl, bsl)


def _mix_cd_prompt_kernel(ga_ref, gb_ref, p_ref, gah_ref, gbh_ref, ph_ref,
                          cw_ref, cb_ref, lng_ref, lnb_ref, pw_ref, ps_ref,
                          o_ref, cstate_ref, dstate_ref, gbuf_ref, pbuf_ref, cbuf_ref):
    t = pl.program_id(1)
    t_rows, dc = ga_ref.shape
    k_c = cw_ref.shape[0]
    c_halo = gah_ref.shape[0]
    d_halo = ph_ref.shape[0]
    gd = pw_ref.shape[1]

    g = ga_ref[...] * jax.nn.sigmoid(gb_ref[...])
    gbuf_ref[0:c_halo, :] = jnp.where(t == 0, 0.0, gah_ref[...] * jax.nn.sigmoid(gbh_ref[...]))
    gbuf_ref[c_halo:c_halo + t_rows, :] = g
    first = c_halo - (k_c - 1)
    for cblk in range(dc // LANES):
        cols = slice(cblk * LANES, (cblk + 1) * LANES)
        acc = cb_ref[:, cols] + cw_ref[k_c - 1:k_c, cols] * gbuf_ref[c_halo:c_halo + t_rows, cols]
        for k in range(k_c - 1):
            acc = acc + cw_ref[k:k + 1, cols] * gbuf_ref[first + k:first + k + t_rows, cols]
        cbuf_ref[:, cols] = acc
    o_ref[:, 0:dc] = _silu(_layernorm(cbuf_ref[...], lng_ref[...], lnb_ref[...])).astype(BF16)
    cstate_ref[0] = gbuf_ref[c_halo + t_rows - (k_c - 1):c_halo + t_rows, :]

    pbuf_ref[0:d_halo, :] = jnp.where(t == 0, 0.0, ph_ref[...])
    pbuf_ref[d_halo:d_halo + t_rows, :] = p_ref[...]
    pos = t * t_rows + lax.broadcasted_iota(I32, (t_rows, 1), 0)
    for gi, w in enumerate(POOL_WINDOWS):
        cols = slice(gi * gd, (gi + 1) * gd)
        cur = p_ref[:, cols]
        s = cur
        for k in range(1, w):
            s = s + pbuf_ref[d_halo - k:d_halo - k + t_rows, cols]
        cnt = jnp.minimum(w, pos + 1).astype(F32)
        diff = (s / cnt - cur).astype(BF16)
        d = _dot(diff, pw_ref[gi].astype(BF16)) * ps_ref[:, cols]
        o_ref[:, dc + gi * gd:dc + (gi + 1) * gd] = d.astype(BF16)
    n_keep = dstate_ref.shape[1]
    dstate_ref[0] = pbuf_ref[d_halo + t_rows - n_keep:d_halo + t_rows, :]


def _mix_cd_prompt(proj, conv_c, conv_c_b, ln_g, ln_b, pool_w, pool_scale, n_seq, seq_len, t_rows):
    m = proj.shape[0]
    k_c, dc = conv_c.shape
    dd = pool_scale.shape[0]
    n_pool, gd = pool_w.shape[0], pool_w.shape[1]
    tiles = seq_len // t_rows
    c_halo = -(-(k_c - 1) // SUBLANES) * SUBLANES
    d_halo = -(-(max(POOL_WINDOWS) - 1) // SUBLANES) * SUBLANES
    assert t_rows % c_halo == 0 and t_rows % d_halo == 0 and dc == dd

    def col_spec(c):
        return pl.BlockSpec((t_rows, dc), lambda b, t: (b * tiles + t, c))

    def halo_spec(c, rows):
        per = t_rows // rows
        return pl.BlockSpec((rows, dc),
                            lambda b, t: (jnp.maximum((b * tiles + t) * per - 1, 0), c))

    def full(shape):
        return pl.BlockSpec(shape, lambda b, t: (0,) * len(shape))

    return pl.pallas_call(
        _mix_cd_prompt_kernel,
        grid=(n_seq, tiles),
        in_specs=[col_spec(0), col_spec(1), col_spec(2),
                  halo_spec(0, c_halo), halo_spec(1, c_halo), halo_spec(2, d_halo),
                  full((k_c, dc)), full((1, dc)), full((1, dc)), full((1, dc)),
                  full((n_pool, gd, gd)), full((1, dd))],
        out_specs=[pl.BlockSpec((t_rows, dc + dd), lambda b, t: (b * tiles + t, 0)),
                   pl.BlockSpec((1, k_c - 1, dc), lambda b, t: (b, 0, 0)),
                   pl.BlockSpec((1, max(POOL_WINDOWS) - 1, dd), lambda b, t: (b, 0, 0))],
        out_shape=[jax.ShapeDtypeStruct((m, dc + dd), BF16),
                   jax.ShapeDtypeStruct((n_seq, k_c - 1, dc), F32),
                   jax.ShapeDtypeStruct((n_seq, max(POOL_WINDOWS) - 1, dd), F32)],
        scratch_shapes=[pltpu.VMEM((c_halo + t_rows, dc), F32),
                        pltpu.VMEM((d_halo + t_rows, dd), F32),
                        pltpu.VMEM((t_rows, dc), F32)],
        compiler_params=_params("parallel", "arbitrary"),
        name="mix_cd_prompt",
    )(proj, proj, proj, proj, proj, proj, conv_c, conv_c_b.reshape(1, dc), ln_g.reshape(1, dc),
      ln_b.reshape(1, dc), pool_w, pool_scale.reshape(1, dd))


def _mix_cd_sample_kernel(mixed_hbm_ref, ga_ref, gb_ref, p_ref, stc_ref, std_ref,
                          cw_ref, cb_ref, lng_ref, lnb_ref, pw_ref, ps_ref,
                          o_ref, g_ref, cbuf_ref, dbuf_ref, *, dl, nseq, start_pos):
    del mixed_hbm_ref
    gi = pl.program_id(0)
    n_groups = pl.num_programs(0)
    gd = ga_ref.shape[1]
    k_c = cw_ref.shape[0]
    n_c = stc_ref.shape[0]
    n_d = std_ref.shape[0]

    def slab(l):
        return slice(l * nseq, (l + 1) * nseq)

    g = ga_ref[...] * jax.nn.sigmoid(gb_ref[...])
    g_ref[...] = g
    c_pad = [stc_ref[k] for k in range(n_c)] + [g[slab(l)] for l in range(dl)]
    d_pad = [std_ref[k] for k in range(n_d)] + [p_ref[slab(l), :] for l in range(dl)]
    for l in range(dl):
        acc = cb_ref[...] + cw_ref[0:1, :] * c_pad[l]
        for k in range(1, k_c):
            acc = acc + cw_ref[k:k + 1, :] * c_pad[l + k]
        cbuf_ref[gi, slab(l), :] = acc

        cur = d_pad[n_d + l]
        s = cur
        pooled = jnp.zeros_like(cur)
        for k in range(1, max(POOL_WINDOWS) + 1):
            if k in POOL_WINDOWS:
                cnt = float(min(k, start_pos + l + 1))
                pooled = jnp.where(gi == POOL_WINDOWS.index(k), s / cnt, pooled)
            if k < max(POOL_WINDOWS):
                s = s + d_pad[n_d + l - k]
        diff = (pooled - cur).astype(BF16)
        dbuf_ref[gi, slab(l), :] = (_dot(diff, pw_ref[0].astype(BF16)) * ps_ref[...]).astype(BF16)

    @pl.when(gi == n_groups - 1)
    def _():
        n_blk = cbuf_ref.shape[0]
        dc = n_blk * gd
        total = cbuf_ref[0].sum(axis=-1, keepdims=True)
        for b in range(1, n_blk):
            total = total + cbuf_ref[b].sum(axis=-1, keepdims=True)
        mu = total / dc
        var = None
        for b in range(n_blk):
            xc = cbuf_ref[b] - mu
            sq = (xc * xc).sum(axis=-1, keepdims=True)
            var = sq if var is None else var + sq
        rstd = lax.rsqrt(var / dc + EPS)
        for b in range(n_blk):
            cols = slice(b * gd, (b + 1) * gd)
            y = (cbuf_ref[b] - mu) * rstd * lng_ref[:, cols] + lnb_ref[:, cols]
            o_ref[:, cols] = _silu(y).astype(BF16)
            o_ref[:, dc + b * gd:dc + (b + 1) * gd] = dbuf_ref[b]


def _mix_cd_sample(mixed, proj, stc_lm, std_lm, conv_c, conv_c_b, ln_g, ln_b, pool_w, pool_scale,
                   row0, dl, nseq):
    ms = dl * nseq
    k_c, dc = conv_c.shape
    dd = pool_scale.shape[0]
    n_pool, gd = pool_w.shape[0], pool_w.shape[1]
    assert dc == dd == n_pool * gd and n_pool == len(POOL_WINDOWS)
    rb = row0 // ms

    return pl.pallas_call(
        functools.partial(_mix_cd_sample_kernel, dl=dl, nseq=nseq, start_pos=PAST_LEN),
        grid=(n_pool,),
        in_specs=[pl.BlockSpec(memory_space=pl.ANY),
                  pl.BlockSpec((ms, gd), lambda i: (rb, i)),
                  pl.BlockSpec((ms, gd), lambda i: (rb, n_pool + i)),
                  pl.BlockSpec((ms, gd), lambda i: (rb, 2 * n_pool + i)),
                  pl.BlockSpec((stc_lm.shape[0], nseq, gd), lambda i: (0, 0, i)),
                  pl.BlockSpec((std_lm.shape[0], nseq, gd), lambda i: (0, 0, i)),
                  pl.BlockSpec((k_c, gd), lambda i: (0, i)),
                  pl.BlockSpec((1, gd), lambda i: (0, i)),
                  pl.BlockSpec((1, dc), lambda i: (0, 0)),
                  pl.BlockSpec((1, dc), lambda i: (0, 0)),
                  pl.BlockSpec((1, gd, gd), lambda i: (i, 0, 0)),
                  pl.BlockSpec((1, gd), lambda i: (0, i))],
        out_specs=[pl.BlockSpec((ms, dc + dd), lambda i: (rb, 0)),
                   pl.BlockSpec((ms, gd), lambda i: (0, i))],
        out_shape=[jax.ShapeDtypeStruct(mixed.shape, BF16),
                   jax.ShapeDtypeStruct((ms, dc), F32)],
        scratch_shapes=[pltpu.VMEM((n_pool, ms, gd), F32),
                        pltpu.VMEM((n_pool, ms, gd), BF16)],
        input_output_aliases={0: 0},
        compiler_params=_params("arbitrary"),
        name="mix_cd_sample",
    )(mixed, proj, proj, proj, stc_lm, std_lm, conv_c, conv_c_b.reshape(1, dc), ln_g.reshape(1, dc),
      ln_b.reshape(1, dc), pool_w, pool_scale.reshape(1, dd))


def _norm_in_kernel(xp_ref, xs_ref, g_ref, o_ref, *, n_a):
    i = pl.program_id(0)

    @pl.when(i < n_a)
    def _():
        o_ref[...] = _rms(xp_ref[...], g_ref[...]).astype(BF16)

    @pl.when(i >= n_a)
    def _():
        o_ref[...] = _rms(xs_ref[...], g_ref[...]).astype(BF16)


def _norm_in(xp, xs, g, tm):
    mp, d = xp.shape
    ms = xs.shape[0]
    n_a, n_b = mp // tm, ms // tm
    return pl.pallas_call(
        functools.partial(_norm_in_kernel, n_a=n_a),
        grid=(n_a + n_b,),
        in_specs=[pl.BlockSpec((tm, d), lambda i: (jnp.minimum(i, n_a - 1), 0)),
                  pl.BlockSpec((tm, d), lambda i: (jnp.maximum(i - n_a, 0), 0)),
                  pl.BlockSpec((1, d), lambda i: (0, 0))],
        out_specs=pl.BlockSpec((tm, d), lambda i: (i, 0)),
        out_shape=jax.ShapeDtypeStruct((mp + ms, d), BF16),
        compiler_params=_params("arbitrary"),
        name="norm_in",
    )(xp, xs, g.reshape(1, d))


def _matmul_kernel(a_ref, w_ref, o_ref):
    o_ref[...] = _dot(a_ref[...], w_ref[...].astype(BF16))


def _matmul(a, w, tm, tn):
    m, k = a.shape
    n = w.shape[1]
    return pl.pallas_call(
        _matmul_kernel,
        grid=(m // tm, n // tn),
        in_specs=[pl.BlockSpec((tm, k), lambda i, j: (i, 0)),
                  pl.BlockSpec((k, tn), lambda i, j: (0, j))],
        out_specs=pl.BlockSpec((tm, tn), lambda i, j: (i, j)),
        out_shape=jax.ShapeDtypeStruct((m, n), F32),
        compiler_params=_params("parallel", "arbitrary"),
        name="in_proj",
    )(a, w)


def _top2(h, r, idx_ref, gate_ref):
    h_hi = h.astype(BF16)
    h_lo = (h - h_hi.astype(F32)).astype(BF16)
    r_hi = r.astype(BF16)
    r_lo = (r - r_hi.astype(F32)).astype(BF16)
    logits = _dot(h_hi, r_hi) + (_dot(h_hi, r_lo) + _dot(h_lo, r_hi))
    n_exp = float(logits.shape[1])
    lane = lax.broadcasted_iota(I32, logits.shape, 1).astype(F32)
    m1 = jnp.max(logits, axis=-1, keepdims=True)
    i1 = jnp.min(jnp.where(logits == m1, lane, n_exp), axis=-1, keepdims=True)
    rest = jnp.where(lane == i1, -jnp.inf, logits)
    m2 = jnp.max(rest, axis=-1, keepdims=True)
    i2 = jnp.min(jnp.where(rest == m2, lane, n_exp), axis=-1, keepdims=True)
    e = jnp.exp(m2 - m1)
    two = lax.broadcasted_iota(I32, idx_ref.shape, 1)
    idx_ref[...] = jnp.where(two == 0, i1, i2).astype(I32)
    gate_ref[...] = jnp.where(two == 0, 1.0, e) / (1.0 + e)


def _out_proj_kernel(a_ref, w_ref, ra_ref, rb_ref, g_ref, *rest, n_a, with_router):
    i = pl.program_id(0)
    if with_router:
        r_ref, x_ref, idx_ref, gate_ref = rest
    else:
        x_ref, h_ref = rest
    y = _dot(a_ref[...], w_ref[...])

    @pl.when(i < n_a)
    def _():
        x_ref[...] = ra_ref[...] + y

    @pl.when(i >= n_a)
    def _():
        x_ref[...] = rb_ref[...] + y

    h = _rms(x_ref[...], g_ref[...])
    if with_router:
        _top2(h, r_ref[...], idx_ref, gate_ref)
    else:
        h_ref[...] = h.astype(BF16)


def _out_proj(a, w_bf16, res_a, res_b, g, router, tm):
    m, k = a.shape
    d = w_bf16.shape[1]
    n_a = res_a.shape[0] // tm
    n_b = m // tm - n_a
    assert n_a * tm == res_a.shape[0] and (n_b == 0 or n_b * tm == res_b.shape[0])
    with_router = router is not None
    in_specs = [pl.BlockSpec((tm, k), lambda i: (i, 0)),
                pl.BlockSpec((k, d), lambda i: (0, 0), pipeline_mode=pl.Buffered(1)),
                pl.BlockSpec((tm, d), lambda i: (jnp.minimum(i, n_a - 1), 0)),
                pl.BlockSpec((tm, d), lambda i: (jnp.maximum(i - n_a, 0), 0)),
                pl.BlockSpec((1, d), lambda i: (0, 0))]
    args = [a, w_bf16, res_a, res_b, g.reshape(1, d)]
    row_spec = pl.BlockSpec((tm, d), lambda i: (i, 0))
    if with_router:
        in_specs.append(pl.BlockSpec(router.shape, lambda i: (0, 0)))
        args.append(router)
        out_specs = [row_spec, pl.BlockSpec((tm, TOP_K), lambda i: (i, 0)),
                     pl.BlockSpec((tm, TOP_K), lambda i: (i, 0))]
        out_shape = [jax.ShapeDtypeStruct((m, d), F32), jax.ShapeDtypeStruct((m, TOP_K), I32),
                     jax.ShapeDtypeStruct((m, TOP_K), F32)]
    else:
        out_specs = [row_spec, row_spec]
        out_shape = [jax.ShapeDtypeStruct((m, d), F32), jax.ShapeDtypeStruct((m, d), BF16)]
    return pl.pallas_call(
        functools.partial(_out_proj_kernel, n_a=n_a, with_router=with_router),
        grid=(m // tm,),
        in_specs=in_specs, out_specs=out_specs, out_shape=out_shape,
        compiler_params=_params("arbitrary"),
        name="out_proj_router" if with_router else "out_proj",
    )(*args)


def _gffn_kernel(bexp_ref, nsub_ref, nused_ref, xs_ref, w1_ref, w3_ref, w2_ref, *rest, sub, with_res):
    if with_res:
        res_ref, g_ref, o_ref, h_ref, w1b_ref, w3b_ref, w2b_ref = rest
    else:
        o_ref, w1b_ref, w3b_ref, w2b_ref = rest
    s = pl.program_id(0)
    j = pl.program_id(1)
    n_j = pl.num_programs(1)

    def rows_of(i):
        return pl.ds(pl.multiple_of(i * sub, sub), sub)

    @pl.when(s < nused_ref[0])
    def _():
        n = nsub_ref[s]

        @pl.when(j == 0)
        def _():
            def init(i, carry):
                if with_res:
                    o_ref[rows_of(i), :] = res_ref[rows_of(i), :]
                else:
                    o_ref[rows_of(i), :] = jnp.zeros((sub, o_ref.shape[1]), F32)
                return carry
            lax.fori_loop(0, n, init, 0)

        w1b_ref[...] = w1_ref[0].astype(BF16)
        w3b_ref[...] = w3_ref[0].astype(BF16)
        w2b_ref[...] = w2_ref[0].astype(BF16)

        def up(i):
            x = xs_ref[rows_of(i), :]
            a = _dot(x, w1b_ref[...])
            b = _dot(x, w3b_ref[...])
            return (_silu(a) * b).astype(BF16)

        def down(i, act):
            o_ref[rows_of(i), :] += _dot(act, w2b_ref[...])

        def pair(p, carry):
            act0 = up(2 * p)
            act1 = up(2 * p + 1)
            down(2 * p, act0)
            down(2 * p + 1, act1)
            return carry
        lax.fori_loop(0, n // 2, pair, 0)

        @pl.when(n % 2 == 1)
        def _():
            down(n - 1, up(n - 1))

        if with_res:
            @pl.when(j == n_j - 1)
            def _():
                def fin(i, carry):
                    h_ref[rows_of(i), :] = _rms(o_ref[rows_of(i), :], g_ref[...]).astype(BF16)
                    return carry
                lax.fori_loop(0, n, fin, 0)


def _gffn(xs, w1, w3, w2, bexp, nsub, nused, blk, sub, tf, res=None, g=None):
    n_rows, d = xs.shape
    n_blocks = bexp.shape[0]
    f = w1.shape[2]
    n_f = f // tf
    with_res = res is not None

    def last(s, nu):
        return jnp.minimum(s, jnp.maximum(nu[0] - 1, 0))

    def jj(s, j, nu):
        return jnp.where(s < nu[0], j, n_f - 1)

    row_spec = pl.BlockSpec((blk, d), lambda s, j, be, ns, nu: (last(s, nu), 0),
                            pipeline_mode=pl.Buffered(1))
    in_specs = [row_spec,
                pl.BlockSpec((1, d, tf), lambda s, j, be, ns, nu: (be[s], 0, jj(s, j, nu))),
                pl.BlockSpec((1, d, tf), lambda s, j, be, ns, nu: (be[s], 0, jj(s, j, nu))),
                pl.BlockSpec((1, tf, d), lambda s, j, be, ns, nu: (be[s], jj(s, j, nu), 0))]
    args = [bexp, nsub, nused, xs, w1, w3, w2]
    out_specs = [row_spec]
    out_shape = [jax.ShapeDtypeStruct((n_rows, d), F32)]
    if with_res:
        in_specs += [row_spec, pl.BlockSpec((1, d), lambda s, j, be, ns, nu: (0, 0))]
        args += [res, g.reshape(1, d)]
        out_specs.append(row_spec)
        out_shape.append(jax.ShapeDtypeStruct((n_rows, d), BF16))
    return pl.pallas_call(
        functools.partial(_gffn_kernel, sub=sub, with_res=with_res),
        grid_spec=pltpu.PrefetchScalarGridSpec(
            num_scalar_prefetch=3,
            grid=(n_blocks, n_f),
            in_specs=in_specs,
            out_specs=out_specs,
            scratch_shapes=[pltpu.VMEM((d, tf), BF16), pltpu.VMEM((d, tf), BF16),
                            pltpu.VMEM((tf, d), BF16)]),
        out_shape=out_shape,
        compiler_params=_params("arbitrary", "arbitrary"),
        name="ffn" if with_res else "moe_experts",
    )(*args)


def _dispatch_kernel(tile_ids_ref, n_tiles_ref, src_ref, x_hbm_ref, g_ref, o_ref, buf_ref, sem_ref, *, sub):
    k = pl.program_id(0)
    n_valid = n_tiles_ref[0]

    def start(step, slot):
        base = tile_ids_ref[step] * sub

        def body(r, carry):
            tok = src_ref[base + r]
            pltpu.make_async_copy(x_hbm_ref.at[pl.ds(tok, 1)], buf_ref.at[slot, pl.ds(r, 1)],
                                  sem_ref.at[slot]).start()
            return carry
        lax.fori_loop(0, sub, body, 0, unroll=8)

    @pl.when(jnp.logical_and(k == 0, n_valid > 0))
    def _():
        start(0, 0)

    @pl.when(k + 1 < n_valid)
    def _():
        start(k + 1, (k + 1) % 2)

    @pl.when(k < n_valid)
    def _():
        slot = k % 2
        pltpu.make_async_copy(x_hbm_ref.at[pl.ds(0, sub)], buf_ref.at[slot], sem_ref.at[slot]).wait()
        o_ref[...] = _rms(buf_ref[slot], g_ref[...]).astype(BF16)


def _dispatch(x, g, tile_ids, n_tiles, src, n_slots, sub):
    m, d = x.shape
    max_tiles = tile_ids.shape[0]
    return pl.pallas_call(
        functools.partial(_dispatch_kernel, sub=sub),
        grid_spec=pltpu.PrefetchScalarGridSpec(
            num_scalar_prefetch=3,
            grid=(max_tiles,),
            in_specs=[pl.BlockSpec(memory_space=pl.ANY),
                      pl.BlockSpec((1, d), lambda k, ids, nt, src: (0, 0))],
            out_specs=pl.BlockSpec((sub, d), lambda k, ids, nt, src: (ids[k], 0)),
            scratch_shapes=[pltpu.VMEM((2, sub, d), F32), pltpu.SemaphoreType.DMA((2,))]),
        out_shape=jax.ShapeDtypeStruct((n_slots, d), BF16),
        compiler_params=_params("arbitrary"),
        name="moe_dispatch",
    )(tile_ids, n_tiles, src, x, g.reshape(1, d))


def _combine_kernel(pos_ref, x_ref, gate_ref, g_ref, ys_hbm_ref, ya_ref, yb_ref, buf_ref, sem_ref, *, tm, n_a):
    i = pl.program_id(0)
    n = pl.num_programs(0)

    def start(step, slot):
        base = step * (tm * TOP_K)

        def body(r, carry):
            for kk in range(TOP_K):
                p = pos_ref[base + r * TOP_K + kk]
                pltpu.make_async_copy(ys_hbm_ref.at[pl.ds(p, 1)], buf_ref.at[slot, kk, pl.ds(r, 1)],
                                      sem_ref.at[slot]).start()
            return carry
        lax.fori_loop(0, tm, body, 0, unroll=4)

    @pl.when(i == 0)
    def _():
        start(0, 0)

    @pl.when(i + 1 < n)
    def _():
        start(i + 1, (i + 1) % 2)

    slot = i % 2
    for kk in range(TOP_K):
        pltpu.make_async_copy(ys_hbm_ref.at[pl.ds(0, tm)], buf_ref.at[slot, kk], sem_ref.at[slot]).wait()
    y = x_ref[...]
    for kk in range(TOP_K):
        y = y + gate_ref[:, kk:kk + 1] * buf_ref[slot, kk]
    y = _rms(y, g_ref[...])

    @pl.when(i < n_a)
    def _():
        ya_ref[...] = y

    @pl.when(i >= n_a)
    def _():
        yb_ref[...] = y


def _combine(x, gates, pos, ys, g, tm, m_a):
    m, d = x.shape
    n_a = m_a // tm
    assert n_a * tm == m_a and m % tm == 0
    return pl.pallas_call(
        functools.partial(_combine_kernel, tm=tm, n_a=n_a),
        grid_spec=pltpu.PrefetchScalarGridSpec(
            num_scalar_prefetch=1,
            grid=(m // tm,),
            in_specs=[pl.BlockSpec((tm, d), lambda i, pos: (i, 0)),
                      pl.BlockSpec((tm, TOP_K), lambda i, pos: (i, 0)),
                      pl.BlockSpec((1, d), lambda i, pos: (0, 0)),
                      pl.BlockSpec(memory_space=pl.ANY)],
            out_specs=[pl.BlockSpec((tm, d), lambda i, pos: (jnp.minimum(i, n_a - 1), 0)),
                       pl.BlockSpec((tm, d), lambda i, pos: (jnp.maximum(i - n_a, 0), 0))],
            scratch_shapes=[pltpu.VMEM((2, TOP_K, tm, d), F32), pltpu.SemaphoreType.DMA((2,))]),
        out_shape=[jax.ShapeDtypeStruct((m_a, d), F32), jax.ShapeDtypeStruct((m - m_a, d), F32)],
        compiler_params=_params("arbitrary"),
        name="moe_combine",
    )(pos, x, gates, g.reshape(1, d), ys)


def _moe_plan(top_i, n_exp, blk, sub, n_blocks, max_tiles):
    m = top_i.shape[0]
    e_flat = top_i.reshape(-1)
    onehot = (e_flat[:, None] == jnp.arange(n_exp, dtype=I32)[None, :]).astype(I32)
    csum = jnp.cumsum(onehot, axis=0)
    rank = jnp.sum((csum - 1) * onehot, axis=1)
    counts = csum[-1]
    blocks_per = (counts + blk - 1) // blk
    block_end = jnp.cumsum(blocks_per)
    block_start = block_end - blocks_per
    n_used = block_end[-1]
    pos = block_start[e_flat] * blk + rank
    src = jnp.zeros((n_blocks * blk,), I32).at[pos].set(jnp.arange(m * TOP_K, dtype=I32) // TOP_K)

    s = jnp.arange(n_blocks, dtype=I32)
    bexp = jnp.minimum(jnp.sum((s[:, None] >= block_end[None, :]).astype(I32), axis=1), n_exp - 1)
    valid = jnp.clip(counts[bexp] - (s - block_start[bexp]) * blk, 0, blk)
    valid = jnp.where(s < n_used, valid, 0)
    nsub = (valid + sub - 1) // sub
    last_exp = bexp[jnp.maximum(n_used - 1, 0)]
    bexp = jnp.where(s < n_used, bexp, last_exp)

    per_blk = blk // sub
    t = jnp.arange(n_blocks * per_blk, dtype=I32)
    tile_valid = (t % per_blk) < nsub[t // per_blk]
    n_tiles = jnp.sum(tile_valid.astype(I32))
    order = jnp.argsort(jnp.logical_not(tile_valid), stable=True).astype(I32)[:max_tiles]
    last_tile = order[jnp.maximum(n_tiles - 1, 0)]
    tile_ids = jnp.where(jnp.arange(max_tiles, dtype=I32) < n_tiles, order, last_tile)
    return pos, src, bexp, nsub, n_used.reshape(1), tile_ids, n_tiles.reshape(1)


def _moe(x, top_i, gates, g_ffn, g_final, w1, w3, w2, blk, sub, tf, tm_c, m_a):
    m, d = x.shape
    n_exp = w1.shape[0]
    n_blocks = (m * TOP_K + n_exp * (blk - 1)) // blk
    max_tiles = (m * TOP_K + n_exp * (sub - 1)) // sub
    pos, src, bexp, nsub, nused, tile_ids, n_tiles = _moe_plan(top_i, n_exp, blk, sub, n_blocks, max_tiles)
    xs = _dispatch(x, g_ffn, tile_ids, n_tiles, src, n_blocks * blk, sub)
    (ys,) = _gffn(xs, w1, w3, w2, bexp, nsub, nused, blk, sub, tf)
    return _combine(x, gates, pos, ys, g_final, tm_c, m_a)


def _dense_plan(m, blk, sub):
    assert m % blk == 0 and blk % sub == 0
    n_blocks = m // blk
    return (jnp.zeros((n_blocks,), I32), jnp.full((n_blocks,), blk // sub, I32),
            jnp.full((1,), n_blocks, I32))


def kernel(x_prompt, x_sample, state_conv_a, state_conv_c, state_pool_d, norm_mix_g, norm_ffn_g, norm_final_g, w_in_ab, conv_a, ln_b_g, ln_b_b, w_s, b_s, w_out_ab, ffn_w1, ffn_w3, ffn_w2, w_in_cd, conv_c, conv_c_b, ln_c_g, ln_c_b, pool_w, pool_scale, w_out_cd, router, exp_w1, exp_w3, exp_w2):
    n_seq, seq_len, d = x_prompt.shape
    nseq_s, dl, _ = x_sample.shape
    mp, ms = n_seq * seq_len, nseq_s * dl
    m = mp + ms
    assert norm_mix_g.shape[0] == 2, "two layers: conv/chunk-MLP + dense FFN, then conformer/pool + MoE"
    assert seq_len % CHUNK == 0 and dl <= CHUNK and mp % ms == 0

    big = m * TOP_K >= 4 * 2560
    tm_in = _pick(m, (2176, 1088, 544, 272))
    tm_row = _pick(ms, (512, 256, 128, 64))
    t_rows = _pick(seq_len, (256, 128))
    tn = 512
    blk_f, sub_f = (1088, 272) if m % 1088 == 0 else (_pick(m, (544, 272)), 136)
    tf = 256
    blk_e, sub_e = (2560, 256) if big else (256, 128)
    tm_c = _pick(ms, (256, 128, 64))

    xp = x_prompt.reshape(mp, d)
    xs_lm = jnp.transpose(x_sample, (1, 0, 2)).reshape(ms, d)

    h = _norm_in(xp, xs_lm, norm_mix_g[0], tm_row)
    proj = _matmul(h, w_in_ab[0], tm_in, tn)
    mixed, a_p, v_p = _mix_ab_prompt(proj, conv_a[0], ln_b_g[0], ln_b_b[0], w_s[0], b_s[0],
                                     n_seq, seq_len, t_rows)
    mixed, a_s, v_s = _mix_ab_sample(mixed, proj, jnp.transpose(state_conv_a[0], (1, 0, 2)),
                                     conv_a[0], ln_b_g[0], ln_b_b[0], w_s[0], b_s[0], mp, dl, nseq_s)
    x, h = _out_proj(mixed, w_out_ab[0].astype(BF16), xp, xs_lm, norm_ffn_g[0], None, tm_row)
    x, h = _gffn(h, ffn_w1, ffn_w3, ffn_w2, *_dense_plan(m, blk_f, sub_f), blk_f, sub_f, tf,
                 res=x, g=norm_mix_g[1])

    proj = _matmul(h, w_in_cd[0], tm_in, tn)
    mixed, c_p, d_p = _mix_cd_prompt(proj, conv_c[0], conv_c_b[0], ln_c_g[0], ln_c_b[0], pool_w[0],
                                     pool_scale[0], n_seq, seq_len, t_rows)
    mixed, g_s = _mix_cd_sample(mixed, proj, jnp.transpose(state_conv_c[0], (1, 0, 2)),
                                jnp.transpose(state_pool_d[0], (1, 0, 2)), conv_c[0], conv_c_b[0],
                                ln_c_g[0], ln_c_b[0], pool_w[0], pool_scale[0], mp, dl, nseq_s)
    x, top_i, gates = _out_proj(mixed, w_out_cd[0].astype(BF16), x, x, norm_ffn_g[1], router[0], tm_row)
    y_p, y_s = _moe(x, top_i, gates, norm_ffn_g[1], norm_final_g, exp_w1[0], exp_w3[0], exp_w2[0],
                    blk_e, sub_e, tf, tm_c, mp)

    dc = conv_c.shape[2]
    y_prompt = y_p.reshape(n_seq, seq_len, d)
    y_sample = jnp.transpose(y_s.reshape(dl, nseq_s, d), (1, 0, 2))
    new_a_s = jnp.transpose(a_s, (1, 0, 2))
    new_v_s = jnp.transpose(v_s.reshape(dl, nseq_s, -1), (1, 0, 2))
    g_rows = jnp.transpose(g_s.reshape(dl, nseq_s, dc), (1, 0, 2))
    p_rows = jnp.transpose(proj[mp:, 2 * dc:].reshape(dl, nseq_s, -1), (1, 0, 2))
    n_c = state_conv_c.shape[2]
    n_d = state_pool_d.shape[2]
    new_c_s = jnp.concatenate([state_conv_c[0], g_rows], axis=1)[:, -n_c:]
    new_d_s = jnp.concatenate([state_pool_d[0], p_rows], axis=1)[:, -n_d:]
    return (y_prompt, y_sample,
            a_p[None], new_a_s[None],
            v_p[None], new_v_s[None],
            c_p[None], new_c_s[None],
            d_p[None], new_d_s[None])
```

```python
import functools

import jax
import jax.numpy as jnp
from jax import lax
from jax.experimental import pallas as pl
from jax.experimental.pallas import tpu as pltpu

F32 = jnp.float32
BF16 = jnp.bfloat16
I32 = jnp.int32

EPS = 1e-6
PAST_LEN = 16384
CHUNK = 128
HEAD_B = 128
POOL_WINDOWS = (2, 4, 8, 16)
TOP_K = 2

VMEM_LIMIT_BYTES = 60 * 1024 * 1024
SUBLANES = 8
LANES = 128


def _params(*sem):
    return pltpu.CompilerParams(dimension_semantics=sem, vmem_limit_bytes=VMEM_LIMIT_BYTES)


def _pick(n, candidates):
    for c in candidates:
        if n % c == 0:
            return c
    raise ValueError(f"no tile in {candidates} divides {n}")


def _rms(x, g):
    return x * lax.rsqrt(jnp.mean(x * x, axis=-1, keepdims=True) + EPS) * g


def _layernorm(x, g, b):
    mu = jnp.mean(x, axis=-1, keepdims=True)
    xc = x - mu
    return xc * lax.rsqrt(jnp.mean(xc * xc, axis=-1, keepdims=True) + EPS) * g + b


def _silu(x):
    return x * jax.nn.sigmoid(x)


def _dot(a, b):
    return jnp.dot(a, b, preferred_element_type=F32)


def _mix_ab_prompt_kernel(bg_ref, cg_ref, xin_ref, u_ref, v_ref, cgh_ref, xinh_ref,
                          cw_ref, lng_ref, lnb_ref, ws_ref, bsb_ref,
                          o_ref, astate_ref, vstate_ref, abuf_ref):
    t = pl.program_id(1)
    t_rows, da = bg_ref.shape
    n_heads = ws_ref.shape[0]
    halo = SUBLANES

    a = cg_ref[...] * xin_ref[...]
    prev = jnp.where(t == 0, 0.0, cgh_ref[...] * xinh_ref[...])
    abuf_ref[0:halo, :] = prev
    abuf_ref[halo:halo + t_rows, :] = a
    conv = (cw_ref[0:1, :] * abuf_ref[halo - 2:halo - 2 + t_rows, :]
            + cw_ref[1:2, :] * abuf_ref[halo - 1:halo - 1 + t_rows, :]
            + cw_ref[2:3, :] * a)
    o_ref[:, 0:da] = (bg_ref[...] * conv).astype(BF16)
    astate_ref[0] = abuf_ref[halo + t_rows - 2:halo + t_rows, :]

    row = lax.broadcasted_iota(I32, (CHUNK, CHUNK), 0)
    col = lax.broadcasted_iota(I32, (CHUNK, CHUNK), 1)
    lower = row >= col
    for h in range(n_heads):
        cols = slice(h * HEAD_B, (h + 1) * HEAD_B)
        vn = _layernorm(jax.nn.gelu(v_ref[:, cols]), lng_ref[:, cols], lnb_ref[:, cols])
        vstate_ref[0, :, cols] = vn[t_rows - CHUNK:t_rows]
        w = jnp.where(lower, ws_ref[h], 0.0).astype(BF16)
        for c in range(t_rows // CHUNK):
            rows = slice(c * CHUNK, (c + 1) * CHUNK)
            mix = _dot(w, vn[rows].astype(BF16)) + bsb_ref[h]
            o_ref[rows, da + h * HEAD_B:da + (h + 1) * HEAD_B] = (
                jax.nn.gelu(u_ref[rows, cols]) * mix).astype(BF16)


def _mix_ab_prompt(proj, conv_a, ln_g, ln_b, w_s, b_s, n_seq, seq_len, t_rows):
    m = proj.shape[0]
    da = conv_a.shape[1]
    db = ln_g.shape[0]
    n_heads = w_s.shape[0]
    tiles = seq_len // t_rows
    halo_blocks = t_rows // SUBLANES

    def col_spec(c):
        return pl.BlockSpec((t_rows, da), lambda b, t: (b * tiles + t, c))

    def halo_spec(c):
        return pl.BlockSpec(
            (SUBLANES, da),
            lambda b, t: (jnp.maximum((b * tiles + t) * halo_blocks - 1, 0), c))

    def full(shape):
        return pl.BlockSpec(shape, lambda b, t: (0,) * len(shape))

    bsb = jnp.broadcast_to(b_s[:, :CHUNK, None], (n_heads, CHUNK, HEAD_B))
    return pl.pallas_call(
        _mix_ab_prompt_kernel,
        grid=(n_seq, tiles),
        in_specs=[col_spec(0), col_spec(1), col_spec(2), col_spec(3), col_spec(4),
                  halo_spec(1), halo_spec(2),
                  full((conv_a.shape[0], da)), full((1, db)), full((1, db)),
                  full((n_heads, CHUNK, CHUNK)), full((n_heads, CHUNK, HEAD_B))],
        out_specs=[pl.BlockSpec((t_rows, da + db), lambda b, t: (b * tiles + t, 0)),
                   pl.BlockSpec((1, 2, da), lambda b, t: (b, 0, 0)),
                   pl.BlockSpec((1, CHUNK, db), lambda b, t: (b, 0, 0))],
        out_shape=[jax.ShapeDtypeStruct((m, da + db), BF16),
                   jax.ShapeDtypeStruct((n_seq, 2, da), F32),
                   jax.ShapeDtypeStruct((n_seq, CHUNK, db), F32)],
        scratch_shapes=[pltpu.VMEM((SUBLANES + t_rows, da), F32)],
        compiler_params=_params("parallel", "arbitrary"),
        name="mix_ab_prompt",
    )(proj, proj, proj, proj, proj, proj, proj, conv_a, ln_g.reshape(1, db), ln_b.reshape(1, db),
      w_s[:, :CHUNK, :CHUNK], bsb)


def _mix_ab_sample_kernel(mixed_hbm_ref, bg_ref, cg_ref, xin_ref, u_ref, v_ref, st_ref,
                          cw_ref, lng_ref, lnb_ref, wsl_ref, bsl_ref,
                          o_ref, astate_ref, vstate_ref, *, dl, nseq):
    del mixed_hbm_ref
    da = bg_ref.shape[1]
    n_heads = lng_ref.shape[1] // HEAD_B
    n_state = st_ref.shape[0]

    def slab(l):
        return slice(l * nseq, (l + 1) * nseq)

    a_pad = [st_ref[k] for k in range(n_state)]
    a_pad += [cg_ref[slab(l), :] * xin_ref[slab(l), :] for l in range(dl)]
    for l in range(dl):
        conv = (cw_ref[0:1, :] * a_pad[l] + cw_ref[1:2, :] * a_pad[l + 1]
                + cw_ref[2:3, :] * a_pad[l + 2])
        o_ref[slab(l), 0:da] = (bg_ref[slab(l), :] * conv).astype(BF16)
    for k in range(n_state):
        astate_ref[k] = a_pad[dl + k]

    for h in range(n_heads):
        cols = slice(h * HEAD_B, (h + 1) * HEAD_B)
        vn = _layernorm(jax.nn.gelu(v_ref[:, cols]), lng_ref[:, cols], lnb_ref[:, cols])
        vstate_ref[:, cols] = vn
        for i in range(dl):
            mix = bsl_ref[i:i + 1, cols]
            for j in range(i + 1):
                mix = mix + wsl_ref[i * dl + j:i * dl + j + 1, cols] * vn[slab(j)]
            o_ref[slab(i), da + h * HEAD_B:da + (h + 1) * HEAD_B] = (
                jax.nn.gelu(u_ref[slab(i), cols]) * mix).astype(BF16)


def _mix_ab_sample(mixed, proj, state_lm, conv_a, ln_g, ln_b, w_s, b_s, row0, dl, nseq):
    ms = dl * nseq
    da = conv_a.shape[1]
    db = ln_g.shape[0]
    n_heads = w_s.shape[0]
    rb = row0 // ms

    def col_spec(c):
        return pl.BlockSpec((ms, da), lambda i: (rb, c))

    def full(shape):
        return pl.BlockSpec(shape, lambda i: (0,) * len(shape))

    wsl = jnp.repeat(jnp.transpose(w_s[:, :dl, :dl], (1, 2, 0)).reshape(dl * dl, n_heads), HEAD_B, axis=1)
    bsl = jnp.repeat(jnp.transpose(b_s[:, :dl], (1, 0)), HEAD_B, axis=1)
    return pl.pallas_call(
        functools.partial(_mix_ab_sample_kernel, dl=dl, nseq=nseq),
        grid=(1,),
        in_specs=[pl.BlockSpec(memory_space=pl.ANY),
                  col_spec(0), col_spec(1), col_spec(2), col_spec(3), col_spec(4),
                  full(state_lm.shape), full((conv_a.shape[0], da)), full((1, db)), full((1, db)),
                  full((dl * dl, db)), full((dl, db))],
        out_specs=[pl.BlockSpec((ms, da + db), lambda i: (rb, 0)),
                   full(state_lm.shape),
                   full((ms, db))],
        out_shape=[jax.ShapeDtypeStruct(mixed.shape, BF16),
                   jax.ShapeDtypeStruct(state_lm.shape, F32),
                   jax.ShapeDtypeStruct((ms, db), F32)],
        input_output_aliases={0: 0},
        compiler_params=_params("arbitrary"),
        name="mix_ab_sample",
    )(mixed, proj, proj, proj, proj, proj, state_lm, conv_a, ln_g.reshape(1, db), ln_b.reshape(1, db),
      wsl, bsl)


def _mix_cd_prompt_kernel(ga_ref, gb_ref, p_ref, gah_ref, gbh_ref, ph_ref,
                          cw_ref, cb_ref, lng_ref, lnb_ref, pw_ref, ps_ref,
                          o_ref, cstate_ref, dstate_ref, gbuf_ref, pbuf_ref, cbuf_ref, shift_ref):
    t = pl.program_id(1)
    t_rows, dc = ga_ref.shape
    k_c = cw_ref.shape[0]
    c_halo = gah_ref.shape[0]
    d_halo = ph_ref.shape[0]
    gd = pw_ref.shape[1]

    g = ga_ref[...] * jax.nn.sigmoid(gb_ref[...])
    gbuf_ref[0:c_halo, :] = jnp.where(t == 0, 0.0, gah_ref[...] * jax.nn.sigmoid(gbh_ref[...]))
    gbuf_ref[c_halo:c_halo + t_rows, :] = g
    n_shift = shift_ref.shape[1]
    for r in range(1, SUBLANES):
        shift_ref[r - 1] = gbuf_ref[r:r + n_shift, :]
    first = c_halo - (k_c - 1)
    for cblk in range(dc // LANES):
        cols = slice(cblk * LANES, (cblk + 1) * LANES)
        acc = cb_ref[:, cols]
        for k in range(k_c):
            q, r = divmod(first + k, SUBLANES)
            if r == 0:
                win = gbuf_ref[first + k:first + k + t_rows, cols]
            else:
                win = shift_ref[r - 1, q * SUBLANES:q * SUBLANES + t_rows, cols]
            acc = acc + cw_ref[k:k + 1, cols] * win
        cbuf_ref[:, cols] = acc
    o_ref[:, 0:dc] = _silu(_layernorm(cbuf_ref[...], lng_ref[...], lnb_ref[...])).astype(BF16)
    cstate_ref[0] = gbuf_ref[c_halo + t_rows - (k_c - 1):c_halo + t_rows, :]

    pbuf_ref[0:d_halo, :] = jnp.where(t == 0, 0.0, ph_ref[...])
    pbuf_ref[d_halo:d_halo + t_rows, :] = p_ref[...]
    pos = t * t_rows + lax.broadcasted_iota(I32, (t_rows, 1), 0)
    for gi, w in enumerate(POOL_WINDOWS):
        cols = slice(gi * gd, (gi + 1) * gd)
        cur = p_ref[:, cols]
        s = cur
        for k in range(1, w):
            s = s + pbuf_ref[d_halo - k:d_halo - k + t_rows, cols]
        cnt = jnp.minimum(w, pos + 1).astype(F32)
        diff = (s / cnt - cur).astype(BF16)
        d = _dot(diff, pw_ref[gi].astype(BF16)) * ps_ref[:, cols]
        o_ref[:, dc + gi * gd:dc + (gi + 1) * gd] = d.astype(BF16)
    n_keep = dstate_ref.shape[1]
    dstate_ref[0] = pbuf_ref[d_halo + t_rows - n_keep:d_halo + t_rows, :]


def _mix_cd_prompt(proj, conv_c, conv_c_b, ln_g, ln_b, pool_w, pool_scale, n_seq, seq_len, t_rows):
    m = proj.shape[0]
    k_c, dc = conv_c.shape
    dd = pool_scale.shape[0]
    n_pool, gd = pool_w.shape[0], pool_w.shape[1]
    tiles = seq_len // t_rows
    c_halo = -(-(k_c - 1) // SUBLANES) * SUBLANES
    d_halo = -(-(max(POOL_WINDOWS) - 1) // SUBLANES) * SUBLANES
    assert t_rows % c_halo == 0 and t_rows % d_halo == 0 and dc == dd

    def col_spec(c):
        return pl.BlockSpec((t_rows, dc), lambda b, t: (b * tiles + t, c))

    def halo_spec(c, rows):
        per = t_rows // rows
        return pl.BlockSpec((rows, dc),
                            lambda b, t: (jnp.maximum((b * tiles + t) * per - 1, 0), c))

    def full(shape):
        return pl.BlockSpec(shape, lambda b, t: (0,) * len(shape))

    return pl.pallas_call(
        _mix_cd_prompt_kernel,
        grid=(n_seq, tiles),
        in_specs=[col_spec(0), col_spec(1), col_spec(2),
                  halo_spec(0, c_halo), halo_spec(1, c_halo), halo_spec(2, d_halo),
                  full((k_c, dc)), full((1, dc)), full((1, dc)), full((1, dc)),
                  full((n_pool, gd, gd)), full((1, dd))],
        out_specs=[pl.BlockSpec((t_rows, dc + dd), lambda b, t: (b * tiles + t, 0)),
                   pl.BlockSpec((1, k_c - 1, dc), lambda b, t: (b, 0, 0)),
                   pl.BlockSpec((1, max(POOL_WINDOWS) - 1, dd), lambda b, t: (b, 0, 0))],
        out_shape=[jax.ShapeDtypeStruct((m, dc + dd), BF16),
                   jax.ShapeDtypeStruct((n_seq, k_c - 1, dc), F32),
                   jax.ShapeDtypeStruct((n_seq, max(POOL_WINDOWS) - 1, dd), F32)],
        scratch_shapes=[pltpu.VMEM((c_halo + t_rows, dc), F32),
                        pltpu.VMEM((d_halo + t_rows, dd), F32),
                        pltpu.VMEM((t_rows, dc), F32),
                        pltpu.VMEM((SUBLANES - 1, c_halo + t_rows - SUBLANES, dc), F32)],
        compiler_params=_params("parallel", "arbitrary"),
        name="mix_cd_prompt",
    )(proj, proj, proj, proj, proj, proj, conv_c, conv_c_b.reshape(1, dc), ln_g.reshape(1, dc),
      ln_b.reshape(1, dc), pool_w, pool_scale.reshape(1, dd))


def _mix_cd_sample_kernel(mixed_hbm_ref, ga_ref, gb_ref, p_ref, stc_ref, std_ref,
                          cw_ref, cb_ref, lng_ref, lnb_ref, pw_ref, ps_ref,
                          o_ref, g_ref, cbuf_ref, dbuf_ref, *, dl, nseq, start_pos):
    del mixed_hbm_ref
    gi = pl.program_id(0)
    n_groups = pl.num_programs(0)
    gd = ga_ref.shape[1]
    k_c = cw_ref.shape[0]
    n_c = stc_ref.shape[0]
    n_d = std_ref.shape[0]

    def slab(l):
        return slice(l * nseq, (l + 1) * nseq)

    g = ga_ref[...] * jax.nn.sigmoid(gb_ref[...])
    g_ref[...] = g
    c_pad = [stc_ref[k] for k in range(n_c)] + [g[slab(l)] for l in range(dl)]
    d_pad = [std_ref[k] for k in range(n_d)] + [p_ref[slab(l), :] for l in range(dl)]
    for l in range(dl):
        acc = cb_ref[...] + cw_ref[0:1, :] * c_pad[l]
        for k in range(1, k_c):
            acc = acc + cw_ref[k:k + 1, :] * c_pad[l + k]
        cbuf_ref[gi, slab(l), :] = acc

        cur = d_pad[n_d + l]
        s = cur
        pooled = jnp.zeros_like(cur)
        for k in range(1, max(POOL_WINDOWS) + 1):
            if k in POOL_WINDOWS:
                cnt = float(min(k, start_pos + l + 1))
                pooled = jnp.where(gi == POOL_WINDOWS.index(k), s / cnt, pooled)
            if k < max(POOL_WINDOWS):
                s = s + d_pad[n_d + l - k]
        diff = (pooled - cur).astype(BF16)
        dbuf_ref[gi, slab(l), :] = (_dot(diff, pw_ref[0].astype(BF16)) * ps_ref[...]).astype(BF16)

    @pl.when(gi == n_groups - 1)
    def _():
        n_blk = cbuf_ref.shape[0]
        dc = n_blk * gd
        total = cbuf_ref[0].sum(axis=-1, keepdims=True)
        for b in range(1, n_blk):
            total = total + cbuf_ref[b].sum(axis=-1, keepdims=True)
        mu = total / dc
        var = None
        for b in range(n_blk):
            xc = cbuf_ref[b] - mu
            sq = (xc * xc).sum(axis=-1, keepdims=True)
            var = sq if var is None else var + sq
        rstd = lax.rsqrt(var / dc + EPS)
        for b in range(n_blk):
            cols = slice(b * gd, (b + 1) * gd)
            y = (cbuf_ref[b] - mu) * rstd * lng_ref[:, cols] + lnb_ref[:, cols]
            o_ref[:, cols] = _silu(y).astype(BF16)
            o_ref[:, dc + b * gd:dc + (b + 1) * gd] = dbuf_ref[b]


def _mix_cd_sample(mixed, proj, stc_lm, std_lm, conv_c, conv_c_b, ln_g, ln_b, pool_w, pool_scale,
                   row0, dl, nseq):
    ms = dl * nseq
    k_c, dc = conv_c.shape
    dd = pool_scale.shape[0]
    n_pool, gd = pool_w.shape[0], pool_w.shape[1]
    assert dc == dd == n_pool * gd and n_pool == len(POOL_WINDOWS)
    rb = row0 // ms

    return pl.pallas_call(
        functools.partial(_mix_cd_sample_kernel, dl=dl, nseq=nseq, start_pos=PAST_LEN),
        grid=(n_pool,),
        in_specs=[pl.BlockSpec(memory_space=pl.ANY),
                  pl.BlockSpec((ms, gd), lambda i: (rb, i)),
                  pl.BlockSpec((ms, gd), lambda i: (rb, n_pool + i)),
                  pl.BlockSpec((ms, gd), lambda i: (rb, 2 * n_pool + i)),
                  pl.BlockSpec((stc_lm.shape[0], nseq, gd), lambda i: (0, 0, i)),
                  pl.BlockSpec((std_lm.shape[0], nseq, gd), lambda i: (0, 0, i)),
                  pl.BlockSpec((k_c, gd), lambda i: (0, i)),
                  pl.BlockSpec((1, gd), lambda i: (0, i)),
                  pl.BlockSpec((1, dc), lambda i: (0, 0)),
                  pl.BlockSpec((1, dc), lambda i: (0, 0)),
                  pl.BlockSpec((1, gd, gd), lambda i: (i, 0, 0)),
                  pl.BlockSpec((1, gd), lambda i: (0, i))],
        out_specs=[pl.BlockSpec((ms, dc + dd), lambda i: (rb, 0)),
                   pl.BlockSpec((ms, gd), lambda i: (0, i))],
        out_shape=[jax.ShapeDtypeStruct(mixed.shape, BF16),
                   jax.ShapeDtypeStruct((ms, dc), F32)],
        scratch_shapes=[pltpu.VMEM((n_pool, ms, gd), F32),
                        pltpu.VMEM((n_pool, ms, gd), BF16)],
        input_output_aliases={0: 0},
        compiler_params=_params("arbitrary"),
        name="mix_cd_sample",
    )(mixed, proj, proj, proj, stc_lm, std_lm, conv_c, conv_c_b.reshape(1, dc), ln_g.reshape(1, dc),
      ln_b.reshape(1, dc), pool_w, pool_scale.reshape(1, dd))


def _norm_in_kernel(xp_ref, xs_ref, g_ref, o_ref, *, n_a):
    i = pl.program_id(0)

    @pl.when(i < n_a)
    def _():
        o_ref[...] = _rms(xp_ref[...], g_ref[...]).astype(BF16)

    @pl.when(i >= n_a)
    def _():
        o_ref[...] = _rms(xs_ref[...], g_ref[...]).astype(BF16)


def _norm_in(xp, xs, g, tm):
    mp, d = xp.shape
    ms = xs.shape[0]
    n_a, n_b = mp // tm, ms // tm
    return pl.pallas_call(
        functools.partial(_norm_in_kernel, n_a=n_a),
        grid=(n_a + n_b,),
        in_specs=[pl.BlockSpec((tm, d), lambda i: (jnp.minimum(i, n_a - 1), 0)),
                  pl.BlockSpec((tm, d), lambda i: (jnp.maximum(i - n_a, 0), 0)),
                  pl.BlockSpec((1, d), lambda i: (0, 0))],
        out_specs=pl.BlockSpec((tm, d), lambda i: (i, 0)),
        out_shape=jax.ShapeDtypeStruct((mp + ms, d), BF16),
        compiler_params=_params("arbitrary"),
        name="norm_in",
    )(xp, xs, g.reshape(1, d))


def _matmul_kernel(a_ref, w_ref, o_ref):
    o_ref[...] = _dot(a_ref[...], w_ref[...].astype(BF16))


def _matmul(a, w, tm, tn):
    m, k = a.shape
    n = w.shape[1]
    return pl.pallas_call(
        _matmul_kernel,
        grid=(m // tm, n // tn),
        in_specs=[pl.BlockSpec((tm, k), lambda i, j: (i, 0)),
                  pl.BlockSpec((k, tn), lambda i, j: (0, j))],
        out_specs=pl.BlockSpec((tm, tn), lambda i, j: (i, j)),
        out_shape=jax.ShapeDtypeStruct((m, n), F32),
        compiler_params=_params("parallel", "arbitrary"),
        name="in_proj",
    )(a, w)


def _ffn_up_kernel(h_ref, w1_ref, w3_ref, o_ref):
    h = h_ref[...]
    a = _dot(h, w1_ref[0].astype(BF16))
    b = _dot(h, w3_ref[0].astype(BF16))
    o_ref[...] = (_silu(a) * b).astype(BF16)


def _ffn_up(h, w1, w3, tm, tf):
    m, d = h.shape
    f = w1.shape[2]
    return pl.pallas_call(
        _ffn_up_kernel,
        grid=(m // tm, f // tf),
        in_specs=[pl.BlockSpec((tm, d), lambda i, j: (i, 0)),
                  pl.BlockSpec((1, d, tf), lambda i, j: (0, 0, j)),
                  pl.BlockSpec((1, d, tf), lambda i, j: (0, 0, j))],
        out_specs=pl.BlockSpec((tm, tf), lambda i, j: (i, j)),
        out_shape=jax.ShapeDtypeStruct((m, f), BF16),
        compiler_params=_params("parallel", "arbitrary"),
        name="ffn_up",
    )(h, w1, w3)


def _ffn_down_kernel(a_ref, w_ref, r_ref, o_ref):
    o_ref[...] = r_ref[...] + _dot(a_ref[...], w_ref[0].astype(BF16))


def _ffn_down(act, w2, res, tm, tn):
    m, f = act.shape
    d = w2.shape[2]
    return pl.pallas_call(
        _ffn_down_kernel,
        grid=(m // tm, d // tn),
        in_specs=[pl.BlockSpec((tm, f), lambda i, j: (i, 0)),
                  pl.BlockSpec((1, f, tn), lambda i, j: (0, 0, j)),
                  pl.BlockSpec((tm, tn), lambda i, j: (i, j))],
        out_specs=pl.BlockSpec((tm, tn), lambda i, j: (i, j)),
        out_shape=jax.ShapeDtypeStruct((m, d), F32),
        compiler_params=_params("parallel", "arbitrary"),
        name="ffn_down",
    )(act, w2, res)


def _norm_rows_kernel(x_ref, g_ref, o_ref):
    o_ref[...] = _rms(x_ref[...], g_ref[...]).astype(BF16)


def _norm_rows(x, g, tm):
    m, d = x.shape
    return pl.pallas_call(
        _norm_rows_kernel,
        grid=(m // tm,),
        in_specs=[pl.BlockSpec((tm, d), lambda i: (i, 0)),
                  pl.BlockSpec((1, d), lambda i: (0, 0))],
        out_specs=pl.BlockSpec((tm, d), lambda i: (i, 0)),
        out_shape=jax.ShapeDtypeStruct((m, d), BF16),
        compiler_params=_params("parallel"),
        name="norm_rows",
    )(x, g.reshape(1, d))


def _out_proj_kernel(a_ref, w_ref, ra_ref, rb_ref, *rest, n_a, with_h):
    i = pl.program_id(0)
    if with_h:
        g_ref, x_ref, h_ref = rest
    else:
        (x_ref,) = rest
    y = _dot(a_ref[...], w_ref[...])

    @pl.when(i < n_a)
    def _():
        x_ref[...] = ra_ref[...] + y

    @pl.when(i >= n_a)
    def _():
        x_ref[...] = rb_ref[...] + y

    if with_h:
        h_ref[...] = _rms(x_ref[...], g_ref[...]).astype(BF16)


def _out_proj(a, w_bf16, res_a, res_b, g, tm):
    m, k = a.shape
    d = w_bf16.shape[1]
    n_a = res_a.shape[0] // tm
    n_b = m // tm - n_a
    assert n_a * tm == res_a.shape[0] and (n_b == 0 or n_b * tm == res_b.shape[0])
    with_h = g is not None
    in_specs = [pl.BlockSpec((tm, k), lambda i: (i, 0)),
                pl.BlockSpec((k, d), lambda i: (0, 0), pipeline_mode=pl.Buffered(1)),
                pl.BlockSpec((tm, d), lambda i: (jnp.minimum(i, n_a - 1), 0)),
                pl.BlockSpec((tm, d), lambda i: (jnp.maximum(i - n_a, 0), 0))]
    args = [a, w_bf16, res_a, res_b]
    row_spec = pl.BlockSpec((tm, d), lambda i: (i, 0))
    out_specs = [row_spec]
    out_shape = [jax.ShapeDtypeStruct((m, d), F32)]
    if with_h:
        in_specs.append(pl.BlockSpec((1, d), lambda i: (0, 0)))
        args.append(g.reshape(1, d))
        out_specs.append(row_spec)
        out_shape.append(jax.ShapeDtypeStruct((m, d), BF16))
    return pl.pallas_call(
        functools.partial(_out_proj_kernel, n_a=n_a, with_h=with_h),
        grid=(m // tm,),
        in_specs=in_specs, out_specs=out_specs, out_shape=out_shape,
        compiler_params=_params("arbitrary"),
        name="out_proj_h" if with_h else "out_proj",
    )(*args)


def _router_kernel(x_ref, g_ref, r_ref, idx_ref, gate_ref):
    h = _rms(x_ref[...], g_ref[...])
    h_hi = h.astype(BF16)
    h_lo = (h - h_hi.astype(F32)).astype(BF16)
    r = r_ref[...]
    r_hi = r.astype(BF16)
    r_lo = (r - r_hi.astype(F32)).astype(BF16)
    logits = _dot(h_hi, r_hi) + (_dot(h_hi, r_lo) + _dot(h_lo, r_hi))
    n_exp = float(logits.shape[1])
    lane = lax.broadcasted_iota(I32, logits.shape, 1).astype(F32)
    m1 = jnp.max(logits, axis=-1, keepdims=True)
    i1 = jnp.min(jnp.where(logits == m1, lane, n_exp), axis=-1, keepdims=True)
    rest = jnp.where(lane == i1, -jnp.inf, logits)
    m2 = jnp.max(rest, axis=-1, keepdims=True)
    i2 = jnp.min(jnp.where(rest == m2, lane, n_exp), axis=-1, keepdims=True)
    e = jnp.exp(m2 - m1)
    two = lax.broadcasted_iota(I32, idx_ref.shape, 1)
    idx_ref[...] = jnp.where(two == 0, i1, i2).astype(I32)
    gate_ref[...] = jnp.where(two == 0, 1.0, e) / (1.0 + e)


def _router(x, g, router, tm):
    m, d = x.shape
    n_exp = router.shape[1]
    return pl.pallas_call(
        _router_kernel,
        grid=(m // tm,),
        in_specs=[pl.BlockSpec((tm, d), lambda i: (i, 0)),
                  pl.BlockSpec((1, d), lambda i: (0, 0)),
                  pl.BlockSpec((d, n_exp), lambda i: (0, 0))],
        out_specs=[pl.BlockSpec((tm, TOP_K), lambda i: (i, 0)),
                   pl.BlockSpec((tm, TOP_K), lambda i: (i, 0))],
        out_shape=[jax.ShapeDtypeStruct((m, TOP_K), I32),
                   jax.ShapeDtypeStruct((m, TOP_K), F32)],
        compiler_params=_params("parallel"),
        name="router",
    )(x, g.reshape(1, d), router)


def _experts_kernel(bexp_ref, nsub_ref, nused_ref, *refs, sub, s0):
    if s0:
        refs = refs[1:]
    xs_ref, w1_ref, w3_ref, w2_ref, o_ref, w1b_ref, w3b_ref, w2b_ref = refs
    s = pl.program_id(0) + s0
    j = pl.program_id(1)

    def rows_of(i):
        return pl.ds(pl.multiple_of(i * sub, sub), sub)

    @pl.when(s < nused_ref[0])
    def _():
        n = nsub_ref[s]

        @pl.when(j == 0)
        def _():
            def init(i, carry):
                o_ref[rows_of(i), :] = jnp.zeros((sub, o_ref.shape[1]), F32)
                return carry
            lax.fori_loop(0, n, init, 0)

        w1b_ref[...] = w1_ref[0].astype(BF16)
        w3b_ref[...] = w3_ref[0].astype(BF16)
        w2b_ref[...] = w2_ref[0].astype(BF16)

        def up(i):
            x = xs_ref[rows_of(i), :]
            a = _dot(x, w1b_ref[...])
            b = _dot(x, w3b_ref[...])
            return (_silu(a) * b).astype(BF16)

        def down(i, act):
            o_ref[rows_of(i), :] += _dot(act, w2b_ref[...])

        def pair(p, carry):
            act0 = up(2 * p)
            act1 = up(2 * p + 1)
            down(2 * p, act0)
            down(2 * p + 1, act1)
            return carry
        lax.fori_loop(0, n // 2, pair, 0)

        @pl.when(n % 2 == 1)
        def _():
            down(n - 1, up(n - 1))


def _experts(xs, w1, w3, w2, bexp, nsub, nused, blk, sub, tf, s0, n_s, ys=None):
    n_rows, d = xs.shape
    f = w1.shape[2]
    n_f = f // tf
    assert (ys is None) == (s0 == 0)

    def blk_idx(s, nu):
        return jnp.minimum(s + s0, jnp.maximum(nu[0] - 1, s0))

    def exp_idx(s, be, nu):
        return be[blk_idx(s, nu)]

    def jj(s, j, nu):
        return jnp.where(s + s0 < nu[0], j, n_f - 1)

    row_spec = pl.BlockSpec((blk, d), lambda s, j, be, ns, nu: (blk_idx(s, nu), 0),
                            pipeline_mode=pl.Buffered(1))
    in_specs = [row_spec,
                pl.BlockSpec((1, d, tf), lambda s, j, be, ns, nu: (exp_idx(s, be, nu), 0, jj(s, j, nu))),
                pl.BlockSpec((1, d, tf), lambda s, j, be, ns, nu: (exp_idx(s, be, nu), 0, jj(s, j, nu))),
                pl.BlockSpec((1, tf, d), lambda s, j, be, ns, nu: (exp_idx(s, be, nu), jj(s, j, nu), 0))]
    args = [xs, w1, w3, w2]
    aliases = {}
    if s0:
        in_specs = [pl.BlockSpec(memory_space=pl.ANY)] + in_specs
        args = [ys] + args
        aliases = {3: 0}
    return pl.pallas_call(
        functools.partial(_experts_kernel, sub=sub, s0=s0),
        grid_spec=pltpu.PrefetchScalarGridSpec(
            num_scalar_prefetch=3,
            grid=(n_s, n_f),
            in_specs=in_specs,
            out_specs=row_spec,
            scratch_shapes=[pltpu.VMEM((d, tf), BF16), pltpu.VMEM((d, tf), BF16),
                            pltpu.VMEM((tf, d), BF16)]),
        out_shape=jax.ShapeDtypeStruct((n_rows, d), F32),
        input_output_aliases=aliases,
        compiler_params=_params("arbitrary", "arbitrary"),
        name="moe_experts" if s0 == 0 else "moe_experts_overflow",
    )(bexp, nsub, nused, *args)


def _dispatch_kernel(tile_ids_ref, n_tiles_ref, src_ref, x_hbm_ref, g_ref, o_ref, buf_ref, sem_ref, *, sub):
    k = pl.program_id(0)
    n_valid = n_tiles_ref[0]

    def start(step, slot):
        base = tile_ids_ref[step] * sub

        def body(r, carry):
            tok = src_ref[base + r]
            pltpu.make_async_copy(x_hbm_ref.at[pl.ds(tok, 1)], buf_ref.at[slot, pl.ds(r, 1)],
                                  sem_ref.at[slot]).start()
            return carry
        lax.fori_loop(0, sub, body, 0, unroll=8)

    @pl.when(jnp.logical_and(k == 0, n_valid > 0))
    def _():
        start(0, 0)

    @pl.when(k + 1 < n_valid)
    def _():
        start(k + 1, (k + 1) % 2)

    @pl.when(k < n_valid)
    def _():
        slot = k % 2
        pltpu.make_async_copy(x_hbm_ref.at[pl.ds(0, sub)], buf_ref.at[slot], sem_ref.at[slot]).wait()
        o_ref[...] = _rms(buf_ref[slot], g_ref[...]).astype(BF16)


def _dispatch(x, g, tile_ids, n_tiles, src, n_slots, sub):
    m, d = x.shape
    max_tiles = tile_ids.shape[0]
    return pl.pallas_call(
        functools.partial(_dispatch_kernel, sub=sub),
        grid_spec=pltpu.PrefetchScalarGridSpec(
            num_scalar_prefetch=3,
            grid=(max_tiles,),
            in_specs=[pl.BlockSpec(memory_space=pl.ANY),
                      pl.BlockSpec((1, d), lambda k, ids, nt, src: (0, 0))],
            out_specs=pl.BlockSpec((sub, d), lambda k, ids, nt, src: (ids[k], 0)),
            scratch_shapes=[pltpu.VMEM((2, sub, d), F32), pltpu.SemaphoreType.DMA((2,))]),
        out_shape=jax.ShapeDtypeStruct((n_slots, d), BF16),
        compiler_params=_params("arbitrary"),
        name="moe_dispatch",
    )(tile_ids, n_tiles, src, x, g.reshape(1, d))


def _combine_kernel(pos_ref, x_ref, gate_ref, g_ref, ys_hbm_ref, ya_ref, yb_ref, buf_ref, sem_ref, *, tm, n_a):
    i = pl.program_id(0)
    n = pl.num_programs(0)

    def start(step, slot):
        base = step * (tm * TOP_K)

        def body(r, carry):
            for kk in range(TOP_K):
                p = pos_ref[base + r * TOP_K + kk]
                pltpu.make_async_copy(ys_hbm_ref.at[pl.ds(p, 1)], buf_ref.at[slot, kk, pl.ds(r, 1)],
                                      sem_ref.at[slot]).start()
            return carry
        lax.fori_loop(0, tm, body, 0, unroll=4)

    @pl.when(i == 0)
    def _():
        start(0, 0)

    @pl.when(i + 1 < n)
    def _():
        start(i + 1, (i + 1) % 2)

    slot = i % 2
    for kk in range(TOP_K):
        pltpu.make_async_copy(ys_hbm_ref.at[pl.ds(0, tm)], buf_ref.at[slot, kk], sem_ref.at[slot]).wait()
    y = x_ref[...]
    for kk in range(TOP_K):
        y = y + gate_ref[:, kk:kk + 1] * buf_ref[slot, kk]
    y = _rms(y, g_ref[...])

    @pl.when(i < n_a)
    def _():
        ya_ref[...] = y

    @pl.when(i >= n_a)
    def _():
        yb_ref[...] = y


def _combine(x, gates, pos, ys, g, tm, m_a):
    m, d = x.shape
    n_a = m_a // tm
    assert n_a * tm == m_a and m % tm == 0
    return pl.pallas_call(
        functools.partial(_combine_kernel, tm=tm, n_a=n_a),
        grid_spec=pltpu.PrefetchScalarGridSpec(
            num_scalar_prefetch=1,
            grid=(m // tm,),
            in_specs=[pl.BlockSpec((tm, d), lambda i, pos: (i, 0)),
                      pl.BlockSpec((tm, TOP_K), lambda i, pos: (i, 0)),
                      pl.BlockSpec((1, d), lambda i, pos: (0, 0)),
                      pl.BlockSpec(memory_space=pl.ANY)],
            out_specs=[pl.BlockSpec((tm, d), lambda i, pos: (jnp.minimum(i, n_a - 1), 0)),
                       pl.BlockSpec((tm, d), lambda i, pos: (jnp.maximum(i - n_a, 0), 0))],
            scratch_shapes=[pltpu.VMEM((2, TOP_K, tm, d), F32), pltpu.SemaphoreType.DMA((2,))]),
        out_shape=[jax.ShapeDtypeStruct((m_a, d), F32), jax.ShapeDtypeStruct((m - m_a, d), F32)],
        compiler_params=_params("arbitrary"),
        name="moe_combine",
    )(pos, x, gates, g.reshape(1, d), ys)


def _moe_plan(top_i, n_exp, blk, sub, n_blocks, max_tiles):
    m = top_i.shape[0]
    e_flat = top_i.reshape(-1)
    onehot = (e_flat[:, None] == jnp.arange(n_exp, dtype=I32)[None, :]).astype(I32)
    csum = jnp.cumsum(onehot, axis=0)
    rank = jnp.sum((csum - 1) * onehot, axis=1)
    counts = csum[-1]
    blocks_per = (counts + blk - 1) // blk
    block_end = jnp.cumsum(blocks_per)
    block_start = block_end - blocks_per
    n_used = block_end[-1]
    pos = block_start[e_flat] * blk + rank
    src = jnp.zeros((n_blocks * blk,), I32).at[pos].set(jnp.arange(m * TOP_K, dtype=I32) // TOP_K)

    s = jnp.arange(n_blocks, dtype=I32)
    bexp = jnp.minimum(jnp.sum((s[:, None] >= block_end[None, :]).astype(I32), axis=1), n_exp - 1)
    valid = jnp.clip(counts[bexp] - (s - block_start[bexp]) * blk, 0, blk)
    valid = jnp.where(s < n_used, valid, 0)
    nsub = (valid + sub - 1) // sub
    last_exp = bexp[jnp.maximum(n_used - 1, 0)]
    bexp = jnp.where(s < n_used, bexp, last_exp)

    per_blk = blk // sub
    t = jnp.arange(n_blocks * per_blk, dtype=I32)
    tile_valid = (t % per_blk) < nsub[t // per_blk]
    n_tiles = jnp.sum(tile_valid.astype(I32))
    order = jnp.argsort(jnp.logical_not(tile_valid), stable=True).astype(I32)[:max_tiles]
    last_tile = order[jnp.maximum(n_tiles - 1, 0)]
    tile_ids = jnp.where(jnp.arange(max_tiles, dtype=I32) < n_tiles, order, last_tile)
    return pos, src, bexp, nsub, n_used.reshape(1), tile_ids, n_tiles.reshape(1)


def _moe(x, top_i, gates, g_ffn, g_final, w1, w3, w2, blk, sub, tf, tm_c, m_a):
    m, d = x.shape
    n_exp = w1.shape[0]
    n_blocks = (m * TOP_K + n_exp * (blk - 1)) // blk
    max_tiles = (m * TOP_K + n_exp * (sub - 1)) // sub
    pos, src, bexp, nsub, nused, tile_ids, n_tiles = _moe_plan(top_i, n_exp, blk, sub, n_blocks, max_tiles)
    xs = _dispatch(x, g_ffn, tile_ids, n_tiles, src, n_blocks * blk, sub)
    n_main = min(n_blocks, n_exp * (-(-(m * TOP_K) // (n_exp * blk))) + 2)
    ys = _experts(xs, w1, w3, w2, bexp, nsub, nused, blk, sub, tf, 0, n_main)
    if n_main < n_blocks:
        ys = lax.cond(
            nused[0] > n_main,
            lambda y: _experts(xs, w1, w3, w2, bexp, nsub, nused, blk, sub, tf, n_main, n_blocks - n_main, ys=y),
            lambda y: y, ys)
    return _combine(x, gates, pos, ys, g_final, tm_c, m_a)


def kernel(x_prompt, x_sample, state_conv_a, state_conv_c, state_pool_d, norm_mix_g, norm_ffn_g, norm_final_g, w_in_ab, conv_a, ln_b_g, ln_b_b, w_s, b_s, w_out_ab, ffn_w1, ffn_w3, ffn_w2, w_in_cd, conv_c, conv_c_b, ln_c_g, ln_c_b, pool_w, pool_scale, w_out_cd, router, exp_w1, exp_w3, exp_w2):
    n_seq, seq_len, d = x_prompt.shape
    nseq_s, dl, _ = x_sample.shape
    mp, ms = n_seq * seq_len, nseq_s * dl
    m = mp + ms
    assert norm_mix_g.shape[0] == 2, "two layers: conv/chunk-MLP + dense FFN, then conformer/pool + MoE"
    assert seq_len % CHUNK == 0 and dl <= CHUNK and mp % ms == 0

    big = m * TOP_K >= 4 * 2560
    tm_in = _pick(m, (2176, 1088, 544, 272))
    tm_mid = _pick(m, (1088, 544, 272))
    tm_row = _pick(ms, (512, 256, 128, 64))
    t_rows = _pick(seq_len, (256, 128))
    tn = 512
    tf_up = _pick(ffn_w1.shape[2], (512, 256))
    tn_down = 256
    tf_e = 256
    blk_e, sub_e = (2560, 256) if big else (256, 128)
    tm_c = _pick(ms, (256, 128, 64))

    xp = x_prompt.reshape(mp, d)
    xs_lm = jnp.transpose(x_sample, (1, 0, 2)).reshape(ms, d)

    h = _norm_in(xp, xs_lm, norm_mix_g[0], tm_row)
    proj = _matmul(h, w_in_ab[0], tm_in, tn)
    mixed, a_p, v_p = _mix_ab_prompt(proj, conv_a[0], ln_b_g[0], ln_b_b[0], w_s[0], b_s[0],
                                     n_seq, seq_len, t_rows)
    mixed, a_s, v_s = _mix_ab_sample(mixed, proj, jnp.transpose(state_conv_a[0], (1, 0, 2)),
                                     conv_a[0], ln_b_g[0], ln_b_b[0], w_s[0], b_s[0], mp, dl, nseq_s)
    x, h = _out_proj(mixed, w_out_ab[0].astype(BF16), xp, xs_lm, norm_ffn_g[0], tm_row)
    act = _ffn_up(h, ffn_w1, ffn_w3, tm_in, tf_up)
    x = _ffn_down(act, ffn_w2, x, tm_mid, tn_down)

    h = _norm_rows(x, norm_mix_g[1], tm_mid)
    proj = _matmul(h, w_in_cd[0], tm_in, tn)
    mixed, c_p, d_p = _mix_cd_prompt(proj, conv_c[0], conv_c_b[0], ln_c_g[0], ln_c_b[0], pool_w[0],
                                     pool_scale[0], n_seq, seq_len, t_rows)
    mixed, g_s = _mix_cd_sample(mixed, proj, jnp.transpose(state_conv_c[0], (1, 0, 2)),
                                jnp.transpose(state_pool_d[0], (1, 0, 2)), conv_c[0], conv_c_b[0],
                                ln_c_g[0], ln_c_b[0], pool_w[0], pool_scale[0], mp, dl, nseq_s)
    (x,) = _out_proj(mixed, w_out_cd[0].astype(BF16), x, x, None, tm_row)
    top_i, gates = _router(x, norm_ffn_g[1], router[0], tm_mid)
    y_p, y_s = _moe(x, top_i, gates, norm_ffn_g[1], norm_final_g, exp_w1[0], exp_w3[0], exp_w2[0],
                    blk_e, sub_e, tf_e, tm_c, mp)

    dc = conv_c.shape[2]
    y_prompt = y_p.reshape(n_seq, seq_len, d)
    y_sample = jnp.transpose(y_s.reshape(dl, nseq_s, d), (1, 0, 2))
    new_a_s = jnp.transpose(a_s, (1, 0, 2))
    new_v_s = jnp.transpose(v_s.reshape(dl, nseq_s, -1), (1, 0, 2))
    g_rows = jnp.transpose(g_s.reshape(dl, nseq_s, dc), (1, 0, 2))
    p_rows = jnp.transpose(proj[mp:, 2 * dc:].reshape(dl, nseq_s, -1), (1, 0, 2))
    n_c = state_conv_c.shape[2]
    n_d = state_pool_d.shape[2]
    new_c_s = jnp.concatenate([state_conv_c[0], g_rows], axis=1)[:, -n_c:]
    new_d_s = jnp.concatenate([state_pool_d[0], p_rows], axis=1)[:, -n_d:]
    return (y_prompt, y_sample,
            a_p[None], new_a_s[None],
            v_p[None], new_v_s[None],
            c_p[None], new_c_s[None],
            d_p[None], new_d_s[None])
```

```python
import functools

import jax
import jax.numpy as jnp
from jax import lax
from jax.experimental import pallas as pl
from jax.experimental.pallas import tpu as pltpu

F32 = jnp.float32
BF16 = jnp.bfloat16
I32 = jnp.int32

EPS = 1e-6
PAST_LEN = 16384
CHUNK = 128
HEAD_B = 128
POOL_WINDOWS = (2, 4, 8, 16)
TOP_K = 2

VMEM_LIMIT_BYTES = 60 * 1024 * 1024
SUBLANES = 8
LANES = 128


def _params(*sem):
    return pltpu.CompilerParams(dimension_semantics=sem, vmem_limit_bytes=VMEM_LIMIT_BYTES)


def _pick(n, candidates):
    for c in candidates:
        if n % c == 0:
            return c
    raise ValueError(f"no tile in {candidates} divides {n}")


def _rms(x, g):
    return x * lax.rsqrt(jnp.mean(x * x, axis=-1, keepdims=True) + EPS) * g


def _layernorm(x, g, b):
    mu = jnp.mean(x, axis=-1, keepdims=True)
    xc = x - mu
    return xc * lax.rsqrt(jnp.mean(xc * xc, axis=-1, keepdims=True) + EPS) * g + b


def _silu(x):
    return x * jax.nn.sigmoid(x)


def _dot(a, b):
    return jnp.dot(a, b, preferred_element_type=F32)


def _mix_ab_prompt_kernel(bg_ref, cg_ref, xin_ref, u_ref, v_ref, cgh_ref, xinh_ref,
                          cw_ref, lng_ref, lnb_ref, ws_ref, bsb_ref,
                          o_ref, astate_ref, vstate_ref, abuf_ref):
    t = pl.program_id(1)
    t_rows, da = bg_ref.shape
    n_heads = ws_ref.shape[0]
    halo = SUBLANES

    a = cg_ref[...] * xin_ref[...]
    prev = jnp.where(t == 0, 0.0, cgh_ref[...] * xinh_ref[...])
    abuf_ref[0:halo, :] = prev
    abuf_ref[halo:halo + t_rows, :] = a
    conv = (cw_ref[0:1, :] * abuf_ref[halo - 2:halo - 2 + t_rows, :]
            + cw_ref[1:2, :] * abuf_ref[halo - 1:halo - 1 + t_rows, :]
            + cw_ref[2:3, :] * a)
    o_ref[:, 0:da] = (bg_ref[...] * conv).astype(BF16)
    astate_ref[0] = abuf_ref[halo + t_rows - 2:halo + t_rows, :]

    row = lax.broadcasted_iota(I32, (CHUNK, CHUNK), 0)
    col = lax.broadcasted_iota(I32, (CHUNK, CHUNK), 1)
    lower = row >= col
    for h in range(n_heads):
        cols = slice(h * HEAD_B, (h + 1) * HEAD_B)
        vn = _layernorm(jax.nn.gelu(v_ref[:, cols]), lng_ref[:, cols], lnb_ref[:, cols])
        vstate_ref[0, :, cols] = vn[t_rows - CHUNK:t_rows]
        w = jnp.where(lower, ws_ref[h], 0.0).astype(BF16)
        for c in range(t_rows // CHUNK):
            rows = slice(c * CHUNK, (c + 1) * CHUNK)
            mix = _dot(w, vn[rows].astype(BF16)) + bsb_ref[h]
            o_ref[rows, da + h * HEAD_B:da + (h + 1) * HEAD_B] = (
                jax.nn.gelu(u_ref[rows, cols]) * mix).astype(BF16)


def _mix_ab_prompt(proj, conv_a, ln_g, ln_b, w_s, b_s, n_seq, seq_len, t_rows):
    m = proj.shape[0]
    da = conv_a.shape[1]
    db = ln_g.shape[0]
    n_heads = w_s.shape[0]
    tiles = seq_len // t_rows
    halo_blocks = t_rows // SUBLANES

    def col_spec(c):
        return pl.BlockSpec((t_rows, da), lambda b, t: (b * tiles + t, c))

    def halo_spec(c):
        return pl.BlockSpec(
            (SUBLANES, da),
            lambda b, t: (jnp.maximum((b * tiles + t) * halo_blocks - 1, 0), c))

    def full(shape):
        return pl.BlockSpec(shape, lambda b, t: (0,) * len(shape))

    bsb = jnp.broadcast_to(b_s[:, :CHUNK, None], (n_heads, CHUNK, HEAD_B))
    return pl.pallas_call(
        _mix_ab_prompt_kernel,
        grid=(n_seq, tiles),
        in_specs=[col_spec(0), col_spec(1), col_spec(2), col_spec(3), col_spec(4),
                  halo_spec(1), halo_spec(2),
                  full((conv_a.shape[0], da)), full((1, db)), full((1, db)),
                  full((n_heads, CHUNK, CHUNK)), full((n_heads, CHUNK, HEAD_B))],
        out_specs=[pl.BlockSpec((t_rows, da + db), lambda b, t: (b * tiles + t, 0)),
                   pl.BlockSpec((1, 2, da), lambda b, t: (b, 0, 0)),
                   pl.BlockSpec((1, CHUNK, db), lambda b, t: (b, 0, 0))],
        out_shape=[jax.ShapeDtypeStruct((m, da + db), BF16),
                   jax.ShapeDtypeStruct((n_seq, 2, da), F32),
                   jax.ShapeDtypeStruct((n_seq, CHUNK, db), F32)],
        scratch_shapes=[pltpu.VMEM((SUBLANES + t_rows, da), F32)],
        compiler_params=_params("parallel", "arbitrary"),
        name="mix_ab_prompt",
    )(proj, proj, proj, proj, proj, proj, proj, conv_a, ln_g.reshape(1, db), ln_b.reshape(1, db),
      w_s[:, :CHUNK, :CHUNK], bsb)


def _mix_ab_sample_kernel(mixed_hbm_ref, bg_ref, cg_ref, xin_ref, u_ref, v_ref, st_ref,
                          cw_ref, lng_ref, lnb_ref, wsl_ref, bsl_ref,
                          o_ref, astate_ref, vstate_ref, *, dl, nseq):
    del mixed_hbm_ref
    da = bg_ref.shape[1]
    n_heads = lng_ref.shape[1] // HEAD_B
    n_state = st_ref.shape[0]

    def slab(l):
        return slice(l * nseq, (l + 1) * nseq)

    a_pad = [st_ref[k] for k in range(n_state)]
    a_pad += [cg_ref[slab(l), :] * xin_ref[slab(l), :] for l in range(dl)]
    for l in range(dl):
        conv = (cw_ref[0:1, :] * a_pad[l] + cw_ref[1:2, :] * a_pad[l + 1]
                + cw_ref[2:3, :] * a_pad[l + 2])
        o_ref[slab(l), 0:da] = (bg_ref[slab(l), :] * conv).astype(BF16)
    for k in range(n_state):
        astate_ref[k] = a_pad[dl + k]

    for h in range(n_heads):
        cols = slice(h * HEAD_B, (h + 1) * HEAD_B)
        vn = _layernorm(jax.nn.gelu(v_ref[:, cols]), lng_ref[:, cols], lnb_ref[:, cols])
        vstate_ref[:, cols] = vn
        for i in range(dl):
            mix = bsl_ref[i:i + 1, cols]
            for j in range(i + 1):
                mix = mix + wsl_ref[i * dl + j:i * dl + j + 1, cols] * vn[slab(j)]
            o_ref[slab(i), da + h * HEAD_B:da + (h + 1) * HEAD_B] = (
                jax.nn.gelu(u_ref[slab(i), cols]) * mix).astype(BF16)


def _mix_ab_sample(mixed, proj, state_lm, conv_a, ln_g, ln_b, w_s, b_s, row0, dl, nseq):
    ms = dl * nseq
    da = conv_a.shape[1]
    db = ln_g.shape[0]
    n_heads = w_s.shape[0]
    rb = row0 // ms

    def col_spec(c):
        return pl.BlockSpec((ms, da), lambda i: (rb, c))

    def full(shape):
        return pl.BlockSpec(shape, lambda i: (0,) * len(shape))

    wsl = jnp.repeat(jnp.transpose(w_s[:, :dl, :dl], (1, 2, 0)).reshape(dl * dl, n_heads), HEAD_B, axis=1)
    bsl = jnp.repeat(jnp.transpose(b_s[:, :dl], (1, 0)), HEAD_B, axis=1)
    return pl.pallas_call(
        functools.partial(_mix_ab_sample_kernel, dl=dl, nseq=nseq),
        grid=(1,),
        in_specs=[pl.BlockSpec(memory_space=pl.ANY),
                  col_spec(0), col_spec(1), col_spec(2), col_spec(3), col_spec(4),
                  full(state_lm.shape), full((conv_a.shape[0], da)), full((1, db)), full((1, db)),
                  full((dl * dl, db)), full((dl, db))],
        out_specs=[pl.BlockSpec((ms, da + db), lambda i: (rb, 0)),
                   full(state_lm.shape),
                   full((ms, db))],
        out_shape=[jax.ShapeDtypeStruct(mixed.shape, BF16),
                   jax.ShapeDtypeStruct(state_lm.shape, F32),
                   jax.ShapeDtypeStruct((ms, db), F32)],
        input_output_aliases={0: 0},
        compiler_params=_params("arbitrary"),
        name="mix_ab_sample",
    )(mixed, proj, proj, proj, proj, proj, state_lm, conv_a, ln_g.reshape(1, db), ln_b.reshape(1, db),
      wsl, bsl)


def _mix_cd_prompt_kernel(ga_ref, gb_ref, p_ref, gah_ref, gbh_ref, ph_ref,
                          cw_ref, cb_ref, lng_ref, lnb_ref, pw_ref, ps_ref,
                          o_ref, cstate_ref, dstate_ref, gbuf_ref, pbuf_ref, cbuf_ref, shift_ref):
    t = pl.program_id(1)
    t_rows, dc = ga_ref.shape
    k_c = cw_ref.shape[0]
    c_halo = gah_ref.shape[0]
    d_halo = ph_ref.shape[0]
    gd = pw_ref.shape[1]

    g = ga_ref[...] * jax.nn.sigmoid(gb_ref[...])
    gbuf_ref[0:c_halo, :] = jnp.where(t == 0, 0.0, gah_ref[...] * jax.nn.sigmoid(gbh_ref[...]))
    gbuf_ref[c_halo:c_halo + t_rows, :] = g
    n_shift = shift_ref.shape[1]
    for r in range(1, SUBLANES):
        shift_ref[r - 1] = gbuf_ref[r:r + n_shift, :]
    first = c_halo - (k_c - 1)
    for cblk in range(dc // LANES):
        cols = slice(cblk * LANES, (cblk + 1) * LANES)
        acc = cb_ref[:, cols]
        for k in range(k_c):
            q, r = divmod(first + k, SUBLANES)
            if r == 0:
                win = gbuf_ref[first + k:first + k + t_rows, cols]
            else:
                win = shift_ref[r - 1, q * SUBLANES:q * SUBLANES + t_rows, cols]
            acc = acc + cw_ref[k:k + 1, cols] * win
        cbuf_ref[:, cols] = acc
    o_ref[:, 0:dc] = _silu(_layernorm(cbuf_ref[...], lng_ref[...], lnb_ref[...])).astype(BF16)
    cstate_ref[0] = gbuf_ref[c_halo + t_rows - (k_c - 1):c_halo + t_rows, :]

    pbuf_ref[0:d_halo, :] = jnp.where(t == 0, 0.0, ph_ref[...])
    pbuf_ref[d_halo:d_halo + t_rows, :] = p_ref[...]
    pos = t * t_rows + lax.broadcasted_iota(I32, (t_rows, 1), 0)
    for gi, w in enumerate(POOL_WINDOWS):
        cols = slice(gi * gd, (gi + 1) * gd)
        cur = p_ref[:, cols]
        s = cur
        for k in range(1, w):
            s = s + pbuf_ref[d_halo - k:d_halo - k + t_rows, cols]
        cnt = jnp.minimum(w, pos + 1).astype(F32)
        diff = (s / cnt - cur).astype(BF16)
        d = _dot(diff, pw_ref[gi].astype(BF16)) * ps_ref[:, cols]
        o_ref[:, dc + gi * gd:dc + (gi + 1) * gd] = d.astype(BF16)
    n_keep = dstate_ref.shape[1]
    dstate_ref[0] = pbuf_ref[d_halo + t_rows - n_keep:d_halo + t_rows, :]


def _mix_cd_prompt(proj, conv_c, conv_c_b, ln_g, ln_b, pool_w, pool_scale, n_seq, seq_len, t_rows):
    m = proj.shape[0]
    k_c, dc = conv_c.shape
    dd = pool_scale.shape[0]
    n_pool, gd = pool_w.shape[0], pool_w.shape[1]
    tiles = seq_len // t_rows
    c_halo = -(-(k_c - 1) // SUBLANES) * SUBLANES
    d_halo = -(-(max(POOL_WINDOWS) - 1) // SUBLANES) * SUBLANES
    assert t_rows % c_halo == 0 and t_rows % d_halo == 0 and dc == dd

    def col_spec(c):
        return pl.BlockSpec((t_rows, dc), lambda b, t: (b * tiles + t, c))

    def halo_spec(c, rows):
        per = t_rows // rows
        return pl.BlockSpec((rows, dc),
                            lambda b, t: (jnp.maximum((b * tiles + t) * per - 1, 0), c))

    def full(shape):
        return pl.BlockSpec(shape, lambda b, t: (0,) * len(shape))

    return pl.pallas_call(
        _mix_cd_prompt_kernel,
        grid=(n_seq, tiles),
        in_specs=[col_spec(0), col_spec(1), col_spec(2),
                  halo_spec(0, c_halo), halo_spec(1, c_halo), halo_spec(2, d_halo),
                  full((k_c, dc)), full((1, dc)), full((1, dc)), full((1, dc)),
                  full((n_pool, gd, gd)), full((1, dd))],
        out_specs=[pl.BlockSpec((t_rows, dc + dd), lambda b, t: (b * tiles + t, 0)),
                   pl.BlockSpec((1, k_c - 1, dc), lambda b, t: (b, 0, 0)),
                   pl.BlockSpec((1, max(POOL_WINDOWS) - 1, dd), lambda b, t: (b, 0, 0))],
        out_shape=[jax.ShapeDtypeStruct((m, dc + dd), BF16),
                   jax.ShapeDtypeStruct((n_seq, k_c - 1, dc), F32),
                   jax.ShapeDtypeStruct((n_seq, max(POOL_WINDOWS) - 1, dd), F32)],
        scratch_shapes=[pltpu.VMEM((c_halo + t_rows, dc), F32),
                        pltpu.VMEM((d_halo + t_rows, dd), F32),
                        pltpu.VMEM((t_rows, dc), F32),
                        pltpu.VMEM((SUBLANES - 1, c_halo + t_rows - SUBLANES, dc), F32)],
        compiler_params=_params("parallel", "arbitrary"),
        name="mix_cd_prompt",
    )(proj, proj, proj, proj, proj, proj, conv_c, conv_c_b.reshape(1, dc), ln_g.reshape(1, dc),
      ln_b.reshape(1, dc), pool_w, pool_scale.reshape(1, dd))


def _mix_cd_sample_kernel(mixed_hbm_ref, ga_ref, gb_ref, p_ref, stc_ref, std_ref,
                          cw_ref, cb_ref, lng_ref, lnb_ref, pw_ref, ps_ref,
                          o_ref, g_ref, cbuf_ref, dbuf_ref, *, dl, nseq, start_pos):
    del mixed_hbm_ref
    gi = pl.program_id(0)
    n_groups = pl.num_programs(0)
    gd = ga_ref.shape[1]
    k_c = cw_ref.shape[0]
    n_c = stc_ref.shape[0]
    n_d = std_ref.shape[0]

    def slab(l):
        return slice(l * nseq, (l + 1) * nseq)

    g = ga_ref[...] * jax.nn.sigmoid(gb_ref[...])
    g_ref[...] = g
    c_pad = [stc_ref[k] for k in range(n_c)] + [g[slab(l)] for l in range(dl)]
    d_pad = [std_ref[k] for k in range(n_d)] + [p_ref[slab(l), :] for l in range(dl)]
    for l in range(dl):
        acc = cb_ref[...] + cw_ref[0:1, :] * c_pad[l]
        for k in range(1, k_c):
            acc = acc + cw_ref[k:k + 1, :] * c_pad[l + k]
        cbuf_ref[gi, slab(l), :] = acc

        cur = d_pad[n_d + l]
        s = cur
        pooled = jnp.zeros_like(cur)
        for k in range(1, max(POOL_WINDOWS) + 1):
            if k in POOL_WINDOWS:
                cnt = float(min(k, start_pos + l + 1))
                pooled = jnp.where(gi == POOL_WINDOWS.index(k), s / cnt, pooled)
            if k < max(POOL_WINDOWS):
                s = s + d_pad[n_d + l - k]
        diff = (pooled - cur).astype(BF16)
        dbuf_ref[gi, slab(l), :] = (_dot(diff, pw_ref[0].astype(BF16)) * ps_ref[...]).astype(BF16)

    @pl.when(gi == n_groups - 1)
    def _():
        n_blk = cbuf_ref.shape[0]
        dc = n_blk * gd
        total = cbuf_ref[0].sum(axis=-1, keepdims=True)
        for b in range(1, n_blk):
            total = total + cbuf_ref[b].sum(axis=-1, keepdims=True)
        mu = total / dc
        var = None
        for b in range(n_blk):
            xc = cbuf_ref[b] - mu
            sq = (xc * xc).sum(axis=-1, keepdims=True)
            var = sq if var is None else var + sq
        rstd = lax.rsqrt(var / dc + EPS)
        for b in range(n_blk):
            cols = slice(b * gd, (b + 1) * gd)
            y = (cbuf_ref[b] - mu) * rstd * lng_ref[:, cols] + lnb_ref[:, cols]
            o_ref[:, cols] = _silu(y).astype(BF16)
            o_ref[:, dc + b * gd:dc + (b + 1) * gd] = dbuf_ref[b]


def _mix_cd_sample(mixed, proj, stc_lm, std_lm, conv_c, conv_c_b, ln_g, ln_b, pool_w, pool_scale,
                   row0, dl, nseq):
    ms = dl * nseq
    k_c, dc = conv_c.shape
    dd = pool_scale.shape[0]
    n_pool, gd = pool_w.shape[0], pool_w.shape[1]
    assert dc == dd == n_pool * gd and n_pool == len(POOL_WINDOWS)
    rb = row0 // ms

    return pl.pallas_call(
        functools.partial(_mix_cd_sample_kernel, dl=dl, nseq=nseq, start_pos=PAST_LEN),
        grid=(n_pool,),
        in_specs=[pl.BlockSpec(memory_space=pl.ANY),
                  pl.BlockSpec((ms, gd), lambda i: (rb, i)),
                  pl.BlockSpec((ms, gd), lambda i: (rb, n_pool + i)),
                  pl.BlockSpec((ms, gd), lambda i: (rb, 2 * n_pool + i)),
                  pl.BlockSpec((stc_lm.shape[0], nseq, gd), lambda i: (0, 0, i)),
                  pl.BlockSpec((std_lm.shape[0], nseq, gd), lambda i: (0, 0, i)),
                  pl.BlockSpec((k_c, gd), lambda i: (0, i)),
                  pl.BlockSpec((1, gd), lambda i: (0, i)),
                  pl.BlockSpec((1, dc), lambda i: (0, 0)),
                  pl.BlockSpec((1, dc), lambda i: (0, 0)),
                  pl.BlockSpec((1, gd, gd), lambda i: (i, 0, 0)),
                  pl.BlockSpec((1, gd), lambda i: (0, i))],
        out_specs=[pl.BlockSpec((ms, dc + dd), lambda i: (rb, 0)),
                   pl.BlockSpec((ms, gd), lambda i: (0, i))],
        out_shape=[jax.ShapeDtypeStruct(mixed.shape, BF16),
                   jax.ShapeDtypeStruct((ms, dc), F32)],
        scratch_shapes=[pltpu.VMEM((n_pool, ms, gd), F32),
                        pltpu.VMEM((n_pool, ms, gd), BF16)],
        input_output_aliases={0: 0},
        compiler_params=_params("arbitrary"),
        name="mix_cd_sample",
    )(mixed, proj, proj, proj, stc_lm, std_lm, conv_c, conv_c_b.reshape(1, dc), ln_g.reshape(1, dc),
      ln_b.reshape(1, dc), pool_w, pool_scale.reshape(1, dd))


def _norm_in_kernel(xp_ref, xs_ref, g_ref, o_ref, *, n_a):
    i = pl.program_id(0)

    @pl.when(i < n_a)
    def _():
        o_ref[...] = _rms(xp_ref[...], g_ref[...]).astype(BF16)

    @pl.when(i >= n_a)
    def _():
        o_ref[...] = _rms(xs_ref[...], g_ref[...]).astype(BF16)


def _norm_in(xp, xs, g, tm):
    mp, d = xp.shape
    ms = xs.shape[0]
    n_a, n_b = mp // tm, ms // tm
    return pl.pallas_call(
        functools.partial(_norm_in_kernel, n_a=n_a),
        grid=(n_a + n_b,),
        in_specs=[pl.BlockSpec((tm, d), lambda i: (jnp.minimum(i, n_a - 1), 0)),
                  pl.BlockSpec((tm, d), lambda i: (jnp.maximum(i - n_a, 0), 0)),
                  pl.BlockSpec((1, d), lambda i: (0, 0))],
        out_specs=pl.BlockSpec((tm, d), lambda i: (i, 0)),
        out_shape=jax.ShapeDtypeStruct((mp + ms, d), BF16),
        compiler_params=_params("arbitrary"),
        name="norm_in",
    )(xp, xs, g.reshape(1, d))


def _matmul_kernel(a_ref, w_ref, o_ref):
    o_ref[...] = _dot(a_ref[...], w_ref[...].astype(BF16))


def _matmul(a, w, tm, tn):
    m, k = a.shape
    n = w.shape[1]
    return pl.pallas_call(
        _matmul_kernel,
        grid=(m // tm, n // tn),
        in_specs=[pl.BlockSpec((tm, k), lambda i, j: (i, 0)),
                  pl.BlockSpec((k, tn), lambda i, j: (0, j))],
        out_specs=pl.BlockSpec((tm, tn), lambda i, j: (i, j)),
        out_shape=jax.ShapeDtypeStruct((m, n), F32),
        compiler_params=_params("parallel", "arbitrary"),
        name="in_proj",
    )(a, w)


def _ffn_up_kernel(h_ref, w1_ref, w3_ref, o_ref):
    h = h_ref[...]
    a = _dot(h, w1_ref[0].astype(BF16))
    b = _dot(h, w3_ref[0].astype(BF16))
    o_ref[...] = (_silu(a) * b).astype(BF16)


def _ffn_up(h, w1, w3, tm, tf):
    m, d = h.shape
    f = w1.shape[2]
    return pl.pallas_call(
        _ffn_up_kernel,
        grid=(m // tm, f // tf),
        in_specs=[pl.BlockSpec((tm, d), lambda i, j: (i, 0)),
                  pl.BlockSpec((1, d, tf), lambda i, j: (0, 0, j)),
                  pl.BlockSpec((1, d, tf), lambda i, j: (0, 0, j))],
        out_specs=pl.BlockSpec((tm, tf), lambda i, j: (i, j)),
        out_shape=jax.ShapeDtypeStruct((m, f), BF16),
        compiler_params=_params("parallel", "arbitrary"),
        name="ffn_up",
    )(h, w1, w3)


def _ffn_down_kernel(a_ref, w_ref, r_ref, o_ref):
    o_ref[...] = r_ref[...] + _dot(a_ref[...], w_ref[0].astype(BF16))


def _ffn_down(act, w2, res, tm, tn):
    m, f = act.shape
    d = w2.shape[2]
    return pl.pallas_call(
        _ffn_down_kernel,
        grid=(m // tm, d // tn),
        in_specs=[pl.BlockSpec((tm, f), lambda i, j: (i, 0)),
                  pl.BlockSpec((1, f, tn), lambda i, j: (0, 0, j)),
                  pl.BlockSpec((tm, tn), lambda i, j: (i, j))],
        out_specs=pl.BlockSpec((tm, tn), lambda i, j: (i, j)),
        out_shape=jax.ShapeDtypeStruct((m, d), F32),
        compiler_params=_params("parallel", "arbitrary"),
        name="ffn_down",
    )(act, w2, res)


def _norm_rows_kernel(x_ref, g_ref, o_ref):
    o_ref[...] = _rms(x_ref[...], g_ref[...]).astype(BF16)


def _norm_rows(x, g, tm):
    m, d = x.shape
    return pl.pallas_call(
        _norm_rows_kernel,
        grid=(m // tm,),
        in_specs=[pl.BlockSpec((tm, d), lambda i: (i, 0)),
                  pl.BlockSpec((1, d), lambda i: (0, 0))],
        out_specs=pl.BlockSpec((tm, d), lambda i: (i, 0)),
        out_shape=jax.ShapeDtypeStruct((m, d), BF16),
        compiler_params=_params("parallel"),
        name="norm_rows",
    )(x, g.reshape(1, d))


def _out_proj_kernel(a_ref, w_ref, ra_ref, rb_ref, *rest, n_a, with_h):
    i = pl.program_id(0)
    if with_h:
        g_ref, x_ref, h_ref = rest
    else:
        (x_ref,) = rest
    tm = a_ref.shape[0]
    n_parts = 2 if tm % (2 * 2 * SUBLANES) == 0 else 1
    rows_per = tm // n_parts
    for part in range(n_parts):
        rows = slice(part * rows_per, (part + 1) * rows_per)
        res = jnp.where(i < n_a, ra_ref[rows, :], rb_ref[rows, :])
        x = res + _dot(a_ref[rows, :], w_ref[...])
        x_ref[rows, :] = x
        if with_h:
            h_ref[rows, :] = _rms(x, g_ref[...]).astype(BF16)


def _out_proj(a, w_bf16, res_a, res_b, g, tm):
    m, k = a.shape
    d = w_bf16.shape[1]
    n_a = res_a.shape[0] // tm
    n_b = m // tm - n_a
    assert n_a * tm == res_a.shape[0] and (n_b == 0 or n_b * tm == res_b.shape[0])
    with_h = g is not None
    in_specs = [pl.BlockSpec((tm, k), lambda i: (i, 0)),
                pl.BlockSpec((k, d), lambda i: (0, 0), pipeline_mode=pl.Buffered(1)),
                pl.BlockSpec((tm, d), lambda i: (jnp.minimum(i, n_a - 1), 0)),
                pl.BlockSpec((tm, d), lambda i: (jnp.maximum(i - n_a, 0), 0))]
    args = [a, w_bf16, res_a, res_b]
    row_spec = pl.BlockSpec((tm, d), lambda i: (i, 0))
    out_specs = [row_spec]
    out_shape = [jax.ShapeDtypeStruct((m, d), F32)]
    if with_h:
        in_specs.append(pl.BlockSpec((1, d), lambda i: (0, 0)))
        args.append(g.reshape(1, d))
        out_specs.append(row_spec)
        out_shape.append(jax.ShapeDtypeStruct((m, d), BF16))
    return pl.pallas_call(
        functools.partial(_out_proj_kernel, n_a=n_a, with_h=with_h),
        grid=(m // tm,),
        in_specs=in_specs, out_specs=out_specs, out_shape=out_shape,
        compiler_params=_params("arbitrary"),
        name="out_proj_h" if with_h else "out_proj",
    )(*args)


def _router_kernel(x_ref, g_ref, r_ref, idx_ref, gate_ref):
    h = _rms(x_ref[...], g_ref[...])
    h_hi = h.astype(BF16)
    h_lo = (h - h_hi.astype(F32)).astype(BF16)
    r = r_ref[...]
    r_hi = r.astype(BF16)
    r_lo = (r - r_hi.astype(F32)).astype(BF16)
    logits = _dot(h_hi, r_hi) + (_dot(h_hi, r_lo) + _dot(h_lo, r_hi))
    n_exp = float(logits.shape[1])
    lane = lax.broadcasted_iota(I32, logits.shape, 1).astype(F32)
    m1 = jnp.max(logits, axis=-1, keepdims=True)
    i1 = jnp.min(jnp.where(logits == m1, lane, n_exp), axis=-1, keepdims=True)
    rest = jnp.where(lane == i1, -jnp.inf, logits)
    m2 = jnp.max(rest, axis=-1, keepdims=True)
    i2 = jnp.min(jnp.where(rest == m2, lane, n_exp), axis=-1, keepdims=True)
    e = jnp.exp(m2 - m1)
    two = lax.broadcasted_iota(I32, idx_ref.shape, 1)
    idx_ref[...] = jnp.where(two == 0, i1, i2).astype(I32)
    gate_ref[...] = jnp.where(two == 0, 1.0, e) / (1.0 + e)


def _router(x, g, router, tm):
    m, d = x.shape
    n_exp = router.shape[1]
    return pl.pallas_call(
        _router_kernel,
        grid=(m // tm,),
        in_specs=[pl.BlockSpec((tm, d), lambda i: (i, 0)),
                  pl.BlockSpec((1, d), lambda i: (0, 0)),
                  pl.BlockSpec((d, n_exp), lambda i: (0, 0))],
        out_specs=[pl.BlockSpec((tm, TOP_K), lambda i: (i, 0)),
                   pl.BlockSpec((tm, TOP_K), lambda i: (i, 0))],
        out_shape=[jax.ShapeDtypeStruct((m, TOP_K), I32),
                   jax.ShapeDtypeStruct((m, TOP_K), F32)],
        compiler_params=_params("parallel"),
        name="router",
    )(x, g.reshape(1, d), router)


def _experts_kernel(bexp_ref, nsub_ref, nused_ref, *refs, sub, s0):
    if s0:
        refs = refs[1:]
    xs_ref, w1_ref, w3_ref, w2_ref, o_ref, w1b_ref, w3b_ref, w2b_ref = refs
    s = pl.program_id(0) + s0
    j = pl.program_id(1)

    tile = 2 * sub

    def rows_at(unit, size):
        return pl.ds(pl.multiple_of(unit * sub, sub), size)

    @pl.when(s < nused_ref[0])
    def _():
        n = nsub_ref[s]

        @pl.when(j == 0)
        def _():
            def init(i, carry):
                o_ref[rows_at(i, sub), :] = jnp.zeros((sub, o_ref.shape[1]), F32)
                return carry
            lax.fori_loop(0, n, init, 0)

        w1b_ref[...] = w1_ref[0].astype(BF16)
        w3b_ref[...] = w3_ref[0].astype(BF16)
        w2b_ref[...] = w2_ref[0].astype(BF16)

        def up(unit, size):
            x = xs_ref[rows_at(unit, size), :]
            a = _dot(x, w1b_ref[...])
            b = _dot(x, w3b_ref[...])
            return (_silu(a) * b).astype(BF16)

        def down(unit, size, act):
            o_ref[rows_at(unit, size), :] += _dot(act, w2b_ref[...])

        def pair(p, carry):
            act0 = up(4 * p, tile)
            act1 = up(4 * p + 2, tile)
            down(4 * p, tile, act0)
            down(4 * p + 2, tile, act1)
            return carry
        lax.fori_loop(0, n // 4, pair, 0)

        done = (n // 4) * 4
        rem = n - done

        @pl.when(rem >= 2)
        def _():
            down(done, tile, up(done, tile))

        @pl.when(rem % 2 == 1)
        def _():
            last = n - 1
            down(last, sub, up(last, sub))


def _experts(xs, w1, w3, w2, bexp, nsub, nused, blk, sub, tf, s0, n_s, ys=None):
    n_rows, d = xs.shape
    f = w1.shape[2]
    n_f = f // tf
    assert (ys is None) == (s0 == 0)

    def blk_idx(s, nu):
        return jnp.minimum(s + s0, jnp.maximum(nu[0] - 1, s0))

    def exp_idx(s, be, nu):
        return be[blk_idx(s, nu)]

    def jj(s, j, nu):
        return jnp.where(s + s0 < nu[0], j, n_f - 1)

    row_spec = pl.BlockSpec((blk, d), lambda s, j, be, ns, nu: (blk_idx(s, nu), 0),
                            pipeline_mode=pl.Buffered(1))
    in_specs = [row_spec,
                pl.BlockSpec((1, d, tf), lambda s, j, be, ns, nu: (exp_idx(s, be, nu), 0, jj(s, j, nu))),
                pl.BlockSpec((1, d, tf), lambda s, j, be, ns, nu: (exp_idx(s, be, nu), 0, jj(s, j, nu))),
                pl.BlockSpec((1, tf, d), lambda s, j, be, ns, nu: (exp_idx(s, be, nu), jj(s, j, nu), 0))]
    args = [xs, w1, w3, w2]
    aliases = {}
    if s0:
        in_specs = [pl.BlockSpec(memory_space=pl.ANY)] + in_specs
        args = [ys] + args
        aliases = {3: 0}
    return pl.pallas_call(
        functools.partial(_experts_kernel, sub=sub, s0=s0),
        grid_spec=pltpu.PrefetchScalarGridSpec(
            num_scalar_prefetch=3,
            grid=(n_s, n_f),
            in_specs=in_specs,
            out_specs=row_spec,
            scratch_shapes=[pltpu.VMEM((d, tf), BF16), pltpu.VMEM((d, tf), BF16),
                            pltpu.VMEM((tf, d), BF16)]),
        out_shape=jax.ShapeDtypeStruct((n_rows, d), F32),
        input_output_aliases=aliases,
        compiler_params=_params("arbitrary", "arbitrary"),
        name="moe_experts" if s0 == 0 else "moe_experts_overflow",
    )(bexp, nsub, nused, *args)


def _dispatch_kernel(tile_ids_ref, n_tiles_ref, src_ref, x_hbm_ref, g_ref, o_ref, buf_ref, sem_ref, *, sub):
    k = pl.program_id(0)
    n_valid = n_tiles_ref[0]

    def start(step, slot):
        base = tile_ids_ref[step] * sub

        def body(r, carry):
            tok = src_ref[base + r]
            pltpu.make_async_copy(x_hbm_ref.at[pl.ds(tok, 1)], buf_ref.at[slot, pl.ds(r, 1)],
                                  sem_ref.at[slot]).start()
            return carry
        lax.fori_loop(0, sub, body, 0, unroll=8)

    @pl.when(jnp.logical_and(k == 0, n_valid > 0))
    def _():
        start(0, 0)

    @pl.when(k + 1 < n_valid)
    def _():
        start(k + 1, (k + 1) % 2)

    @pl.when(k < n_valid)
    def _():
        slot = k % 2
        pltpu.make_async_copy(x_hbm_ref.at[pl.ds(0, sub)], buf_ref.at[slot], sem_ref.at[slot]).wait()
        o_ref[...] = _rms(buf_ref[slot], g_ref[...]).astype(BF16)


def _dispatch(x, g, tile_ids, n_tiles, src, n_slots, sub):
    m, d = x.shape
    max_tiles = tile_ids.shape[0]
    return pl.pallas_call(
        functools.partial(_dispatch_kernel, sub=sub),
        grid_spec=pltpu.PrefetchScalarGridSpec(
            num_scalar_prefetch=3,
            grid=(max_tiles,),
            in_specs=[pl.BlockSpec(memory_space=pl.ANY),
                      pl.BlockSpec((1, d), lambda k, ids, nt, src: (0, 0))],
            out_specs=pl.BlockSpec((sub, d), lambda k, ids, nt, src: (ids[k], 0)),
            scratch_shapes=[pltpu.VMEM((2, sub, d), F32), pltpu.SemaphoreType.DMA((2,))]),
        out_shape=jax.ShapeDtypeStruct((n_slots, d), BF16),
        compiler_params=_params("arbitrary"),
        name="moe_dispatch",
    )(tile_ids, n_tiles, src, x, g.reshape(1, d))


def _combine_kernel(pos_ref, x_ref, gate_ref, g_ref, ys_hbm_ref, ya_ref, yb_ref, buf_ref, sem_ref, *, tm, n_a):
    i = pl.program_id(0)
    n = pl.num_programs(0)

    def start(step, slot):
        base = step * (tm * TOP_K)

        def body(r, carry):
            for kk in range(TOP_K):
                p = pos_ref[base + r * TOP_K + kk]
                pltpu.make_async_copy(ys_hbm_ref.at[pl.ds(p, 1)], buf_ref.at[slot, kk, pl.ds(r, 1)],
                                      sem_ref.at[slot]).start()
            return carry
        lax.fori_loop(0, tm, body, 0, unroll=4)

    @pl.when(i == 0)
    def _():
        start(0, 0)

    @pl.when(i + 1 < n)
    def _():
        start(i + 1, (i + 1) % 2)

    slot = i % 2
    for kk in range(TOP_K):
        pltpu.make_async_copy(ys_hbm_ref.at[pl.ds(0, tm)], buf_ref.at[slot, kk], sem_ref.at[slot]).wait()
    y = x_ref[...]
    for kk in range(TOP_K):
        y = y + gate_ref[:, kk:kk + 1] * buf_ref[slot, kk]
    y = _rms(y, g_ref[...])

    @pl.when(i < n_a)
    def _():
        ya_ref[...] = y

    @pl.when(i >= n_a)
    def _():
        yb_ref[...] = y


def _combine(x, gates, pos, ys, g, tm, m_a):
    m, d = x.shape
    n_a = m_a // tm
    assert n_a * tm == m_a and m % tm == 0
    return pl.pallas_call(
        functools.partial(_combine_kernel, tm=tm, n_a=n_a),
        grid_spec=pltpu.PrefetchScalarGridSpec(
            num_scalar_prefetch=1,
            grid=(m // tm,),
            in_specs=[pl.BlockSpec((tm, d), lambda i, pos: (i, 0)),
                      pl.BlockSpec((tm, TOP_K), lambda i, pos: (i, 0)),
                      pl.BlockSpec((1, d), lambda i, pos: (0, 0)),
                      pl.BlockSpec(memory_space=pl.ANY)],
            out_specs=[pl.BlockSpec((tm, d), lambda i, pos: (jnp.minimum(i, n_a - 1), 0)),
                       pl.BlockSpec((tm, d), lambda i, pos: (jnp.maximum(i - n_a, 0), 0))],
            scratch_shapes=[pltpu.VMEM((2, TOP_K, tm, d), F32), pltpu.SemaphoreType.DMA((2,))]),
        out_shape=[jax.ShapeDtypeStruct((m_a, d), F32), jax.ShapeDtypeStruct((m - m_a, d), F32)],
        compiler_params=_params("arbitrary"),
        name="moe_combine",
    )(pos, x, gates, g.reshape(1, d), ys)


def _moe_plan(top_i, n_exp, blk, sub, n_blocks, max_tiles):
    m = top_i.shape[0]
    e_flat = top_i.reshape(-1)
    onehot = (e_flat[:, None] == jnp.arange(n_exp, dtype=I32)[None, :]).astype(I32)
    csum = jnp.cumsum(onehot, axis=0)
    rank = jnp.sum((csum - 1) * onehot, axis=1)
    counts = csum[-1]
    blocks_per = (counts + blk - 1) // blk
    block_end = jnp.cumsum(blocks_per)
    block_start = block_end - blocks_per
    n_used = block_end[-1]
    pos = block_start[e_flat] * blk + rank

    s = jnp.arange(n_blocks, dtype=I32)
    bexp = jnp.minimum(jnp.sum((s[:, None] >= block_end[None, :]).astype(I32), axis=1), n_exp - 1)
    valid = jnp.clip(counts[bexp] - (s - block_start[bexp]) * blk, 0, blk)
    valid = jnp.where(s < n_used, valid, 0)

    by_expert = jnp.argsort(e_flat, stable=True).astype(I32) // TOP_K
    first_pair = jnp.cumsum(counts) - counts
    slot = jnp.arange(n_blocks * blk, dtype=I32)
    sb = slot // blk
    within = slot - block_start[bexp[sb]] * blk
    src = jnp.where(jnp.logical_and(sb < n_used, within < counts[bexp[sb]]),
                    by_expert[jnp.clip(first_pair[bexp[sb]] + within, 0, m * TOP_K - 1)], 0)
    nsub = (valid + sub - 1) // sub
    last_exp = bexp[jnp.maximum(n_used - 1, 0)]
    bexp = jnp.where(s < n_used, bexp, last_exp)

    per_blk = blk // sub
    t = jnp.arange(n_blocks * per_blk, dtype=I32)
    tile_valid = (t % per_blk) < nsub[t // per_blk]
    n_tiles = jnp.sum(tile_valid.astype(I32))
    order = jnp.argsort(jnp.logical_not(tile_valid), stable=True).astype(I32)[:max_tiles]
    last_tile = order[jnp.maximum(n_tiles - 1, 0)]
    tile_ids = jnp.where(jnp.arange(max_tiles, dtype=I32) < n_tiles, order, last_tile)
    return pos, src, bexp, nsub, n_used.reshape(1), tile_ids, n_tiles.reshape(1)


def _moe(x, top_i, gates, g_ffn, g_final, w1, w3, w2, blk, sub, tf, tm_c, m_a):
    m, d = x.shape
    n_exp = w1.shape[0]
    n_blocks = (m * TOP_K + n_exp * (blk - 1)) // blk
    max_tiles = (m * TOP_K + n_exp * (sub - 1)) // sub
    pos, src, bexp, nsub, nused, tile_ids, n_tiles = _moe_plan(top_i, n_exp, blk, sub, n_blocks, max_tiles)
    xs = _dispatch(x, g_ffn, tile_ids, n_tiles, src, n_blocks * blk, sub)
    n_main = min(n_blocks, n_exp * (-(-(m * TOP_K) // (n_exp * blk))) + 2)
    ys = _experts(xs, w1, w3, w2, bexp, nsub, nused, blk, sub, tf, 0, n_main)
    if n_main < n_blocks:
        ys = lax.cond(
            nused[0] > n_main,
            lambda y: _experts(xs, w1, w3, w2, bexp, nsub, nused, blk, sub, tf, n_main, n_blocks - n_main, ys=y),
            lambda y: y, ys)
    return _combine(x, gates, pos, ys, g_final, tm_c, m_a)


def kernel(x_prompt, x_sample, state_conv_a, state_conv_c, state_pool_d, norm_mix_g, norm_ffn_g, norm_final_g, w_in_ab, conv_a, ln_b_g, ln_b_b, w_s, b_s, w_out_ab, ffn_w1, ffn_w3, ffn_w2, w_in_cd, conv_c, conv_c_b, ln_c_g, ln_c_b, pool_w, pool_scale, w_out_cd, router, exp_w1, exp_w3, exp_w2):
    n_seq, seq_len, d = x_prompt.shape
    nseq_s, dl, _ = x_sample.shape
    mp, ms = n_seq * seq_len, nseq_s * dl
    m = mp + ms
    assert norm_mix_g.shape[0] == 2, "two layers: conv/chunk-MLP + dense FFN, then conformer/pool + MoE"
    assert seq_len % CHUNK == 0 and dl <= CHUNK and mp % ms == 0

    big = m * TOP_K >= 4 * 2560
    tm_in = _pick(m, (2176, 1088, 544, 272))
    tm_mid = _pick(m, (1088, 544, 272))
    tm_row = _pick(ms, (512, 256, 128, 64))
    t_rows = _pick(seq_len, (256, 128))
    tn = 512
    tf_up = _pick(ffn_w1.shape[2], (512, 256))
    tn_down = 256
    tf_e = 256
    blk_e, sub_e = (2560, 128) if big else (256, 64)
    tm_c = _pick(ms, (256, 128, 64))

    xp = x_prompt.reshape(mp, d)
    xs_lm = jnp.transpose(x_sample, (1, 0, 2)).reshape(ms, d)

    h = _norm_in(xp, xs_lm, norm_mix_g[0], tm_row)
    proj = _matmul(h, w_in_ab[0], tm_in, tn)
    mixed, a_p, v_p = _mix_ab_prompt(proj, conv_a[0], ln_b_g[0], ln_b_b[0], w_s[0], b_s[0],
                                     n_seq, seq_len, t_rows)
    mixed, a_s, v_s = _mix_ab_sample(mixed, proj, jnp.transpose(state_conv_a[0], (1, 0, 2)),
                                     conv_a[0], ln_b_g[0], ln_b_b[0], w_s[0], b_s[0], mp, dl, nseq_s)
    x, h = _out_proj(mixed, w_out_ab[0].astype(BF16), xp, xs_lm, norm_ffn_g[0], tm_row)
    act = _ffn_up(h, ffn_w1, ffn_w3, tm_in, tf_up)
    x = _ffn_down(act, ffn_w2, x, tm_mid, tn_down)

    h = _norm_rows(x, norm_mix_g[1], tm_mid)
    proj = _matmul(h, w_in_cd[0], tm_in, tn)
    mixed, c_p, d_p = _mix_cd_prompt(proj, conv_c[0], conv_c_b[0], ln_c_g[0], ln_c_b[0], pool_w[0],
                                     pool_scale[0], n_seq, seq_len, t_rows)
    mixed, g_s = _mix_cd_sample(mixed, proj, jnp.transpose(state_conv_c[0], (1, 0, 2)),
                                jnp.transpose(state_pool_d[0], (1, 0, 2)), conv_c[0], conv_c_b[0],
                                ln_c_g[0], ln_c_b[0], pool_w[0], pool_scale[0], mp, dl, nseq_s)
    (x,) = _out_proj(mixed, w_out_cd[0].astype(BF16), x, x, None, tm_row)
    top_i, gates = _router(x, norm_ffn_g[1], router[0], tm_mid)
    y_p, y_s = _moe(x, top_i, gates, norm_ffn_g[1], norm_final_g, exp_w1[0], exp_w3[0], exp_w2[0],
                    blk_e, sub_e, tf_e, tm_c, mp)

    dc = conv_c.shape[2]
    y_prompt = y_p.reshape(n_seq, seq_len, d)
    y_sample = jnp.transpose(y_s.reshape(dl, nseq_s, d), (1, 0, 2))
    new_a_s = jnp.transpose(a_s, (1, 0, 2))
    new_v_s = jnp.transpose(v_s.reshape(dl, nseq_s, -1), (1, 0, 2))
    g_rows = jnp.transpose(g_s.reshape(dl, nseq_s, dc), (1, 0, 2))
    p_rows = jnp.transpose(proj[mp:, 2 * dc:].reshape(dl, nseq_s, -1), (1, 0, 2))
    n_c = state_conv_c.shape[2]
    n_d = state_pool_d.shape[2]
    new_c_s = jnp.concatenate([state_conv_c[0], g_rows], axis=1)[:, -n_c:]
    new_d_s = jnp.concatenate([state_pool_d[0], p_rows], axis=1)[:, -n_d:]
    return (y_prompt, y_sample,
            a_p[None], new_a_s[None],
            v_p[None], new_v_s[None],
            c_p[None], new_c_s[None],
            d_p[None], new_d_s[None])
```

```python
import functools

import jax
import jax.numpy as jnp
from jax import lax
from jax.experimental import pallas as pl
from jax.experimental.pallas import tpu as pltpu

F32 = jnp.float32
BF16 = jnp.bfloat16
I32 = jnp.int32

EPS = 1e-6
PAST_LEN = 16384
CHUNK = 128
HEAD_B = 128
POOL_WINDOWS = (2, 4, 8, 16)
TOP_K = 2

VMEM_LIMIT_BYTES = 60 * 1024 * 1024
SUBLANES = 8
LANES = 128


def _params(*sem):
    return pltpu.CompilerParams(dimension_semantics=sem, vmem_limit_bytes=VMEM_LIMIT_BYTES)


def _pick(n, candidates):
    for c in candidates:
        if n % c == 0:
            return c
    raise ValueError(f"no tile in {candidates} divides {n}")


def _rms(x, g):
    return x * lax.rsqrt(jnp.mean(x * x, axis=-1, keepdims=True) + EPS) * g


def _layernorm(x, g, b):
    mu = jnp.mean(x, axis=-1, keepdims=True)
    xc = x - mu
    return xc * lax.rsqrt(jnp.mean(xc * xc, axis=-1, keepdims=True) + EPS) * g + b


def _silu(x):
    return x * jax.nn.sigmoid(x)


def _dot(a, b):
    return jnp.dot(a, b, preferred_element_type=F32)


def _mix_ab_prompt_kernel(bg_ref, cg_ref, xin_ref, u_ref, v_ref, cgh_ref, xinh_ref,
                          cw_ref, lng_ref, lnb_ref, ws_ref, bsb_ref,
                          o_ref, astate_ref, vstate_ref, abuf_ref):
    t = pl.program_id(1)
    t_rows, da = bg_ref.shape
    n_heads = ws_ref.shape[0]
    halo = SUBLANES

    a = cg_ref[...] * xin_ref[...]
    prev = jnp.where(t == 0, 0.0, cgh_ref[...] * xinh_ref[...])
    abuf_ref[0:halo, :] = prev
    abuf_ref[halo:halo + t_rows, :] = a
    conv = (cw_ref[0:1, :] * abuf_ref[halo - 2:halo - 2 + t_rows, :]
            + cw_ref[1:2, :] * abuf_ref[halo - 1:halo - 1 + t_rows, :]
            + cw_ref[2:3, :] * a)
    o_ref[:, 0:da] = (bg_ref[...] * conv).astype(BF16)
    astate_ref[0] = abuf_ref[halo + t_rows - 2:halo + t_rows, :]

    row = lax.broadcasted_iota(I32, (CHUNK, CHUNK), 0)
    col = lax.broadcasted_iota(I32, (CHUNK, CHUNK), 1)
    lower = row >= col
    for h in range(n_heads):
        cols = slice(h * HEAD_B, (h + 1) * HEAD_B)
        vn = _layernorm(jax.nn.gelu(v_ref[:, cols]), lng_ref[:, cols], lnb_ref[:, cols])
        vstate_ref[0, :, cols] = vn[t_rows - CHUNK:t_rows]
        w = jnp.where(lower, ws_ref[h], 0.0).astype(BF16)
        for c in range(t_rows // CHUNK):
            rows = slice(c * CHUNK, (c + 1) * CHUNK)
            mix = _dot(w, vn[rows].astype(BF16)) + bsb_ref[h]
            o_ref[rows, da + h * HEAD_B:da + (h + 1) * HEAD_B] = (
                jax.nn.gelu(u_ref[rows, cols]) * mix).astype(BF16)


def _mix_ab_prompt(proj, conv_a, ln_g, ln_b, w_s, b_s, n_seq, seq_len, t_rows):
    m = proj.shape[0]
    da = conv_a.shape[1]
    db = ln_g.shape[0]
    n_heads = w_s.shape[0]
    tiles = seq_len // t_rows
    halo_blocks = t_rows // SUBLANES

    def col_spec(c):
        return pl.BlockSpec((t_rows, da), lambda b, t: (b * tiles + t, c))

    def halo_spec(c):
        return pl.BlockSpec(
            (SUBLANES, da),
            lambda b, t: (jnp.maximum((b * tiles + t) * halo_blocks - 1, 0), c))

    def full(shape):
        return pl.BlockSpec(shape, lambda b, t: (0,) * len(shape))

    bsb = jnp.broadcast_to(b_s[:, :CHUNK, None], (n_heads, CHUNK, HEAD_B))
    return pl.pallas_call(
        _mix_ab_prompt_kernel,
        grid=(n_seq, tiles),
        in_specs=[col_spec(0), col_spec(1), col_spec(2), col_spec(3), col_spec(4),
                  halo_spec(1), halo_spec(2),
                  full((conv_a.shape[0], da)), full((1, db)), full((1, db)),
                  full((n_heads, CHUNK, CHUNK)), full((n_heads, CHUNK, HEAD_B))],
        out_specs=[pl.BlockSpec((t_rows, da + db), lambda b, t: (b * tiles + t, 0)),
                   pl.BlockSpec((1, 2, da), lambda b, t: (b, 0, 0)),
                   pl.BlockSpec((1, CHUNK, db), lambda b, t: (b, 0, 0))],
        out_shape=[jax.ShapeDtypeStruct((m, da + db), BF16),
                   jax.ShapeDtypeStruct((n_seq, 2, da), F32),
                   jax.ShapeDtypeStruct((n_seq, CHUNK, db), F32)],
        scratch_shapes=[pltpu.VMEM((SUBLANES + t_rows, da), F32)],
        compiler_params=_params("parallel", "arbitrary"),
        name="mix_ab_prompt",
    )(proj, proj, proj, proj, proj, proj, proj, conv_a, ln_g.reshape(1, db), ln_b.reshape(1, db),
      w_s[:, :CHUNK, :CHUNK], bsb)


def _mix_ab_sample_kernel(mixed_hbm_ref, bg_ref, cg_ref, xin_ref, u_ref, v_ref, st_ref,
                          cw_ref, lng_ref, lnb_ref, wsl_ref, bsl_ref,
                          o_ref, astate_ref, vstate_ref, *, dl, nseq):
    del mixed_hbm_ref
    da = bg_ref.shape[1]
    n_heads = lng_ref.shape[1] // HEAD_B
    n_state = st_ref.shape[0]

    def slab(l):
        return slice(l * nseq, (l + 1) * nseq)

    a_pad = [st_ref[k] for k in range(n_state)]
    a_pad += [cg_ref[slab(l), :] * xin_ref[slab(l), :] for l in range(dl)]
    for l in range(dl):
        conv = (cw_ref[0:1, :] * a_pad[l] + cw_ref[1:2, :] * a_pad[l + 1]
                + cw_ref[2:3, :] * a_pad[l + 2])
        o_ref[slab(l), 0:da] = (bg_ref[slab(l), :] * conv).astype(BF16)
    for k in range(n_state):
        astate_ref[k] = a_pad[dl + k]

    for h in range(n_heads):
        cols = slice(h * HEAD_B, (h + 1) * HEAD_B)
        vn = _layernorm(jax.nn.gelu(v_ref[:, cols]), lng_ref[:, cols], lnb_ref[:, cols])
        vstate_ref[:, cols] = vn
        for i in range(dl):
            mix = bsl_ref[i:i + 1, cols]
            for j in range(i + 1):
                mix = mix + wsl_ref[i * dl + j:i * dl + j + 1, cols] * vn[slab(j)]
            o_ref[slab(i), da + h * HEAD_B:da + (h + 1) * HEAD_B] = (
                jax.nn.gelu(u_ref[slab(i), cols]) * mix).astype(BF16)


def _mix_ab_sample(mixed, proj, state_lm, conv_a, ln_g, ln_b, w_s, b_s, row0, dl, nseq):
    ms = dl * nseq
    da = conv_a.shape[1]
    db = ln_g.shape[0]
    n_heads = w_s.shape[0]
    rb = row0 // ms

    def col_spec(c):
        return pl.BlockSpec((ms, da), lambda i: (rb, c))

    def full(shape):
        return pl.BlockSpec(shape, lambda i: (0,) * len(shape))

    wsl = jnp.repeat(jnp.transpose(w_s[:, :dl, :dl], (1, 2, 0)).reshape(dl * dl, n_heads), HEAD_B, axis=1)
    bsl = jnp.repeat(jnp.transpose(b_s[:, :dl], (1, 0)), HEAD_B, axis=1)
    return pl.pallas_call(
        functools.partial(_mix_ab_sample_kernel, dl=dl, nseq=nseq),
        grid=(1,),
        in_specs=[pl.BlockSpec(memory_space=pl.ANY),
                  col_spec(0), col_spec(1), col_spec(2), col_spec(3), col_spec(4),
                  full(state_lm.shape), full((conv_a.shape[0], da)), full((1, db)), full((1, db)),
                  full((dl * dl, db)), full((dl, db))],
        out_specs=[pl.BlockSpec((ms, da + db), lambda i: (rb, 0)),
                   full(state_lm.shape),
                   full((ms, db))],
        out_shape=[jax.ShapeDtypeStruct(mixed.shape, BF16),
                   jax.ShapeDtypeStruct(state_lm.shape, F32),
                   jax.ShapeDtypeStruct((ms, db), F32)],
        input_output_aliases={0: 0},
        compiler_params=_params("arbitrary"),
        name="mix_ab_sample",
    )(mixed, proj, proj, proj, proj, proj, state_lm, conv_a, ln_g.reshape(1, db), ln_b.reshape(1, db),
      wsl, bsl)


def _mix_cd_prompt_kernel(ga_ref, gb_ref, p_ref, gah_ref, gbh_ref, ph_ref,
                          cw_ref, cb_ref, lng_ref, lnb_ref, pw_ref, ps_ref,
                          o_ref, cstate_ref, dstate_ref, gbuf_ref, pbuf_ref, cbuf_ref, shift_ref):
    t = pl.program_id(1)
    t_rows, dc = ga_ref.shape
    k_c = cw_ref.shape[0]
    c_halo = gah_ref.shape[0]
    d_halo = ph_ref.shape[0]
    gd = pw_ref.shape[1]

    g = ga_ref[...] * jax.nn.sigmoid(gb_ref[...])
    gbuf_ref[0:c_halo, :] = jnp.where(t == 0, 0.0, gah_ref[...] * jax.nn.sigmoid(gbh_ref[...]))
    gbuf_ref[c_halo:c_halo + t_rows, :] = g
    n_shift = shift_ref.shape[1]
    for r in range(1, SUBLANES):
        shift_ref[r - 1] = gbuf_ref[r:r + n_shift, :]
    first = c_halo - (k_c - 1)
    for cblk in range(dc // LANES):
        cols = slice(cblk * LANES, (cblk + 1) * LANES)
        acc = cb_ref[:, cols]
        for k in range(k_c):
            q, r = divmod(first + k, SUBLANES)
            if r == 0:
                win = gbuf_ref[first + k:first + k + t_rows, cols]
            else:
                win = shift_ref[r - 1, q * SUBLANES:q * SUBLANES + t_rows, cols]
            acc = acc + cw_ref[k:k + 1, cols] * win
        cbuf_ref[:, cols] = acc
    o_ref[:, 0:dc] = _silu(_layernorm(cbuf_ref[...], lng_ref[...], lnb_ref[...])).astype(BF16)
    cstate_ref[0] = gbuf_ref[c_halo + t_rows - (k_c - 1):c_halo + t_rows, :]

    pbuf_ref[0:d_halo, :] = jnp.where(t == 0, 0.0, ph_ref[...])
    pbuf_ref[d_halo:d_halo + t_rows, :] = p_ref[...]
    pos = t * t_rows + lax.broadcasted_iota(I32, (t_rows, 1), 0)
    for gi, w in enumerate(POOL_WINDOWS):
        cols = slice(gi * gd, (gi + 1) * gd)
        cur = p_ref[:, cols]
        s = cur
        for k in range(1, w):
            s = s + pbuf_ref[d_halo - k:d_halo - k + t_rows, cols]
        cnt = jnp.minimum(w, pos + 1).astype(F32)
        diff = (s / cnt - cur).astype(BF16)
        d = _dot(diff, pw_ref[gi].astype(BF16)) * ps_ref[:, cols]
        o_ref[:, dc + gi * gd:dc + (gi + 1) * gd] = d.astype(BF16)
    n_keep = dstate_ref.shape[1]
    dstate_ref[0] = pbuf_ref[d_halo + t_rows - n_keep:d_halo + t_rows, :]


def _mix_cd_prompt(proj, conv_c, conv_c_b, ln_g, ln_b, pool_w, pool_scale, n_seq, seq_len, t_rows):
    m = proj.shape[0]
    k_c, dc = conv_c.shape
    dd = pool_scale.shape[0]
    n_pool, gd = pool_w.shape[0], pool_w.shape[1]
    tiles = seq_len // t_rows
    c_halo = -(-(k_c - 1) // SUBLANES) * SUBLANES
    d_halo = -(-(max(POOL_WINDOWS) - 1) // SUBLANES) * SUBLANES
    assert t_rows % c_halo == 0 and t_rows % d_halo == 0 and dc == dd

    def col_spec(c):
        return pl.BlockSpec((t_rows, dc), lambda b, t: (b * tiles + t, c))

    def halo_spec(c, rows):
        per = t_rows // rows
        return pl.BlockSpec((rows, dc),
                            lambda b, t: (jnp.maximum((b * tiles + t) * per - 1, 0), c))

    def full(shape):
        return pl.BlockSpec(shape, lambda b, t: (0,) * len(shape))

    return pl.pallas_call(
        _mix_cd_prompt_kernel,
        grid=(n_seq, tiles),
        in_specs=[col_spec(0), col_spec(1), col_spec(2),
                  halo_spec(0, c_halo), halo_spec(1, c_halo), halo_spec(2, d_halo),
                  full((k_c, dc)), full((1, dc)), full((1, dc)), full((1, dc)),
                  full((n_pool, gd, gd)), full((1, dd))],
        out_specs=[pl.BlockSpec((t_rows, dc + dd), lambda b, t: (b * tiles + t, 0)),
                   pl.BlockSpec((1, k_c - 1, dc), lambda b, t: (b, 0, 0)),
                   pl.BlockSpec((1, max(POOL_WINDOWS) - 1, dd), lambda b, t: (b, 0, 0))],
        out_shape=[jax.ShapeDtypeStruct((m, dc + dd), BF16),
                   jax.ShapeDtypeStruct((n_seq, k_c - 1, dc), F32),
                   jax.ShapeDtypeStruct((n_seq, max(POOL_WINDOWS) - 1, dd), F32)],
        scratch_shapes=[pltpu.VMEM((c_halo + t_rows, dc), F32),
                        pltpu.VMEM((d_halo + t_rows, dd), F32),
                        pltpu.VMEM((t_rows, dc), F32),
                        pltpu.VMEM((SUBLANES - 1, c_halo + t_rows - SUBLANES, dc), F32)],
        compiler_params=_params("parallel", "arbitrary"),
        name="mix_cd_prompt",
    )(proj, proj, proj, proj, proj, proj, conv_c, conv_c_b.reshape(1, dc), ln_g.reshape(1, dc),
      ln_b.reshape(1, dc), pool_w, pool_scale.reshape(1, dd))


def _mix_cd_sample_kernel(mixed_hbm_ref, ga_ref, gb_ref, p_ref, stc_ref, std_ref,
                          cw_ref, cb_ref, lng_ref, lnb_ref, pw_ref, ps_ref,
                          o_ref, g_ref, cbuf_ref, dbuf_ref, *, dl, nseq, start_pos):
    del mixed_hbm_ref
    gi = pl.program_id(0)
    n_groups = pl.num_programs(0)
    gd = ga_ref.shape[1]
    k_c = cw_ref.shape[0]
    n_c = stc_ref.shape[0]
    n_d = std_ref.shape[0]

    def slab(l):
        return slice(l * nseq, (l + 1) * nseq)

    g = ga_ref[...] * jax.nn.sigmoid(gb_ref[...])
    g_ref[...] = g
    c_pad = [stc_ref[k] for k in range(n_c)] + [g[slab(l)] for l in range(dl)]
    d_pad = [std_ref[k] for k in range(n_d)] + [p_ref[slab(l), :] for l in range(dl)]
    for l in range(dl):
        acc = cb_ref[...] + cw_ref[0:1, :] * c_pad[l]
        for k in range(1, k_c):
            acc = acc + cw_ref[k:k + 1, :] * c_pad[l + k]
        cbuf_ref[gi, slab(l), :] = acc

        cur = d_pad[n_d + l]
        s = cur
        pooled = jnp.zeros_like(cur)
        for k in range(1, max(POOL_WINDOWS) + 1):
            if k in POOL_WINDOWS:
                cnt = float(min(k, start_pos + l + 1))
                pooled = jnp.where(gi == POOL_WINDOWS.index(k), s / cnt, pooled)
            if k < max(POOL_WINDOWS):
                s = s + d_pad[n_d + l - k]
        diff = (pooled - cur).astype(BF16)
        dbuf_ref[gi, slab(l), :] = (_dot(diff, pw_ref[0].astype(BF16)) * ps_ref[...]).astype(BF16)

    @pl.when(gi == n_groups - 1)
    def _():
        n_blk = cbuf_ref.shape[0]
        dc = n_blk * gd
        total = cbuf_ref[0].sum(axis=-1, keepdims=True)
        for b in range(1, n_blk):
            total = total + cbuf_ref[b].sum(axis=-1, keepdims=True)
        mu = total / dc
        var = None
        for b in range(n_blk):
            xc = cbuf_ref[b] - mu
            sq = (xc * xc).sum(axis=-1, keepdims=True)
            var = sq if var is None else var + sq
        rstd = lax.rsqrt(var / dc + EPS)
        for b in range(n_blk):
            cols = slice(b * gd, (b + 1) * gd)
            y = (cbuf_ref[b] - mu) * rstd * lng_ref[:, cols] + lnb_ref[:, cols]
            o_ref[:, cols] = _silu(y).astype(BF16)
            o_ref[:, dc + b * gd:dc + (b + 1) * gd] = dbuf_ref[b]


def _mix_cd_sample(mixed, proj, stc_lm, std_lm, conv_c, conv_c_b, ln_g, ln_b, pool_w, pool_scale,
                   row0, dl, nseq):
    ms = dl * nseq
    k_c, dc = conv_c.shape
    dd = pool_scale.shape[0]
    n_pool, gd = pool_w.shape[0], pool_w.shape[1]
    assert dc == dd == n_pool * gd and n_pool == len(POOL_WINDOWS)
    rb = row0 // ms

    return pl.pallas_call(
        functools.partial(_mix_cd_sample_kernel, dl=dl, nseq=nseq, start_pos=PAST_LEN),
        grid=(n_pool,),
        in_specs=[pl.BlockSpec(memory_space=pl.ANY),
                  pl.BlockSpec((ms, gd), lambda i: (rb, i)),
                  pl.BlockSpec((ms, gd), lambda i: (rb, n_pool + i)),
                  pl.BlockSpec((ms, gd), lambda i: (rb, 2 * n_pool + i)),
                  pl.BlockSpec((stc_lm.shape[0], nseq, gd), lambda i: (0, 0, i)),
                  pl.BlockSpec((std_lm.shape[0], nseq, gd), lambda i: (0, 0, i)),
                  pl.BlockSpec((k_c, gd), lambda i: (0, i)),
                  pl.BlockSpec((1, gd), lambda i: (0, i)),
                  pl.BlockSpec((1, dc), lambda i: (0, 0)),
                  pl.BlockSpec((1, dc), lambda i: (0, 0)),
                  pl.BlockSpec((1, gd, gd), lambda i: (i, 0, 0)),
                  pl.BlockSpec((1, gd), lambda i: (0, i))],
        out_specs=[pl.BlockSpec((ms, dc + dd), lambda i: (rb, 0)),
                   pl.BlockSpec((ms, gd), lambda i: (0, i))],
        out_shape=[jax.ShapeDtypeStruct(mixed.shape, BF16),
                   jax.ShapeDtypeStruct((ms, dc), F32)],
        scratch_shapes=[pltpu.VMEM((n_pool, ms, gd), F32),
                        pltpu.VMEM((n_pool, ms, gd), BF16)],
        input_output_aliases={0: 0},
        compiler_params=_params("arbitrary"),
        name="mix_cd_sample",
    )(mixed, proj, proj, proj, stc_lm, std_lm, conv_c, conv_c_b.reshape(1, dc), ln_g.reshape(1, dc),
      ln_b.reshape(1, dc), pool_w, pool_scale.reshape(1, dd))


def _norm_in_kernel(xp_ref, xs_ref, g_ref, o_ref, *, n_a):
    i = pl.program_id(0)

    @pl.when(i < n_a)
    def _():
        o_ref[...] = _rms(xp_ref[...], g_ref[...]).astype(BF16)

    @pl.when(i >= n_a)
    def _():
        o_ref[...] = _rms(xs_ref[...], g_ref[...]).astype(BF16)


def _norm_in(xp, xs, g, tm):
    mp, d = xp.shape
    ms = xs.shape[0]
    n_a, n_b = mp // tm, ms // tm
    return pl.pallas_call(
        functools.partial(_norm_in_kernel, n_a=n_a),
        grid=(n_a + n_b,),
        in_specs=[pl.BlockSpec((tm, d), lambda i: (jnp.minimum(i, n_a - 1), 0)),
                  pl.BlockSpec((tm, d), lambda i: (jnp.maximum(i - n_a, 0), 0)),
                  pl.BlockSpec((1, d), lambda i: (0, 0))],
        out_specs=pl.BlockSpec((tm, d), lambda i: (i, 0)),
        out_shape=jax.ShapeDtypeStruct((mp + ms, d), BF16),
        compiler_params=_params("arbitrary"),
        name="norm_in",
    )(xp, xs, g.reshape(1, d))


def _matmul_kernel(a_ref, w_ref, o_ref):
    o_ref[...] = _dot(a_ref[...], w_ref[...].astype(BF16))


def _matmul(a, w, tm, tn):
    m, k = a.shape
    n = w.shape[1]
    return pl.pallas_call(
        _matmul_kernel,
        grid=(m // tm, n // tn),
        in_specs=[pl.BlockSpec((tm, k), lambda i, j: (i, 0)),
                  pl.BlockSpec((k, tn), lambda i, j: (0, j))],
        out_specs=pl.BlockSpec((tm, tn), lambda i, j: (i, j)),
        out_shape=jax.ShapeDtypeStruct((m, n), F32),
        compiler_params=_params("parallel", "arbitrary"),
        name="in_proj",
    )(a, w)


def _ffn_up_kernel(h_ref, w1_ref, w3_ref, o_ref):
    h = h_ref[...]
    a = _dot(h, w1_ref[0].astype(BF16))
    b = _dot(h, w3_ref[0].astype(BF16))
    o_ref[...] = (_silu(a) * b).astype(BF16)


def _ffn_up(h, w1, w3, tm, tf):
    m, d = h.shape
    f = w1.shape[2]
    return pl.pallas_call(
        _ffn_up_kernel,
        grid=(m // tm, f // tf),
        in_specs=[pl.BlockSpec((tm, d), lambda i, j: (i, 0)),
                  pl.BlockSpec((1, d, tf), lambda i, j: (0, 0, j)),
                  pl.BlockSpec((1, d, tf), lambda i, j: (0, 0, j))],
        out_specs=pl.BlockSpec((tm, tf), lambda i, j: (i, j)),
        out_shape=jax.ShapeDtypeStruct((m, f), BF16),
        compiler_params=_params("parallel", "arbitrary"),
        name="ffn_up",
    )(h, w1, w3)


def _ffn_down_kernel(a_ref, w_ref, r_ref, o_ref):
    o_ref[...] = r_ref[...] + _dot(a_ref[...], w_ref[0].astype(BF16))


def _ffn_down(act, w2, res, tm, tn):
    m, f = act.shape
    d = w2.shape[2]
    return pl.pallas_call(
        _ffn_down_kernel,
        grid=(m // tm, d // tn),
        in_specs=[pl.BlockSpec((tm, f), lambda i, j: (i, 0)),
                  pl.BlockSpec((1, f, tn), lambda i, j: (0, 0, j)),
                  pl.BlockSpec((tm, tn), lambda i, j: (i, j))],
        out_specs=pl.BlockSpec((tm, tn), lambda i, j: (i, j)),
        out_shape=jax.ShapeDtypeStruct((m, d), F32),
        compiler_params=_params("parallel", "arbitrary"),
        name="ffn_down",
    )(act, w2, res)


def _norm_rows_kernel(x_ref, g_ref, o_ref):
    o_ref[...] = _rms(x_ref[...], g_ref[...]).astype(BF16)


def _norm_rows(x, g, tm):
    m, d = x.shape
    return pl.pallas_call(
        _norm_rows_kernel,
        grid=(m // tm,),
        in_specs=[pl.BlockSpec((tm, d), lambda i: (i, 0)),
                  pl.BlockSpec((1, d), lambda i: (0, 0))],
        out_specs=pl.BlockSpec((tm, d), lambda i: (i, 0)),
        out_shape=jax.ShapeDtypeStruct((m, d), BF16),
        compiler_params=_params("parallel"),
        name="norm_rows",
    )(x, g.reshape(1, d))


def _out_proj_kernel(a_ref, w_ref, ra_ref, rb_ref, *rest, n_a, with_h):
    i = pl.program_id(0)
    if with_h:
        g_ref, x_ref, h_ref = rest
    else:
        (x_ref,) = rest
    tm = a_ref.shape[0]
    n_parts = 2 if tm % (2 * 2 * SUBLANES) == 0 else 1
    rows_per = tm // n_parts
    for part in range(n_parts):
        rows = slice(part * rows_per, (part + 1) * rows_per)
        res = jnp.where(i < n_a, ra_ref[rows, :], rb_ref[rows, :])
        x = res + _dot(a_ref[rows, :], w_ref[...])
        x_ref[rows, :] = x
        if with_h:
            h_ref[rows, :] = _rms(x, g_ref[...]).astype(BF16)


def _out_proj(a, w_bf16, res_a, res_b, g, tm):
    m, k = a.shape
    d = w_bf16.shape[1]
    n_a = res_a.shape[0] // tm
    n_b = m // tm - n_a
    assert n_a * tm == res_a.shape[0] and (n_b == 0 or n_b * tm == res_b.shape[0])
    with_h = g is not None
    in_specs = [pl.BlockSpec((tm, k), lambda i: (i, 0)),
                pl.BlockSpec((k, d), lambda i: (0, 0), pipeline_mode=pl.Buffered(1)),
                pl.BlockSpec((tm, d), lambda i: (jnp.minimum(i, n_a - 1), 0)),
                pl.BlockSpec((tm, d), lambda i: (jnp.maximum(i - n_a, 0), 0))]
    args = [a, w_bf16, res_a, res_b]
    row_spec = pl.BlockSpec((tm, d), lambda i: (i, 0))
    out_specs = [row_spec]
    out_shape = [jax.ShapeDtypeStruct((m, d), F32)]
    if with_h:
        in_specs.append(pl.BlockSpec((1, d), lambda i: (0, 0)))
        args.append(g.reshape(1, d))
        out_specs.append(row_spec)
        out_shape.append(jax.ShapeDtypeStruct((m, d), BF16))
    return pl.pallas_call(
        functools.partial(_out_proj_kernel, n_a=n_a, with_h=with_h),
        grid=(m // tm,),
        in_specs=in_specs, out_specs=out_specs, out_shape=out_shape,
        compiler_params=_params("arbitrary"),
        name="out_proj_h" if with_h else "out_proj",
    )(*args)


def _router_kernel(x_ref, g_ref, r_ref, idx_ref, gate_ref):
    h = _rms(x_ref[...], g_ref[...])
    h_hi = h.astype(BF16)
    h_lo = (h - h_hi.astype(F32)).astype(BF16)
    r = r_ref[...]
    r_hi = r.astype(BF16)
    r_lo = (r - r_hi.astype(F32)).astype(BF16)
    logits = _dot(h_hi, r_hi) + (_dot(h_hi, r_lo) + _dot(h_lo, r_hi))
    n_exp = float(logits.shape[1])
    lane = lax.broadcasted_iota(I32, logits.shape, 1).astype(F32)
    m1 = jnp.max(logits, axis=-1, keepdims=True)
    i1 = jnp.min(jnp.where(logits == m1, lane, n_exp), axis=-1, keepdims=True)
    rest = jnp.where(lane == i1, -jnp.inf, logits)
    m2 = jnp.max(rest, axis=-1, keepdims=True)
    i2 = jnp.min(jnp.where(rest == m2, lane, n_exp), axis=-1, keepdims=True)
    e = jnp.exp(m2 - m1)
    two = lax.broadcasted_iota(I32, idx_ref.shape, 1)
    idx_ref[...] = jnp.where(two == 0, i1, i2).astype(I32)
    gate_ref[...] = jnp.where(two == 0, 1.0, e) / (1.0 + e)


def _router(x, g, router, tm):
    m, d = x.shape
    n_exp = router.shape[1]
    return pl.pallas_call(
        _router_kernel,
        grid=(m // tm,),
        in_specs=[pl.BlockSpec((tm, d), lambda i: (i, 0)),
                  pl.BlockSpec((1, d), lambda i: (0, 0)),
                  pl.BlockSpec((d, n_exp), lambda i: (0, 0))],
        out_specs=[pl.BlockSpec((tm, TOP_K), lambda i: (i, 0)),
                   pl.BlockSpec((tm, TOP_K), lambda i: (i, 0))],
        out_shape=[jax.ShapeDtypeStruct((m, TOP_K), I32),
                   jax.ShapeDtypeStruct((m, TOP_K), F32)],
        compiler_params=_params("parallel"),
        name="router",
    )(x, g.reshape(1, d), router)


def _experts_kernel(bexp_ref, nsub_ref, nused_ref, *refs, sub, s0):
    if s0:
        refs = refs[1:]
    xs_ref, w1_ref, w3_ref, w2_ref, o_ref, w1b_ref, w3b_ref, w2b_ref = refs
    s = pl.program_id(0) + s0
    j = pl.program_id(1)

    def rows_of(i):
        return pl.ds(pl.multiple_of(i * sub, sub), sub)

    def group(first, count, w1v, w3v, w2v):
        acts = []
        for t in range(count):
            x = xs_ref[rows_of(first + t), :]
            acts.append((_silu(_dot(x, w1v)) * _dot(x, w3v)).astype(BF16))
        for t in range(count):
            o_ref[rows_of(first + t), :] += _dot(acts[t], w2v)

    def from_scratch(first, count):
        group(first, count, w1b_ref[...], w3b_ref[...], w2b_ref[...])

    @pl.when(s < nused_ref[0])
    def _():
        n = nsub_ref[s]

        @pl.when(j == 0)
        def _():
            def init(i, carry):
                o_ref[rows_of(i), :] = jnp.zeros((sub, o_ref.shape[1]), F32)
                return carry
            lax.fori_loop(0, n, init, 0)

        @pl.when(n >= 3)
        def _():
            w1v = w1_ref[0].astype(BF16)
            w3v = w3_ref[0].astype(BF16)
            w2v = w2_ref[0].astype(BF16)
            w1b_ref[...] = w1v
            w3b_ref[...] = w3v
            w2b_ref[...] = w2v
            group(0, 3, w1v, w3v, w2v)

        @pl.when(n < 3)
        def _():
            w1b_ref[...] = w1_ref[0].astype(BF16)
            w3b_ref[...] = w3_ref[0].astype(BF16)
            w2b_ref[...] = w2_ref[0].astype(BF16)

        def triple(p, carry):
            from_scratch(3 * p, 3)
            return carry
        lax.fori_loop(1, n // 3, triple, 0)

        done = (n // 3) * 3
        rem = n - done

        @pl.when(rem == 1)
        def _():
            from_scratch(done, 1)

        @pl.when(rem == 2)
        def _():
            from_scratch(done, 2)


def _experts(xs, w1, w3, w2, bexp, nsub, nused, blk, sub, tf, s0, n_s, ys=None):
    n_rows, d = xs.shape
    f = w1.shape[2]
    n_f = f // tf
    assert (ys is None) == (s0 == 0)

    def blk_idx(s, nu):
        return jnp.minimum(s + s0, jnp.maximum(nu[0] - 1, s0))

    def exp_idx(s, be, nu):
        return be[blk_idx(s, nu)]

    def jj(s, j, nu):
        return jnp.where(s + s0 < nu[0], j, n_f - 1)

    row_spec = pl.BlockSpec((blk, d), lambda s, j, be, ns, nu: (blk_idx(s, nu), 0),
                            pipeline_mode=pl.Buffered(1))
    in_specs = [row_spec,
                pl.BlockSpec((1, d, tf), lambda s, j, be, ns, nu: (exp_idx(s, be, nu), 0, jj(s, j, nu))),
                pl.BlockSpec((1, d, tf), lambda s, j, be, ns, nu: (exp_idx(s, be, nu), 0, jj(s, j, nu))),
                pl.BlockSpec((1, tf, d), lambda s, j, be, ns, nu: (exp_idx(s, be, nu), jj(s, j, nu), 0))]
    args = [xs, w1, w3, w2]
    aliases = {}
    if s0:
        in_specs = [pl.BlockSpec(memory_space=pl.ANY)] + in_specs
        args = [ys] + args
        aliases = {3: 0}
    return pl.pallas_call(
        functools.partial(_experts_kernel, sub=sub, s0=s0),
        grid_spec=pltpu.PrefetchScalarGridSpec(
            num_scalar_prefetch=3,
            grid=(n_s, n_f),
            in_specs=in_specs,
            out_specs=row_spec,
            scratch_shapes=[pltpu.VMEM((d, tf), BF16), pltpu.VMEM((d, tf), BF16),
                            pltpu.VMEM((tf, d), BF16)]),
        out_shape=jax.ShapeDtypeStruct((n_rows, d), F32),
        input_output_aliases=aliases,
        compiler_params=_params("arbitrary", "arbitrary"),
        name="moe_experts" if s0 == 0 else "moe_experts_overflow",
    )(bexp, nsub, nused, *args)


def _dispatch_kernel(tile_ids_ref, n_tiles_ref, src_ref, x_hbm_ref, g_ref, o_ref, buf_ref, sem_ref, *, sub):
    k = pl.program_id(0)
    n_valid = n_tiles_ref[0]

    def start(step, slot):
        base = tile_ids_ref[step] * sub

        def body(r, carry):
            tok = src_ref[base + r]
            pltpu.make_async_copy(x_hbm_ref.at[pl.ds(tok, 1)], buf_ref.at[slot, pl.ds(r, 1)],
                                  sem_ref.at[slot]).start()
            return carry
        lax.fori_loop(0, sub, body, 0, unroll=8)

    @pl.when(jnp.logical_and(k == 0, n_valid > 0))
    def _():
        start(0, 0)

    @pl.when(k + 1 < n_valid)
    def _():
        start(k + 1, (k + 1) % 2)

    @pl.when(k < n_valid)
    def _():
        slot = k % 2
        pltpu.make_async_copy(x_hbm_ref.at[pl.ds(0, sub)], buf_ref.at[slot], sem_ref.at[slot]).wait()
        o_ref[...] = _rms(buf_ref[slot], g_ref[...]).astype(BF16)


def _dispatch(x, g, tile_ids, n_tiles, src, n_slots, sub):
    m, d = x.shape
    max_tiles = tile_ids.shape[0]
    return pl.pallas_call(
        functools.partial(_dispatch_kernel, sub=sub),
        grid_spec=pltpu.PrefetchScalarGridSpec(
            num_scalar_prefetch=3,
            grid=(max_tiles,),
            in_specs=[pl.BlockSpec(memory_space=pl.ANY),
                      pl.BlockSpec((1, d), lambda k, ids, nt, src: (0, 0))],
            out_specs=pl.BlockSpec((sub, d), lambda k, ids, nt, src: (ids[k], 0)),
            scratch_shapes=[pltpu.VMEM((2, sub, d), F32), pltpu.SemaphoreType.DMA((2,))]),
        out_shape=jax.ShapeDtypeStruct((n_slots, d), BF16),
        compiler_params=_params("arbitrary"),
        name="moe_dispatch",
    )(tile_ids, n_tiles, src, x, g.reshape(1, d))


def _combine_kernel(pos_ref, x_ref, gate_ref, g_ref, ys_hbm_ref, ya_ref, yb_ref, buf_ref, sem_ref, *, tm, n_a):
    i = pl.program_id(0)
    n = pl.num_programs(0)

    def start(step, slot):
        base = step * (tm * TOP_K)

        def body(r, carry):
            for kk in range(TOP_K):
                p = pos_ref[base + r * TOP_K + kk]
                pltpu.make_async_copy(ys_hbm_ref.at[pl.ds(p, 1)], buf_ref.at[slot, kk, pl.ds(r, 1)],
                                      sem_ref.at[slot]).start()
            return carry
        lax.fori_loop(0, tm, body, 0, unroll=4)

    @pl.when(i == 0)
    def _():
        start(0, 0)

    @pl.when(i + 1 < n)
    def _():
        start(i + 1, (i + 1) % 2)

    slot = i % 2
    for kk in range(TOP_K):
        pltpu.make_async_copy(ys_hbm_ref.at[pl.ds(0, tm)], buf_ref.at[slot, kk], sem_ref.at[slot]).wait()
    y = x_ref[...]
    for kk in range(TOP_K):
        y = y + gate_ref[:, kk:kk + 1] * buf_ref[slot, kk]
    y = _rms(y, g_ref[...])

    @pl.when(i < n_a)
    def _():
        ya_ref[...] = y

    @pl.when(i >= n_a)
    def _():
        yb_ref[...] = y


def _combine(x, gates, pos, ys, g, tm, m_a):
    m, d = x.shape
    n_a = m_a // tm
    assert n_a * tm == m_a and m % tm == 0
    return pl.pallas_call(
        functools.partial(_combine_kernel, tm=tm, n_a=n_a),
        grid_spec=pltpu.PrefetchScalarGridSpec(
            num_scalar_prefetch=1,
            grid=(m // tm,),
            in_specs=[pl.BlockSpec((tm, d), lambda i, pos: (i, 0)),
                      pl.BlockSpec((tm, TOP_K), lambda i, pos: (i, 0)),
                      pl.BlockSpec((1, d), lambda i, pos: (0, 0)),
                      pl.BlockSpec(memory_space=pl.ANY)],
            out_specs=[pl.BlockSpec((tm, d), lambda i, pos: (jnp.minimum(i, n_a - 1), 0)),
                       pl.BlockSpec((tm, d), lambda i, pos: (jnp.maximum(i - n_a, 0), 0))],
            scratch_shapes=[pltpu.VMEM((2, TOP_K, tm, d), F32), pltpu.SemaphoreType.DMA((2,))]),
        out_shape=[jax.ShapeDtypeStruct((m_a, d), F32), jax.ShapeDtypeStruct((m - m_a, d), F32)],
        compiler_params=_params("arbitrary"),
        name="moe_combine",
    )(pos, x, gates, g.reshape(1, d), ys)


def _moe_plan(top_i, n_exp, blk, sub, n_blocks, max_tiles):
    m = top_i.shape[0]
    e_flat = top_i.reshape(-1)
    onehot = (e_flat[:, None] == jnp.arange(n_exp, dtype=I32)[None, :]).astype(I32)
    csum = jnp.cumsum(onehot, axis=0)
    rank = jnp.sum((csum - 1) * onehot, axis=1)
    counts = csum[-1]
    blocks_per = (counts + blk - 1) // blk
    block_end = jnp.cumsum(blocks_per)
    block_start = block_end - blocks_per
    n_used = block_end[-1]
    pos = block_start[e_flat] * blk + rank

    s = jnp.arange(n_blocks, dtype=I32)
    bexp = jnp.minimum(jnp.sum((s[:, None] >= block_end[None, :]).astype(I32), axis=1), n_exp - 1)
    valid = jnp.clip(counts[bexp] - (s - block_start[bexp]) * blk, 0, blk)
    valid = jnp.where(s < n_used, valid, 0)

    by_expert = jnp.argsort(e_flat, stable=True).astype(I32) // TOP_K
    by_expert = jnp.concatenate([by_expert, jnp.zeros((m,), I32)])
    first_pair = jnp.cumsum(counts) - counts
    src = jnp.zeros((n_blocks * blk + m,), I32)
    for e in range(n_exp):
        run = lax.dynamic_slice(by_expert, (first_pair[e],), (m,))
        src = lax.dynamic_update_slice(src, run, (block_start[e] * blk,))
    src = src[:n_blocks * blk]
    nsub = (valid + sub - 1) // sub
    last_exp = bexp[jnp.maximum(n_used - 1, 0)]
    bexp = jnp.where(s < n_used, bexp, last_exp)

    per_blk = blk // sub
    t = jnp.arange(n_blocks * per_blk, dtype=I32)
    tile_valid = (t % per_blk) < nsub[t // per_blk]
    n_tiles = jnp.sum(tile_valid.astype(I32))
    order = jnp.argsort(jnp.logical_not(tile_valid), stable=True).astype(I32)[:max_tiles]
    last_tile = order[jnp.maximum(n_tiles - 1, 0)]
    tile_ids = jnp.where(jnp.arange(max_tiles, dtype=I32) < n_tiles, order, last_tile)
    return pos, src, bexp, nsub, n_used.reshape(1), tile_ids, n_tiles.reshape(1)


def _moe(x, top_i, gates, g_ffn, g_final, w1, w3, w2, blk, sub, tf, tm_c, m_a):
    m, d = x.shape
    n_exp = w1.shape[0]
    n_blocks = (m * TOP_K + n_exp * (blk - 1)) // blk
    max_tiles = (m * TOP_K + n_exp * (sub - 1)) // sub
    pos, src, bexp, nsub, nused, tile_ids, n_tiles = _moe_plan(top_i, n_exp, blk, sub, n_blocks, max_tiles)
    xs = _dispatch(x, g_ffn, tile_ids, n_tiles, src, n_blocks * blk, sub)
    n_main = min(n_blocks, n_exp * (-(-(m * TOP_K) // (n_exp * blk))) + 2)
    ys = _experts(xs, w1, w3, w2, bexp, nsub, nused, blk, sub, tf, 0, n_main)
    if n_main < n_blocks:
        ys = lax.cond(
            nused[0] > n_main,
            lambda y: _experts(xs, w1, w3, w2, bexp, nsub, nused, blk, sub, tf, n_main, n_blocks - n_main, ys=y),
            lambda y: y, ys)
    return _combine(x, gates, pos, ys, g_final, tm_c, m_a)


def kernel(x_prompt, x_sample, state_conv_a, state_conv_c, state_pool_d, norm_mix_g, norm_ffn_g, norm_final_g, w_in_ab, conv_a, ln_b_g, ln_b_b, w_s, b_s, w_out_ab, ffn_w1, ffn_w3, ffn_w2, w_in_cd, conv_c, conv_c_b, ln_c_g, ln_c_b, pool_w, pool_scale, w_out_cd, router, exp_w1, exp_w3, exp_w2):
    n_seq, seq_len, d = x_prompt.shape
    nseq_s, dl, _ = x_sample.shape
    mp, ms = n_seq * seq_len, nseq_s * dl
    m = mp + ms
    assert norm_mix_g.shape[0] == 2, "two layers: conv/chunk-MLP + dense FFN, then conformer/pool + MoE"
    assert seq_len % CHUNK == 0 and dl <= CHUNK and mp % ms == 0

    big = m * TOP_K >= 4 * 2560
    tm_in = _pick(m, (2176, 1088, 544, 272))
    tm_mid = _pick(m, (1088, 544, 272))
    tm_row = _pick(ms, (512, 256, 128, 64))
    t_rows = _pick(seq_len, (256, 128))
    tn = 512
    tf_up = _pick(ffn_w1.shape[2], (512, 256))
    tn_down = 256
    tf_e = 256
    blk_e, sub_e = (2560, 256) if big else (512, 128)
    tm_c = _pick(ms, (256, 128, 64))

    xp = x_prompt.reshape(mp, d)
    xs_lm = jnp.transpose(x_sample, (1, 0, 2)).reshape(ms, d)

    h = _norm_in(xp, xs_lm, norm_mix_g[0], tm_row)
    proj = _matmul(h, w_in_ab[0], tm_in, tn)
    mixed, a_p, v_p = _mix_ab_prompt(proj, conv_a[0], ln_b_g[0], ln_b_b[0], w_s[0], b_s[0],
                                     n_seq, seq_len, t_rows)
    mixed, a_s, v_s = _mix_ab_sample(mixed, proj, jnp.transpose(state_conv_a[0], (1, 0, 2)),
                                     conv_a[0], ln_b_g[0], ln_b_b[0], w_s[0], b_s[0], mp, dl, nseq_s)
    x, h = _out_proj(mixed, w_out_ab[0].astype(BF16), xp, xs_lm, norm_ffn_g[0], tm_row)
    act = _ffn_up(h, ffn_w1, ffn_w3, tm_in, tf_up)
    x = _ffn_down(act, ffn_w2, x, tm_mid, tn_down)

    h = _norm_rows(x, norm_mix_g[1], tm_mid)
    proj = _matmul(h, w_in_cd[0], tm_in, tn)
    mixed, c_p, d_p = _mix_cd_prompt(proj, conv_c[0], conv_c_b[0], ln_c_g[0], ln_c_b[0], pool_w[0],
                                     pool_scale[0], n_seq, seq_len, t_rows)
    mixed, g_s = _mix_cd_sample(mixed, proj, jnp.transpose(state_conv_c[0], (1, 0, 2)),
                                jnp.transpose(state_pool_d[0], (1, 0, 2)), conv_c[0], conv_c_b[0],
                                ln_c_g[0], ln_c_b[0], pool_w[0], pool_scale[0], mp, dl, nseq_s)
    (x,) = _out_proj(mixed, w_out_cd[0].astype(BF16), x, x, None, tm_row)
    top_i, gates = _router(x, norm_ffn_g[1], router[0], tm_mid)
    y_p, y_s = _moe(x, top_i, gates, norm_ffn_g[1], norm_final_g, exp_w1[0], exp_w3[0], exp_w2[0],
                    blk_e, sub_e, tf_e, tm_c, mp)

    dc = conv_c.shape[2]
    y_prompt = y_p.reshape(n_seq, seq_len, d)
    y_sample = jnp.transpose(y_s.reshape(dl, nseq_s, d), (1, 0, 2))
    new_a_s = jnp.transpose(a_s, (1, 0, 2))
    new_v_s = jnp.transpose(v_s.reshape(dl, nseq_s, -1), (1, 0, 2))
    g_rows = jnp.transpose(g_s.reshape(dl, nseq_s, dc), (1, 0, 2))
    p_rows = jnp.transpose(proj[mp:, 2 * dc:].reshape(dl, nseq_s, -1), (1, 0, 2))
    n_c = state_conv_c.shape[2]
    n_d = state_pool_d.shape[2]
    new_c_s = jnp.concatenate([state_conv_c[0], g_rows], axis=1)[:, -n_c:]
    new_d_s = jnp.concatenate([state_pool_d[0], p_rows], axis=1)[:, -n_d:]
    return (y_prompt, y_sample,
            a_p[None], new_a_s[None],
            v_p[None], new_v_s[None],
            c_p[None], new_c_s[None],
            d_p[None], new_d_s[None])
```

```python
import functools

import jax
import jax.numpy as jnp
from jax import lax
from jax.experimental import pallas as pl
from jax.experimental.pallas import tpu as pltpu

F32 = jnp.float32
BF16 = jnp.bfloat16
I32 = jnp.int32

EPS = 1e-6
PAST_LEN = 16384
CHUNK = 128
HEAD_B = 128
POOL_WINDOWS = (2, 4, 8, 16)
TOP_K = 2

VMEM_LIMIT_BYTES = 60 * 1024 * 1024
SUBLANES = 8
LANES = 128


def _params(*sem):
    return pltpu.CompilerParams(dimension_semantics=sem, vmem_limit_bytes=VMEM_LIMIT_BYTES)


def _pick(n, candidates):
    for c in candidates:
        if n % c == 0:
            return c
    raise ValueError(f"no tile in {candidates} divides {n}")


def _rms(x, g):
    return x * lax.rsqrt(jnp.mean(x * x, axis=-1, keepdims=True) + EPS) * g


def _layernorm(x, g, b):
    mu = jnp.mean(x, axis=-1, keepdims=True)
    xc = x - mu
    return xc * lax.rsqrt(jnp.mean(xc * xc, axis=-1, keepdims=True) + EPS) * g + b


def _silu(x):
    return x * jax.nn.sigmoid(x)


def _dot(a, b):
    return jnp.dot(a, b, preferred_element_type=F32)


def _mix_ab_prompt_kernel(bg_ref, cg_ref, xin_ref, u_ref, v_ref, cgh_ref, xinh_ref,
                          cw_ref, lng_ref, lnb_ref, ws_ref, bsb_ref,
                          o_ref, astate_ref, vstate_ref, abuf_ref):
    t = pl.program_id(1)
    t_rows, da = bg_ref.shape
    n_heads = ws_ref.shape[0]
    halo = SUBLANES

    a = cg_ref[...] * xin_ref[...]
    prev = jnp.where(t == 0, 0.0, cgh_ref[...] * xinh_ref[...])
    abuf_ref[0:halo, :] = prev
    abuf_ref[halo:halo + t_rows, :] = a
    conv = (cw_ref[0:1, :] * abuf_ref[halo - 2:halo - 2 + t_rows, :]
            + cw_ref[1:2, :] * abuf_ref[halo - 1:halo - 1 + t_rows, :]
            + cw_ref[2:3, :] * a)
    o_ref[:, 0:da] = (bg_ref[...] * conv).astype(BF16)
    astate_ref[0] = abuf_ref[halo + t_rows - 2:halo + t_rows, :]

    row = lax.broadcasted_iota(I32, (CHUNK, CHUNK), 0)
    col = lax.broadcasted_iota(I32, (CHUNK, CHUNK), 1)
    lower = row >= col
    for h in range(n_heads):
        cols = slice(h * HEAD_B, (h + 1) * HEAD_B)
        vn = _layernorm(jax.nn.gelu(v_ref[:, cols]), lng_ref[:, cols], lnb_ref[:, cols])
        vstate_ref[0, :, cols] = vn[t_rows - CHUNK:t_rows]
        w = jnp.where(lower, ws_ref[h], 0.0).astype(BF16)
        for c in range(t_rows // CHUNK):
            rows = slice(c * CHUNK, (c + 1) * CHUNK)
            mix = _dot(w, vn[rows].astype(BF16)) + bsb_ref[h]
            o_ref[rows, da + h * HEAD_B:da + (h + 1) * HEAD_B] = (
                jax.nn.gelu(u_ref[rows, cols]) * mix).astype(BF16)


def _mix_ab_prompt(proj, conv_a, ln_g, ln_b, w_s, b_s, n_seq, seq_len, t_rows):
    m = proj.shape[0]
    da = conv_a.shape[1]
    db = ln_g.shape[0]
    n_heads = w_s.shape[0]
    tiles = seq_len // t_rows
    halo_blocks = t_rows // SUBLANES

    def col_spec(c):
        return pl.BlockSpec((t_rows, da), lambda b, t: (b * tiles + t, c))

    def halo_spec(c):
        return pl.BlockSpec(
            (SUBLANES, da),
            lambda b, t: (jnp.maximum((b * tiles + t) * halo_blocks - 1, 0), c))

    def full(shape):
        return pl.BlockSpec(shape, lambda b, t: (0,) * len(shape))

    bsb = jnp.broadcast_to(b_s[:, :CHUNK, None], (n_heads, CHUNK, HEAD_B))
    return pl.pallas_call(
        _mix_ab_prompt_kernel,
        grid=(n_seq, tiles),
        in_specs=[col_spec(0), col_spec(1), col_spec(2), col_spec(3), col_spec(4),
                  halo_spec(1), halo_spec(2),
                  full((conv_a.shape[0], da)), full((1, db)), full((1, db)),
                  full((n_heads, CHUNK, CHUNK)), full((n_heads, CHUNK, HEAD_B))],
        out_specs=[pl.BlockSpec((t_rows, da + db), lambda b, t: (b * tiles + t, 0)),
                   pl.BlockSpec((1, 2, da), lambda b, t: (b, 0, 0)),
                   pl.BlockSpec((1, CHUNK, db), lambda b, t: (b, 0, 0))],
        out_shape=[jax.ShapeDtypeStruct((m, da + db), BF16),
                   jax.ShapeDtypeStruct((n_seq, 2, da), F32),
                   jax.ShapeDtypeStruct((n_seq, CHUNK, db), F32)],
        scratch_shapes=[pltpu.VMEM((SUBLANES + t_rows, da), F32)],
        compiler_params=_params("parallel", "arbitrary"),
        name="mix_ab_prompt",
    )(proj, proj, proj, proj, proj, proj, proj, conv_a, ln_g.reshape(1, db), ln_b.reshape(1, db),
      w_s[:, :CHUNK, :CHUNK], bsb)


def _mix_ab_sample_kernel(mixed_hbm_ref, bg_ref, cg_ref, xin_ref, u_ref, v_ref, st_ref,
                          cw_ref, lng_ref, lnb_ref, wsl_ref, bsl_ref,
                          o_ref, astate_ref, vstate_ref, *, dl, nseq):
    del mixed_hbm_ref
    da = bg_ref.shape[1]
    n_heads = lng_ref.shape[1] // HEAD_B
    n_state = st_ref.shape[0]

    def slab(l):
        return slice(l * nseq, (l + 1) * nseq)

    a_pad = [st_ref[k] for k in range(n_state)]
    a_pad += [cg_ref[slab(l), :] * xin_ref[slab(l), :] for l in range(dl)]
    for l in range(dl):
        conv = (cw_ref[0:1, :] * a_pad[l] + cw_ref[1:2, :] * a_pad[l + 1]
                + cw_ref[2:3, :] * a_pad[l + 2])
        o_ref[slab(l), 0:da] = (bg_ref[slab(l), :] * conv).astype(BF16)
    for k in range(n_state):
        astate_ref[k] = a_pad[dl + k]

    for h in range(n_heads):
        cols = slice(h * HEAD_B, (h + 1) * HEAD_B)
        vn = _layernorm(jax.nn.gelu(v_ref[:, cols]), lng_ref[:, cols], lnb_ref[:, cols])
        vstate_ref[:, cols] = vn
        for i in range(dl):
            mix = bsl_ref[i:i + 1, cols]
            for j in range(i + 1):
                mix = mix + wsl_ref[i * dl + j:i * dl + j + 1, cols] * vn[slab(j)]
            o_ref[slab(i), da + h * HEAD_B:da + (h + 1) * HEAD_B] = (
                jax.nn.gelu(u_ref[slab(i), cols]) * mix).astype(BF16)


def _mix_ab_sample(mixed, proj, state_lm, conv_a, ln_g, ln_b, w_s, b_s, row0, dl, nseq):
    ms = dl * nseq
    da = conv_a.shape[1]
    db = ln_g.shape[0]
    n_heads = w_s.shape[0]
    rb = row0 // ms

    def col_spec(c):
        return pl.BlockSpec((ms, da), lambda i: (rb, c))

    def full(shape):
        return pl.BlockSpec(shape, lambda i: (0,) * len(shape))

    wsl = jnp.repeat(jnp.transpose(w_s[:, :dl, :dl], (1, 2, 0)).reshape(dl * dl, n_heads), HEAD_B, axis=1)
    bsl = jnp.repeat(jnp.transpose(b_s[:, :dl], (1, 0)), HEAD_B, axis=1)
    return pl.pallas_call(
        functools.partial(_mix_ab_sample_kernel, dl=dl, nseq=nseq),
        grid=(1,),
        in_specs=[pl.BlockSpec(memory_space=pl.ANY),
                  col_spec(0), col_spec(1), col_spec(2), col_spec(3), col_spec(4),
                  full(state_lm.shape), full((conv_a.shape[0], da)), full((1, db)), full((1, db)),
                  full((dl * dl, db)), full((dl, db))],
        out_specs=[pl.BlockSpec((ms, da + db), lambda i: (rb, 0)),
                   full(state_lm.shape),
                   full((ms, db))],
        out_shape=[jax.ShapeDtypeStruct(mixed.shape, BF16),
                   jax.ShapeDtypeStruct(state_lm.shape, F32),
                   jax.ShapeDtypeStruct((ms, db), F32)],
        input_output_aliases={0: 0},
        compiler_params=_params("arbitrary"),
        name="mix_ab_sample",
    )(mixed, proj, proj, proj, proj, proj, state_lm, conv_a, ln_g.reshape(1, db), ln_b.reshape(1, db),
      wsl, bsl)


def _mix_cd_prompt_kernel(ga_ref, gb_ref, p_ref, gah_ref, gbh_ref, ph_ref,
                          cw_ref, cb_ref, lng_ref, lnb_ref, pw_ref, ps_ref,
                          o_ref, cstate_ref, dstate_ref, gbuf_ref, pbuf_ref, cbuf_ref, shift_ref):
    t = pl.program_id(1)
    t_rows, dc = ga_ref.shape
    k_c = cw_ref.shape[0]
    c_halo = gah_ref.shape[0]
    d_halo = ph_ref.shape[0]
    gd = pw_ref.shape[1]

    g = ga_ref[...] * jax.nn.sigmoid(gb_ref[...])
    gbuf_ref[0:c_halo, :] = jnp.where(t == 0, 0.0, gah_ref[...] * jax.nn.sigmoid(gbh_ref[...]))
    gbuf_ref[c_halo:c_halo + t_rows, :] = g
    n_shift = shift_ref.shape[1]
    for r in range(1, SUBLANES):
        shift_ref[r - 1] = gbuf_ref[r:r + n_shift, :]
    first = c_halo - (k_c - 1)
    for cblk in range(dc // LANES):
        cols = slice(cblk * LANES, (cblk + 1) * LANES)
        acc = cb_ref[:, cols]
        for k in range(k_c):
            q, r = divmod(first + k, SUBLANES)
            if r == 0:
                win = gbuf_ref[first + k:first + k + t_rows, cols]
            else:
                win = shift_ref[r - 1, q * SUBLANES:q * SUBLANES + t_rows, cols]
            acc = acc + cw_ref[k:k + 1, cols] * win
        cbuf_ref[:, cols] = acc
    o_ref[:, 0:dc] = _silu(_layernorm(cbuf_ref[...], lng_ref[...], lnb_ref[...])).astype(BF16)
    cstate_ref[0] = gbuf_ref[c_halo + t_rows - (k_c - 1):c_halo + t_rows, :]

    pbuf_ref[0:d_halo, :] = jnp.where(t == 0, 0.0, ph_ref[...])
    pbuf_ref[d_halo:d_halo + t_rows, :] = p_ref[...]
    pos = t * t_rows + lax.broadcasted_iota(I32, (t_rows, 1), 0)
    for gi, w in enumerate(POOL_WINDOWS):
        cols = slice(gi * gd, (gi + 1) * gd)
        cur = p_ref[:, cols]
        s = cur
        for k in range(1, w):
            s = s + pbuf_ref[d_halo - k:d_halo - k + t_rows, cols]
        cnt = jnp.minimum(w, pos + 1).astype(F32)
        diff = (s / cnt - cur).astype(BF16)
        d = _dot(diff, pw_ref[gi].astype(BF16)) * ps_ref[:, cols]
        o_ref[:, dc + gi * gd:dc + (gi + 1) * gd] = d.astype(BF16)
    n_keep = dstate_ref.shape[1]
    dstate_ref[0] = pbuf_ref[d_halo + t_rows - n_keep:d_halo + t_rows, :]


def _mix_cd_prompt(proj, conv_c, conv_c_b, ln_g, ln_b, pool_w, pool_scale, n_seq, seq_len, t_rows):
    m = proj.shape[0]
    k_c, dc = conv_c.shape
    dd = pool_scale.shape[0]
    n_pool, gd = pool_w.shape[0], pool_w.shape[1]
    tiles = seq_len // t_rows
    c_halo = -(-(k_c - 1) // SUBLANES) * SUBLANES
    d_halo = -(-(max(POOL_WINDOWS) - 1) // SUBLANES) * SUBLANES
    assert t_rows % c_halo == 0 and t_rows % d_halo == 0 and dc == dd

    def col_spec(c):
        return pl.BlockSpec((t_rows, dc), lambda b, t: (b * tiles + t, c))

    def halo_spec(c, rows):
        per = t_rows // rows
        return pl.BlockSpec((rows, dc),
                            lambda b, t: (jnp.maximum((b * tiles + t) * per - 1, 0), c))

    def full(shape):
        return pl.BlockSpec(shape, lambda b, t: (0,) * len(shape))

    return pl.pallas_call(
        _mix_cd_prompt_kernel,
        grid=(n_seq, tiles),
        in_specs=[col_spec(0), col_spec(1), col_spec(2),
                  halo_spec(0, c_halo), halo_spec(1, c_halo), halo_spec(2, d_halo),
                  full((k_c, dc)), full((1, dc)), full((1, dc)), full((1, dc)),
                  full((n_pool, gd, gd)), full((1, dd))],
        out_specs=[pl.BlockSpec((t_rows, dc + dd), lambda b, t: (b * tiles + t, 0)),
                   pl.BlockSpec((1, k_c - 1, dc), lambda b, t: (b, 0, 0)),
                   pl.BlockSpec((1, max(POOL_WINDOWS) - 1, dd), lambda b, t: (b, 0, 0))],
        out_shape=[jax.ShapeDtypeStruct((m, dc + dd), BF16),
                   jax.ShapeDtypeStruct((n_seq, k_c - 1, dc), F32),
                   jax.ShapeDtypeStruct((n_seq, max(POOL_WINDOWS) - 1, dd), F32)],
        scratch_shapes=[pltpu.VMEM((c_halo + t_rows, dc), F32),
                        pltpu.VMEM((d_halo + t_rows, dd), F32),
                        pltpu.VMEM((t_rows, dc), F32),
                        pltpu.VMEM((SUBLANES - 1, c_halo + t_rows - SUBLANES, dc), F32)],
        compiler_params=_params("parallel", "arbitrary"),
        name="mix_cd_prompt",
    )(proj, proj, proj, proj, proj, proj, conv_c, conv_c_b.reshape(1, dc), ln_g.reshape(1, dc),
      ln_b.reshape(1, dc), pool_w, pool_scale.reshape(1, dd))


def _mix_cd_sample_kernel(mixed_hbm_ref, ga_ref, gb_ref, p_ref, stc_ref, std_ref,
                          cw_ref, cb_ref, lng_ref, lnb_ref, pw_ref, ps_ref,
                          o_ref, g_ref, cbuf_ref, dbuf_ref, *, dl, nseq, start_pos):
    del mixed_hbm_ref
    gi = pl.program_id(0)
    n_groups = pl.num_programs(0)
    gd = ga_ref.shape[1]
    k_c = cw_ref.shape[0]
    n_c = stc_ref.shape[0]
    n_d = std_ref.shape[0]

    def slab(l):
        return slice(l * nseq, (l + 1) * nseq)

    g = ga_ref[...] * jax.nn.sigmoid(gb_ref[...])
    g_ref[...] = g
    c_pad = [stc_ref[k] for k in range(n_c)] + [g[slab(l)] for l in range(dl)]
    d_pad = [std_ref[k] for k in range(n_d)] + [p_ref[slab(l), :] for l in range(dl)]
    for l in range(dl):
        acc = cb_ref[...] + cw_ref[0:1, :] * c_pad[l]
        for k in range(1, k_c):
            acc = acc + cw_ref[k:k + 1, :] * c_pad[l + k]
        cbuf_ref[gi, slab(l), :] = acc

        cur = d_pad[n_d + l]
        s = cur
        pooled = jnp.zeros_like(cur)
        for k in range(1, max(POOL_WINDOWS) + 1):
            if k in POOL_WINDOWS:
                cnt = float(min(k, start_pos + l + 1))
                pooled = jnp.where(gi == POOL_WINDOWS.index(k), s / cnt, pooled)
            if k < max(POOL_WINDOWS):
                s = s + d_pad[n_d + l - k]
        diff = (pooled - cur).astype(BF16)
        dbuf_ref[gi, slab(l), :] = (_dot(diff, pw_ref[0].astype(BF16)) * ps_ref[...]).astype(BF16)

    @pl.when(gi == n_groups - 1)
    def _():
        n_blk = cbuf_ref.shape[0]
        dc = n_blk * gd
        total = cbuf_ref[0].sum(axis=-1, keepdims=True)
        for b in range(1, n_blk):
            total = total + cbuf_ref[b].sum(axis=-1, keepdims=True)
        mu = total / dc
        var = None
        for b in range(n_blk):
            xc = cbuf_ref[b] - mu
            sq = (xc * xc).sum(axis=-1, keepdims=True)
            var = sq if var is None else var + sq
        rstd = lax.rsqrt(var / dc + EPS)
        for b in range(n_blk):
            cols = slice(b * gd, (b + 1) * gd)
            y = (cbuf_ref[b] - mu) * rstd * lng_ref[:, cols] + lnb_ref[:, cols]
            o_ref[:, cols] = _silu(y).astype(BF16)
            o_ref[:, dc + b * gd:dc + (b + 1) * gd] = dbuf_ref[b]


def _mix_cd_sample(mixed, proj, stc_lm, std_lm, conv_c, conv_c_b, ln_g, ln_b, pool_w, pool_scale,
                   row0, dl, nseq):
    ms = dl * nseq
    k_c, dc = conv_c.shape
    dd = pool_scale.shape[0]
    n_pool, gd = pool_w.shape[0], pool_w.shape[1]
    assert dc == dd == n_pool * gd and n_pool == len(POOL_WINDOWS)
    rb = row0 // ms

    return pl.pallas_call(
        functools.partial(_mix_cd_sample_kernel, dl=dl, nseq=nseq, start_pos=PAST_LEN),
        grid=(n_pool,),
        in_specs=[pl.BlockSpec(memory_space=pl.ANY),
                  pl.BlockSpec((ms, gd), lambda i: (rb, i)),
                  pl.BlockSpec((ms, gd), lambda i: (rb, n_pool + i)),
                  pl.BlockSpec((ms, gd), lambda i: (rb, 2 * n_pool + i)),
                  pl.BlockSpec((stc_lm.shape[0], nseq, gd), lambda i: (0, 0, i)),
                  pl.BlockSpec((std_lm.shape[0], nseq, gd), lambda i: (0, 0, i)),
                  pl.BlockSpec((k_c, gd), lambda i: (0, i)),
                  pl.BlockSpec((1, gd), lambda i: (0, i)),
                  pl.BlockSpec((1, dc), lambda i: (0, 0)),
                  pl.BlockSpec((1, dc), lambda i: (0, 0)),
                  pl.BlockSpec((1, gd, gd), lambda i: (i, 0, 0)),
                  pl.BlockSpec((1, gd), lambda i: (0, i))],
        out_specs=[pl.BlockSpec((ms, dc + dd), lambda i: (rb, 0)),
                   pl.BlockSpec((ms, gd), lambda i: (0, i))],
        out_shape=[jax.ShapeDtypeStruct(mixed.shape, BF16),
                   jax.ShapeDtypeStruct((ms, dc), F32)],
        scratch_shapes=[pltpu.VMEM((n_pool, ms, gd), F32),
                        pltpu.VMEM((n_pool, ms, gd), BF16)],
        input_output_aliases={0: 0},
        compiler_params=_params("arbitrary"),
        name="mix_cd_sample",
    )(mixed, proj, proj, proj, stc_lm, std_lm, conv_c, conv_c_b.reshape(1, dc), ln_g.reshape(1, dc),
      ln_b.reshape(1, dc), pool_w, pool_scale.reshape(1, dd))


def _norm_in_kernel(xp_ref, xs_ref, g_ref, o_ref, *, n_a):
    i = pl.program_id(0)

    @pl.when(i < n_a)
    def _():
        o_ref[...] = _rms(xp_ref[...], g_ref[...]).astype(BF16)

    @pl.when(i >= n_a)
    def _():
        o_ref[...] = _rms(xs_ref[...], g_ref[...]).astype(BF16)


def _norm_in(xp, xs, g, tm):
    mp, d = xp.shape
    ms = xs.shape[0]
    n_a, n_b = mp // tm, ms // tm
    return pl.pallas_call(
        functools.partial(_norm_in_kernel, n_a=n_a),
        grid=(n_a + n_b,),
        in_specs=[pl.BlockSpec((tm, d), lambda i: (jnp.minimum(i, n_a - 1), 0)),
                  pl.BlockSpec((tm, d), lambda i: (jnp.maximum(i - n_a, 0), 0)),
                  pl.BlockSpec((1, d), lambda i: (0, 0))],
        out_specs=pl.BlockSpec((tm, d), lambda i: (i, 0)),
        out_shape=jax.ShapeDtypeStruct((mp + ms, d), BF16),
        compiler_params=_params("arbitrary"),
        name="norm_in",
    )(xp, xs, g.reshape(1, d))


def _matmul_kernel(a_ref, w_ref, o_ref):
    o_ref[...] = _dot(a_ref[...], w_ref[...].astype(BF16))


def _matmul(a, w, tm, tn):
    m, k = a.shape
    n = w.shape[1]
    return pl.pallas_call(
        _matmul_kernel,
        grid=(m // tm, n // tn),
        in_specs=[pl.BlockSpec((tm, k), lambda i, j: (i, 0)),
                  pl.BlockSpec((k, tn), lambda i, j: (0, j))],
        out_specs=pl.BlockSpec((tm, tn), lambda i, j: (i, j)),
        out_shape=jax.ShapeDtypeStruct((m, n), F32),
        compiler_params=_params("parallel", "arbitrary"),
        name="in_proj",
    )(a, w)


def _ffn_up_kernel(h_ref, w1_ref, w3_ref, o_ref):
    w1 = w1_ref[0].astype(BF16)
    w3 = w3_ref[0].astype(BF16)
    tm = h_ref.shape[0]
    n_parts = 2 if tm % (2 * 2 * SUBLANES) == 0 else 1
    rows_per = tm // n_parts
    for part in range(n_parts):
        rows = slice(part * rows_per, (part + 1) * rows_per)
        h = h_ref[rows, :]
        o_ref[rows, :] = (_silu(_dot(h, w1)) * _dot(h, w3)).astype(BF16)


def _ffn_up(h, w1, w3, tm, tf):
    m, d = h.shape
    f = w1.shape[2]
    return pl.pallas_call(
        _ffn_up_kernel,
        grid=(m // tm, f // tf),
        in_specs=[pl.BlockSpec((tm, d), lambda i, j: (i, 0)),
                  pl.BlockSpec((1, d, tf), lambda i, j: (0, 0, j)),
                  pl.BlockSpec((1, d, tf), lambda i, j: (0, 0, j))],
        out_specs=pl.BlockSpec((tm, tf), lambda i, j: (i, j)),
        out_shape=jax.ShapeDtypeStruct((m, f), BF16),
        compiler_params=_params("parallel", "arbitrary"),
        name="ffn_up",
    )(h, w1, w3)


def _ffn_down_kernel(a_ref, w_ref, r_ref, o_ref):
    o_ref[...] = r_ref[...] + _dot(a_ref[...], w_ref[0].astype(BF16))


def _ffn_down(act, w2, res, tm, tn):
    m, f = act.shape
    d = w2.shape[2]
    return pl.pallas_call(
        _ffn_down_kernel,
        grid=(m // tm, d // tn),
        in_specs=[pl.BlockSpec((tm, f), lambda i, j: (i, 0)),
                  pl.BlockSpec((1, f, tn), lambda i, j: (0, 0, j)),
                  pl.BlockSpec((tm, tn), lambda i, j: (i, j))],
        out_specs=pl.BlockSpec((tm, tn), lambda i, j: (i, j)),
        out_shape=jax.ShapeDtypeStruct((m, d), F32),
        compiler_params=_params("parallel", "arbitrary"),
        name="ffn_down",
    )(act, w2, res)


def _norm_rows_kernel(x_ref, g_ref, o_ref):
    o_ref[...] = _rms(x_ref[...], g_ref[...]).astype(BF16)


def _norm_rows(x, g, tm):
    m, d = x.shape
    return pl.pallas_call(
        _norm_rows_kernel,
        grid=(m // tm,),
        in_specs=[pl.BlockSpec((tm, d), lambda i: (i, 0)),
                  pl.BlockSpec((1, d), lambda i: (0, 0))],
        out_specs=pl.BlockSpec((tm, d), lambda i: (i, 0)),
        out_shape=jax.ShapeDtypeStruct((m, d), BF16),
        compiler_params=_params("parallel"),
        name="norm_rows",
    )(x, g.reshape(1, d))


def _out_proj_kernel(a_ref, w_ref, ra_ref, rb_ref, *rest, n_a, with_h):
    i = pl.program_id(0)
    if with_h:
        g_ref, x_ref, h_ref = rest
    else:
        (x_ref,) = rest
    tm = a_ref.shape[0]
    n_parts = 2 if tm % (2 * 2 * SUBLANES) == 0 else 1
    rows_per = tm // n_parts
    for part in range(n_parts):
        rows = slice(part * rows_per, (part + 1) * rows_per)
        res = jnp.where(i < n_a, ra_ref[rows, :], rb_ref[rows, :])
        x = res + _dot(a_ref[rows, :], w_ref[...])
        x_ref[rows, :] = x
        if with_h:
            h_ref[rows, :] = _rms(x, g_ref[...]).astype(BF16)


def _out_proj(a, w_bf16, res_a, res_b, g, tm):
    m, k = a.shape
    d = w_bf16.shape[1]
    n_a = res_a.shape[0] // tm
    n_b = m // tm - n_a
    assert n_a * tm == res_a.shape[0] and (n_b == 0 or n_b * tm == res_b.shape[0])
    with_h = g is not None
    in_specs = [pl.BlockSpec((tm, k), lambda i: (i, 0)),
                pl.BlockSpec((k, d), lambda i: (0, 0), pipeline_mode=pl.Buffered(1)),
                pl.BlockSpec((tm, d), lambda i: (jnp.minimum(i, n_a - 1), 0)),
                pl.BlockSpec((tm, d), lambda i: (jnp.maximum(i - n_a, 0), 0))]
    args = [a, w_bf16, res_a, res_b]
    row_spec = pl.BlockSpec((tm, d), lambda i: (i, 0))
    out_specs = [row_spec]
    out_shape = [jax.ShapeDtypeStruct((m, d), F32)]
    if with_h:
        in_specs.append(pl.BlockSpec((1, d), lambda i: (0, 0)))
        args.append(g.reshape(1, d))
        out_specs.append(row_spec)
        out_shape.append(jax.ShapeDtypeStruct((m, d), BF16))
    return pl.pallas_call(
        functools.partial(_out_proj_kernel, n_a=n_a, with_h=with_h),
        grid=(m // tm,),
        in_specs=in_specs, out_specs=out_specs, out_shape=out_shape,
        compiler_params=_params("arbitrary"),
        name="out_proj_h" if with_h else "out_proj",
    )(*args)


def _router_kernel(x_ref, g_ref, r_ref, idx_ref, gate_ref):
    h = _rms(x_ref[...], g_ref[...])
    h_hi = h.astype(BF16)
    h_lo = (h - h_hi.astype(F32)).astype(BF16)
    r = r_ref[...]
    r_hi = r.astype(BF16)
    r_lo = (r - r_hi.astype(F32)).astype(BF16)
    logits = _dot(h_hi, r_hi) + (_dot(h_hi, r_lo) + _dot(h_lo, r_hi))
    n_exp = float(logits.shape[1])
    lane = lax.broadcasted_iota(I32, logits.shape, 1).astype(F32)
    m1 = jnp.max(logits, axis=-1, keepdims=True)
    i1 = jnp.min(jnp.where(logits == m1, lane, n_exp), axis=-1, keepdims=True)
    rest = jnp.where(lane == i1, -jnp.inf, logits)
    m2 = jnp.max(rest, axis=-1, keepdims=True)
    i2 = jnp.min(jnp.where(rest == m2, lane, n_exp), axis=-1, keepdims=True)
    e = jnp.exp(m2 - m1)
    two = lax.broadcasted_iota(I32, idx_ref.shape, 1)
    idx_ref[...] = jnp.where(two == 0, i1, i2).astype(I32)
    gate_ref[...] = jnp.where(two == 0, 1.0, e) / (1.0 + e)


def _router(x, g, router, tm):
    m, d = x.shape
    n_exp = router.shape[1]
    return pl.pallas_call(
        _router_kernel,
        grid=(m // tm,),
        in_specs=[pl.BlockSpec((tm, d), lambda i: (i, 0)),
                  pl.BlockSpec((1, d), lambda i: (0, 0)),
                  pl.BlockSpec((d, n_exp), lambda i: (0, 0))],
        out_specs=[pl.BlockSpec((tm, TOP_K), lambda i: (i, 0)),
                   pl.BlockSpec((tm, TOP_K), lambda i: (i, 0))],
        out_shape=[jax.ShapeDtypeStruct((m, TOP_K), I32),
                   jax.ShapeDtypeStruct((m, TOP_K), F32)],
        compiler_params=_params("parallel"),
        name="router",
    )(x, g.reshape(1, d), router)


def _experts_kernel(bexp_ref, nsub_ref, nused_ref, *refs, sub, s0):
    if s0:
        refs = refs[1:]
    xs_ref, w1_ref, w3_ref, w2_ref, o_ref, w1b_ref, w3b_ref, w2b_ref = refs
    s = pl.program_id(0) + s0
    j = pl.program_id(1)

    def rows_of(i):
        return pl.ds(pl.multiple_of(i * sub, sub), sub)

    def group(first, count, w1v, w3v, w2v):
        acts = []
        for t in range(count):
            x = xs_ref[rows_of(first + t), :]
            acts.append((_silu(_dot(x, w1v)) * _dot(x, w3v)).astype(BF16))
        for t in range(count):
            o_ref[rows_of(first + t), :] += _dot(acts[t], w2v)

    def from_scratch(first, count):
        group(first, count, w1b_ref[...], w3b_ref[...], w2b_ref[...])

    @pl.when(s < nused_ref[0])
    def _():
        n = nsub_ref[s]

        @pl.when(j == 0)
        def _():
            def init(i, carry):
                o_ref[rows_of(i), :] = jnp.zeros((sub, o_ref.shape[1]), F32)
                return carry
            lax.fori_loop(0, n, init, 0)

        @pl.when(n >= 3)
        def _():
            w1v = w1_ref[0].astype(BF16)
            w3v = w3_ref[0].astype(BF16)
            w2v = w2_ref[0].astype(BF16)
            w1b_ref[...] = w1v
            w3b_ref[...] = w3v
            w2b_ref[...] = w2v
            group(0, 3, w1v, w3v, w2v)

        @pl.when(n < 3)
        def _():
            w1b_ref[...] = w1_ref[0].astype(BF16)
            w3b_ref[...] = w3_ref[0].astype(BF16)
            w2b_ref[...] = w2_ref[0].astype(BF16)

        def triple(p, carry):
            from_scratch(3 * p, 3)
            return carry
        lax.fori_loop(1, n // 3, triple, 0)

        done = (n // 3) * 3
        rem = n - done

        @pl.when(rem == 1)
        def _():
            from_scratch(done, 1)

        @pl.when(rem == 2)
        def _():
            from_scratch(done, 2)


def _experts(xs, w1, w3, w2, bexp, nsub, nused, blk, sub, tf, s0, n_s, ys=None):
    n_rows, d = xs.shape
    f = w1.shape[2]
    n_f = f // tf
    assert (ys is None) == (s0 == 0)

    def blk_idx(s, nu):
        return jnp.minimum(s + s0, jnp.maximum(nu[0] - 1, s0))

    def exp_idx(s, be, nu):
        return be[blk_idx(s, nu)]

    def jj(s, j, nu):
        return jnp.where(s + s0 < nu[0], j, n_f - 1)

    row_spec = pl.BlockSpec((blk, d), lambda s, j, be, ns, nu: (blk_idx(s, nu), 0),
                            pipeline_mode=pl.Buffered(1))
    in_specs = [pl.BlockSpec((blk, d), lambda s, j, be, ns, nu: (blk_idx(s, nu), 0)),
                pl.BlockSpec((1, d, tf), lambda s, j, be, ns, nu: (exp_idx(s, be, nu), 0, jj(s, j, nu))),
                pl.BlockSpec((1, d, tf), lambda s, j, be, ns, nu: (exp_idx(s, be, nu), 0, jj(s, j, nu))),
                pl.BlockSpec((1, tf, d), lambda s, j, be, ns, nu: (exp_idx(s, be, nu), jj(s, j, nu), 0))]
    args = [xs, w1, w3, w2]
    aliases = {}
    if s0:
        in_specs = [pl.BlockSpec(memory_space=pl.ANY)] + in_specs
        args = [ys] + args
        aliases = {3: 0}
    return pl.pallas_call(
        functools.partial(_experts_kernel, sub=sub, s0=s0),
        grid_spec=pltpu.PrefetchScalarGridSpec(
            num_scalar_prefetch=3,
            grid=(n_s, n_f),
            in_specs=in_specs,
            out_specs=row_spec,
            scratch_shapes=[pltpu.VMEM((d, tf), BF16), pltpu.VMEM((d, tf), BF16),
                            pltpu.VMEM((tf, d), BF16)]),
        out_shape=jax.ShapeDtypeStruct((n_rows, d), F32),
        input_output_aliases=aliases,
        compiler_params=_params("arbitrary", "arbitrary"),
        name="moe_experts" if s0 == 0 else "moe_experts_overflow",
    )(bexp, nsub, nused, *args)


def _dispatch_kernel(tile_ids_ref, n_tiles_ref, src_ref, x_hbm_ref, g_ref, o_ref, buf_ref, sem_ref, *, sub):
    k = pl.program_id(0)
    n_valid = n_tiles_ref[0]

    def start(step, slot):
        base = tile_ids_ref[step] * sub

        def body(r, carry):
            tok = src_ref[base + r]
            pltpu.make_async_copy(x_hbm_ref.at[pl.ds(tok, 1)], buf_ref.at[slot, pl.ds(r, 1)],
                                  sem_ref.at[slot]).start()
            return carry
        lax.fori_loop(0, sub, body, 0, unroll=8)

    @pl.when(jnp.logical_and(k == 0, n_valid > 0))
    def _():
        start(0, 0)

    @pl.when(k + 1 < n_valid)
    def _():
        start(k + 1, (k + 1) % 2)

    @pl.when(k < n_valid)
    def _():
        slot = k % 2
        pltpu.make_async_copy(x_hbm_ref.at[pl.ds(0, sub)], buf_ref.at[slot], sem_ref.at[slot]).wait()
        o_ref[...] = _rms(buf_ref[slot], g_ref[...]).astype(BF16)


def _dispatch(x, g, tile_ids, n_tiles, src, n_slots, sub):
    m, d = x.shape
    max_tiles = tile_ids.shape[0]
    return pl.pallas_call(
        functools.partial(_dispatch_kernel, sub=sub),
        grid_spec=pltpu.PrefetchScalarGridSpec(
            num_scalar_prefetch=3,
            grid=(max_tiles,),
            in_specs=[pl.BlockSpec(memory_space=pl.ANY),
                      pl.BlockSpec((1, d), lambda k, ids, nt, src: (0, 0))],
            out_specs=pl.BlockSpec((sub, d), lambda k, ids, nt, src: (ids[k], 0)),
            scratch_shapes=[pltpu.VMEM((2, sub, d), F32), pltpu.SemaphoreType.DMA((2,))]),
        out_shape=jax.ShapeDtypeStruct((n_slots, d), BF16),
        compiler_params=_params("arbitrary"),
        name="moe_dispatch",
    )(tile_ids, n_tiles, src, x, g.reshape(1, d))


def _combine_kernel(pos_ref, x_ref, gate_ref, g_ref, ys_hbm_ref, ya_ref, yb_ref, buf_ref, sem_ref, *, tm, n_a):
    i = pl.program_id(0)
    n = pl.num_programs(0)

    def start(step, slot):
        base = step * (tm * TOP_K)

        def body(r, carry):
            for kk in range(TOP_K):
                p = pos_ref[base + r * TOP_K + kk]
                pltpu.make_async_copy(ys_hbm_ref.at[pl.ds(p, 1)], buf_ref.at[slot, kk, pl.ds(r, 1)],
                                      sem_ref.at[slot]).start()
            return carry
        lax.fori_loop(0, tm, body, 0, unroll=4)

    @pl.when(i == 0)
    def _():
        start(0, 0)

    @pl.when(i + 1 < n)
    def _():
        start(i + 1, (i + 1) % 2)

    slot = i % 2
    for kk in range(TOP_K):
        pltpu.make_async_copy(ys_hbm_ref.at[pl.ds(0, tm)], buf_ref.at[slot, kk], sem_ref.at[slot]).wait()
    y = x_ref[...]
    for kk in range(TOP_K):
        y = y + gate_ref[:, kk:kk + 1] * buf_ref[slot, kk]
    y = _rms(y, g_ref[...])

    @pl.when(i < n_a)
    def _():
        ya_ref[...] = y

    @pl.when(i >= n_a)
    def _():
        yb_ref[...] = y


def _combine(x, gates, pos, ys, g, tm, m_a):
    m, d = x.shape
    n_a = m_a // tm
    assert n_a * tm == m_a and m % tm == 0
    return pl.pallas_call(
        functools.partial(_combine_kernel, tm=tm, n_a=n_a),
        grid_spec=pltpu.PrefetchScalarGridSpec(
            num_scalar_prefetch=1,
            grid=(m // tm,),
            in_specs=[pl.BlockSpec((tm, d), lambda i, pos: (i, 0)),
                      pl.BlockSpec((tm, TOP_K), lambda i, pos: (i, 0)),
                      pl.BlockSpec((1, d), lambda i, pos: (0, 0)),
                      pl.BlockSpec(memory_space=pl.ANY)],
            out_specs=[pl.BlockSpec((tm, d), lambda i, pos: (jnp.minimum(i, n_a - 1), 0)),
                       pl.BlockSpec((tm, d), lambda i, pos: (jnp.maximum(i - n_a, 0), 0))],
            scratch_shapes=[pltpu.VMEM((2, TOP_K, tm, d), F32), pltpu.SemaphoreType.DMA((2,))]),
        out_shape=[jax.ShapeDtypeStruct((m_a, d), F32), jax.ShapeDtypeStruct((m - m_a, d), F32)],
        compiler_params=_params("arbitrary"),
        name="moe_combine",
    )(pos, x, gates, g.reshape(1, d), ys)


def _moe_plan(top_i, n_exp, blk, sub, n_blocks, max_tiles):
    m = top_i.shape[0]
    e_flat = top_i.reshape(-1)
    onehot = (e_flat[:, None] == jnp.arange(n_exp, dtype=I32)[None, :]).astype(I32)
    csum = jnp.cumsum(onehot, axis=0)
    rank = jnp.sum((csum - 1) * onehot, axis=1)
    counts = csum[-1]
    blocks_per = (counts + blk - 1) // blk
    block_end = jnp.cumsum(blocks_per)
    block_start = block_end - blocks_per
    n_used = block_end[-1]
    pos = block_start[e_flat] * blk + rank

    s = jnp.arange(n_blocks, dtype=I32)
    bexp = jnp.minimum(jnp.sum((s[:, None] >= block_end[None, :]).astype(I32), axis=1), n_exp - 1)
    valid = jnp.clip(counts[bexp] - (s - block_start[bexp]) * blk, 0, blk)
    valid = jnp.where(s < n_used, valid, 0)

    by_expert = jnp.argsort(e_flat, stable=True).astype(I32) // TOP_K
    by_expert = jnp.concatenate([by_expert, jnp.zeros((m,), I32)])
    first_pair = jnp.cumsum(counts) - counts
    src = jnp.zeros((n_blocks * blk + m,), I32)
    for e in range(n_exp):
        run = lax.dynamic_slice(by_expert, (first_pair[e],), (m,))
        src = lax.dynamic_update_slice(src, run, (block_start[e] * blk,))
    src = src[:n_blocks * blk]
    nsub = (valid + sub - 1) // sub
    last_exp = bexp[jnp.maximum(n_used - 1, 0)]
    bexp = jnp.where(s < n_used, bexp, last_exp)

    per_blk = blk // sub
    t = jnp.arange(n_blocks * per_blk, dtype=I32)
    tile_valid = (t % per_blk) < nsub[t // per_blk]
    n_tiles = jnp.sum(tile_valid.astype(I32))
    order = jnp.argsort(jnp.logical_not(tile_valid), stable=True).astype(I32)[:max_tiles]
    last_tile = order[jnp.maximum(n_tiles - 1, 0)]
    tile_ids = jnp.where(jnp.arange(max_tiles, dtype=I32) < n_tiles, order, last_tile)
    return pos, src, bexp, nsub, n_used.reshape(1), tile_ids, n_tiles.reshape(1)


def _moe(x, top_i, gates, g_ffn, g_final, w1, w3, w2, blk, sub, tf, tm_c, m_a):
    m, d = x.shape
    n_exp = w1.shape[0]
    n_blocks = (m * TOP_K + n_exp * (blk - 1)) // blk
    max_tiles = (m * TOP_K + n_exp * (sub - 1)) // sub
    pos, src, bexp, nsub, nused, tile_ids, n_tiles = _moe_plan(top_i, n_exp, blk, sub, n_blocks, max_tiles)
    xs = _dispatch(x, g_ffn, tile_ids, n_tiles, src, n_blocks * blk, sub)
    n_main = min(n_blocks, n_exp * (-(-(m * TOP_K) // (n_exp * blk))) + 2)
    ys = _experts(xs, w1, w3, w2, bexp, nsub, nused, blk, sub, tf, 0, n_main)
    if n_main < n_blocks:
        ys = lax.cond(
            nused[0] > n_main,
            lambda y: _experts(xs, w1, w3, w2, bexp, nsub, nused, blk, sub, tf, n_main, n_blocks - n_main, ys=y),
            lambda y: y, ys)
    return _combine(x, gates, pos, ys, g_final, tm_c, m_a)


def kernel(x_prompt, x_sample, state_conv_a, state_conv_c, state_pool_d, norm_mix_g, norm_ffn_g, norm_final_g, w_in_ab, conv_a, ln_b_g, ln_b_b, w_s, b_s, w_out_ab, ffn_w1, ffn_w3, ffn_w2, w_in_cd, conv_c, conv_c_b, ln_c_g, ln_c_b, pool_w, pool_scale, w_out_cd, router, exp_w1, exp_w3, exp_w2):
    n_seq, seq_len, d = x_prompt.shape
    nseq_s, dl, _ = x_sample.shape
    mp, ms = n_seq * seq_len, nseq_s * dl
    m = mp + ms
    assert norm_mix_g.shape[0] == 2, "two layers: conv/chunk-MLP + dense FFN, then conformer/pool + MoE"
    assert seq_len % CHUNK == 0 and dl <= CHUNK and mp % ms == 0

    big = m * TOP_K >= 4 * 2560
    tm_in = _pick(m, (2176, 1088, 544, 272))
    tm_mid = _pick(m, (1088, 544, 272))
    tm_row = _pick(ms, (512, 256, 128, 64))
    t_rows = _pick(seq_len, (256, 128))
    tn = 512
    tf_up = _pick(ffn_w1.shape[2], (512, 256))
    tn_down = 256
    tf_e = 256
    blk_e, sub_e = (2560, 256) if big else (512, 128)
    tm_c = _pick(ms, (256, 128, 64))

    xp = x_prompt.reshape(mp, d)
    xs_lm = jnp.transpose(x_sample, (1, 0, 2)).reshape(ms, d)

    h = _norm_in(xp, xs_lm, norm_mix_g[0], tm_row)
    proj = _matmul(h, w_in_ab[0], tm_in, tn)
    mixed, a_p, v_p = _mix_ab_prompt(proj, conv_a[0], ln_b_g[0], ln_b_b[0], w_s[0], b_s[0],
                                     n_seq, seq_len, t_rows)
    mixed, a_s, v_s = _mix_ab_sample(mixed, proj, jnp.transpose(state_conv_a[0], (1, 0, 2)),
                                     conv_a[0], ln_b_g[0], ln_b_b[0], w_s[0], b_s[0], mp, dl, nseq_s)
    x, h = _out_proj(mixed, w_out_ab[0].astype(BF16), xp, xs_lm, norm_ffn_g[0], tm_row)
    act = _ffn_up(h, ffn_w1, ffn_w3, tm_in, tf_up)
    x = _ffn_down(act, ffn_w2, x, tm_mid, tn_down)

    h = _norm_rows(x, norm_mix_g[1], tm_mid)
    proj = _matmul(h, w_in_cd[0], tm_in, tn)
    mixed, c_p, d_p = _mix_cd_prompt(proj, conv_c[0], conv_c_b[0], ln_c_g[0], ln_c_b[0], pool_w[0],
                                     pool_scale[0], n_seq, seq_len, t_rows)
    mixed, g_s = _mix_cd_sample(mixed, proj, jnp.transpose(state_conv_c[0], (1, 0, 2)),
                                jnp.transpose(state_pool_d[0], (1, 0, 2)), conv_c[0], conv_c_b[0],
                                ln_c_g[0], ln_c_b[0], pool_w[0], pool_scale[0], mp, dl, nseq_s)
    (x,) = _out_proj(mixed, w_out_cd[0].astype(BF16), x, x, None, tm_row)
    top_i, gates = _router(x, norm_ffn_g[1], router[0], tm_mid)
    y_p, y_s = _moe(x, top_i, gates, norm_ffn_g[1], norm_final_g, exp_w1[0], exp_w3[0], exp_w2[0],
                    blk_e, sub_e, tf_e, tm_c, mp)

    dc = conv_c.shape[2]
    y_prompt = y_p.reshape(n_seq, seq_len, d)
    y_sample = jnp.transpose(y_s.reshape(dl, nseq_s, d), (1, 0, 2))
    new_a_s = jnp.transpose(a_s, (1, 0, 2))
    new_v_s = jnp.transpose(v_s.reshape(dl, nseq_s, -1), (1, 0, 2))
    g_rows = jnp.transpose(g_s.reshape(dl, nseq_s, dc), (1, 0, 2))
    p_rows = jnp.transpose(proj[mp:, 2 * dc:].reshape(dl, nseq_s, -1), (1, 0, 2))
    n_c = state_conv_c.shape[2]
    n_d = state_pool_d.shape[2]
    new_c_s = jnp.concatenate([state_conv_c[0], g_rows], axis=1)[:, -n_c:]
    new_d_s = jnp.concatenate([state_pool_d[0], p_rows], axis=1)[:, -n_d:]
    return (y_prompt, y_sample,
            a_p[None], new_a_s[None],
            v_p[None], new_v_s[None],
            c_p[None], new_c_s[None],
            d_p[None], new_d_s[None])
```

```python
import functools

import jax
import jax.numpy as jnp
from jax import lax
from jax.experimental import pallas as pl
from jax.experimental.pallas import tpu as pltpu

F32 = jnp.float32
BF16 = jnp.bfloat16
I32 = jnp.int32

EPS = 1e-6
PAST_LEN = 16384
CHUNK = 128
HEAD_B = 128
POOL_WINDOWS = (2, 4, 8, 16)
TOP_K = 2

VMEM_LIMIT_BYTES = 60 * 1024 * 1024
SUBLANES = 8
LANES = 128


def _params(*sem):
    return pltpu.CompilerParams(dimension_semantics=sem, vmem_limit_bytes=VMEM_LIMIT_BYTES)


def _pick(n, candidates):
    for c in candidates:
        if n % c == 0:
            return c
    raise ValueError(f"no tile in {candidates} divides {n}")


def _rms(x, g):
    return x * lax.rsqrt(jnp.mean(x * x, axis=-1, keepdims=True) + EPS) * g


def _layernorm(x, g, b):
    mu = jnp.mean(x, axis=-1, keepdims=True)
    xc = x - mu
    return xc * lax.rsqrt(jnp.mean(xc * xc, axis=-1, keepdims=True) + EPS) * g + b


def _silu(x):
    return x * jax.nn.sigmoid(x)


def _dot(a, b):
    return jnp.dot(a, b, preferred_element_type=F32)


def _f32(x):
    return x.astype(F32)


def _mix_ab_prompt_kernel(bg_ref, cg_ref, xin_ref, u_ref, v_ref, cgh_ref, xinh_ref,
                          cw_ref, lng_ref, lnb_ref, ws_ref, bsb_ref,
                          o_ref, astate_ref, vstate_ref, abuf_ref):
    t = pl.program_id(1)
    t_rows, da = bg_ref.shape
    n_heads = ws_ref.shape[0]
    halo = cgh_ref.shape[0]

    a = _f32(cg_ref[...]) * _f32(xin_ref[...])
    prev = jnp.where(t == 0, 0.0, _f32(cgh_ref[...]) * _f32(xinh_ref[...]))
    abuf_ref[0:halo, :] = prev
    abuf_ref[halo:halo + t_rows, :] = a
    conv = (cw_ref[0:1, :] * abuf_ref[halo - 2:halo - 2 + t_rows, :]
            + cw_ref[1:2, :] * abuf_ref[halo - 1:halo - 1 + t_rows, :]
            + cw_ref[2:3, :] * a)
    o_ref[:, 0:da] = (_f32(bg_ref[...]) * conv).astype(BF16)
    astate_ref[0] = abuf_ref[halo + t_rows - 2:halo + t_rows, :]

    row = lax.broadcasted_iota(I32, (CHUNK, CHUNK), 0)
    col = lax.broadcasted_iota(I32, (CHUNK, CHUNK), 1)
    lower = row >= col
    for h in range(n_heads):
        cols = slice(h * HEAD_B, (h + 1) * HEAD_B)
        vn = _layernorm(jax.nn.gelu(_f32(v_ref[:, cols])), lng_ref[:, cols], lnb_ref[:, cols])
        vstate_ref[0, :, cols] = vn[t_rows - CHUNK:t_rows]
        w = jnp.where(lower, ws_ref[h], 0.0).astype(BF16)
        for c in range(t_rows // CHUNK):
            rows = slice(c * CHUNK, (c + 1) * CHUNK)
            mix = _dot(w, vn[rows].astype(BF16)) + bsb_ref[h]
            o_ref[rows, da + h * HEAD_B:da + (h + 1) * HEAD_B] = (
                jax.nn.gelu(_f32(u_ref[rows, cols])) * mix).astype(BF16)


def _mix_ab_prompt(proj, conv_a, ln_g, ln_b, w_s, b_s, n_seq, seq_len, t_rows):
    m = proj.shape[0]
    da = conv_a.shape[1]
    db = ln_g.shape[0]
    n_heads = w_s.shape[0]
    tiles = seq_len // t_rows
    halo_rows = 2 * SUBLANES
    halo_blocks = t_rows // halo_rows

    def col_spec(c):
        return pl.BlockSpec((t_rows, da), lambda b, t: (b * tiles + t, c))

    def halo_spec(c):
        return pl.BlockSpec(
            (halo_rows, da),
            lambda b, t: (jnp.maximum((b * tiles + t) * halo_blocks - 1, 0), c))

    def full(shape):
        return pl.BlockSpec(shape, lambda b, t: (0,) * len(shape))

    bsb = jnp.broadcast_to(b_s[:, :CHUNK, None], (n_heads, CHUNK, HEAD_B))
    return pl.pallas_call(
        _mix_ab_prompt_kernel,
        grid=(n_seq, tiles),
        in_specs=[col_spec(0), col_spec(1), col_spec(2), col_spec(3), col_spec(4),
                  halo_spec(1), halo_spec(2),
                  full((conv_a.shape[0], da)), full((1, db)), full((1, db)),
                  full((n_heads, CHUNK, CHUNK)), full((n_heads, CHUNK, HEAD_B))],
        out_specs=[pl.BlockSpec((t_rows, da + db), lambda b, t: (b * tiles + t, 0)),
                   pl.BlockSpec((1, 2, da), lambda b, t: (b, 0, 0)),
                   pl.BlockSpec((1, CHUNK, db), lambda b, t: (b, 0, 0))],
        out_shape=[jax.ShapeDtypeStruct((m, da + db), BF16),
                   jax.ShapeDtypeStruct((n_seq, 2, da), F32),
                   jax.ShapeDtypeStruct((n_seq, CHUNK, db), F32)],
        scratch_shapes=[pltpu.VMEM((halo_rows + t_rows, da), F32)],
        compiler_params=_params("parallel", "arbitrary"),
        name="mix_ab_prompt",
    )(proj, proj, proj, proj, proj, proj, proj, conv_a, ln_g.reshape(1, db), ln_b.reshape(1, db),
      w_s[:, :CHUNK, :CHUNK], bsb)


def _mix_ab_sample_kernel(mixed_hbm_ref, bg_ref, cg_ref, xin_ref, u_ref, v_ref, st_ref,
                          cw_ref, lng_ref, lnb_ref, wsl_ref, bsl_ref,
                          o_ref, astate_ref, vstate_ref, *, dl, nseq):
    del mixed_hbm_ref
    da = bg_ref.shape[1]
    n_heads = lng_ref.shape[1] // HEAD_B
    n_state = st_ref.shape[0]

    def slab(l):
        return slice(l * nseq, (l + 1) * nseq)

    a_pad = [st_ref[k] for k in range(n_state)]
    a_pad += [_f32(cg_ref[slab(l), :]) * _f32(xin_ref[slab(l), :]) for l in range(dl)]
    for l in range(dl):
        conv = (cw_ref[0:1, :] * a_pad[l] + cw_ref[1:2, :] * a_pad[l + 1]
                + cw_ref[2:3, :] * a_pad[l + 2])
        o_ref[slab(l), 0:da] = (_f32(bg_ref[slab(l), :]) * conv).astype(BF16)
    for k in range(n_state):
        astate_ref[k] = a_pad[dl + k]

    for h in range(n_heads):
        cols = slice(h * HEAD_B, (h + 1) * HEAD_B)
        vn = _layernorm(jax.nn.gelu(_f32(v_ref[:, cols])), lng_ref[:, cols], lnb_ref[:, cols])
        vstate_ref[:, cols] = vn
        for i in range(dl):
            mix = bsl_ref[i:i + 1, cols]
            for j in range(i + 1):
                mix = mix + wsl_ref[i * dl + j:i * dl + j + 1, cols] * vn[slab(j)]
            o_ref[slab(i), da + h * HEAD_B:da + (h + 1) * HEAD_B] = (
                jax.nn.gelu(_f32(u_ref[slab(i), cols])) * mix).astype(BF16)


def _mix_ab_sample(mixed, proj, state_lm, conv_a, ln_g, ln_b, w_s, b_s, row0, dl, nseq):
    ms = dl * nseq
    da = conv_a.shape[1]
    db = ln_g.shape[0]
    n_heads = w_s.shape[0]
    rb = row0 // ms

    def col_spec(c):
        return pl.BlockSpec((ms, da), lambda i: (rb, c))

    def full(shape):
        return pl.BlockSpec(shape, lambda i: (0,) * len(shape))

    wsl = jnp.repeat(jnp.transpose(w_s[:, :dl, :dl], (1, 2, 0)).reshape(dl * dl, n_heads), HEAD_B, axis=1)
    bsl = jnp.repeat(jnp.transpose(b_s[:, :dl], (1, 0)), HEAD_B, axis=1)
    return pl.pallas_call(
        functools.partial(_mix_ab_sample_kernel, dl=dl, nseq=nseq),
        grid=(1,),
        in_specs=[pl.BlockSpec(memory_space=pl.ANY),
                  col_spec(0), col_spec(1), col_spec(2), col_spec(3), col_spec(4),
                  full(state_lm.shape), full((conv_a.shape[0], da)), full((1, db)), full((1, db)),
                  full((dl * dl, db)), full((dl, db))],
        out_specs=[pl.BlockSpec((ms, da + db), lambda i: (rb, 0)),
                   full(state_lm.shape),
                   full((ms, db))],
        out_shape=[jax.ShapeDtypeStruct(mixed.shape, BF16),
                   jax.ShapeDtypeStruct(state_lm.shape, F32),
                   jax.ShapeDtypeStruct((ms, db), F32)],
        input_output_aliases={0: 0},
        compiler_params=_params("arbitrary"),
        name="mix_ab_sample",
    )(mixed, proj, proj, proj, proj, proj, state_lm, conv_a, ln_g.reshape(1, db), ln_b.reshape(1, db),
      wsl, bsl)


def _mix_cd_prompt_kernel(ga_ref, gb_ref, p_ref, gah_ref, gbh_ref, ph_ref,
                          cw_ref, cb_ref, lng_ref, lnb_ref, pw_ref, ps_ref,
                          o_ref, cstate_ref, dstate_ref, gbuf_ref, pbuf_ref, cbuf_ref, shift_ref):
    t = pl.program_id(1)
    t_rows, dc = ga_ref.shape
    k_c = cw_ref.shape[0]
    c_halo = gah_ref.shape[0]
    d_halo = ph_ref.shape[0]
    gd = pw_ref.shape[1]

    g = _f32(ga_ref[...]) * jax.nn.sigmoid(_f32(gb_ref[...]))
    gbuf_ref[0:c_halo, :] = jnp.where(
        t == 0, 0.0, _f32(gah_ref[...]) * jax.nn.sigmoid(_f32(gbh_ref[...])))
    gbuf_ref[c_halo:c_halo + t_rows, :] = g
    n_shift = shift_ref.shape[1]
    for r in range(1, SUBLANES):
        shift_ref[r - 1] = gbuf_ref[r:r + n_shift, :]
    first = c_halo - (k_c - 1)
    for cblk in range(dc // LANES):
        cols = slice(cblk * LANES, (cblk + 1) * LANES)
        acc = cb_ref[:, cols]
        for k in range(k_c):
            q, r = divmod(first + k, SUBLANES)
            if r == 0:
                win = gbuf_ref[first + k:first + k + t_rows, cols]
            else:
                win = shift_ref[r - 1, q * SUBLANES:q * SUBLANES + t_rows, cols]
            acc = acc + cw_ref[k:k + 1, cols] * win
        cbuf_ref[:, cols] = acc
    o_ref[:, 0:dc] = _silu(_layernorm(cbuf_ref[...], lng_ref[...], lnb_ref[...])).astype(BF16)
    cstate_ref[0] = gbuf_ref[c_halo + t_rows - (k_c - 1):c_halo + t_rows, :]

    pbuf_ref[0:d_halo, :] = jnp.where(t == 0, 0.0, _f32(ph_ref[...]))
    pbuf_ref[d_halo:d_halo + t_rows, :] = _f32(p_ref[...])
    pos = t * t_rows + lax.broadcasted_iota(I32, (t_rows, 1), 0)
    for gi, w in enumerate(POOL_WINDOWS):
        cols = slice(gi * gd, (gi + 1) * gd)
        cur = pbuf_ref[d_halo:d_halo + t_rows, cols]
        s = cur
        for k in range(1, w):
            s = s + pbuf_ref[d_halo - k:d_halo - k + t_rows, cols]
        cnt = jnp.minimum(w, pos + 1).astype(F32)
        diff = (s / cnt - cur).astype(BF16)
        d = _dot(diff, pw_ref[gi].astype(BF16)) * ps_ref[:, cols]
        o_ref[:, dc + gi * gd:dc + (gi + 1) * gd] = d.astype(BF16)
    n_keep = dstate_ref.shape[1]
    dstate_ref[0] = pbuf_ref[d_halo + t_rows - n_keep:d_halo + t_rows, :]


def _mix_cd_prompt(proj, conv_c, conv_c_b, ln_g, ln_b, pool_w, pool_scale, n_seq, seq_len, t_rows):
    m = proj.shape[0]
    k_c, dc = conv_c.shape
    dd = pool_scale.shape[0]
    n_pool, gd = pool_w.shape[0], pool_w.shape[1]
    tiles = seq_len // t_rows
    c_halo = -(-(k_c - 1) // SUBLANES) * SUBLANES
    d_halo = -(-(max(POOL_WINDOWS) - 1) // SUBLANES) * SUBLANES
    assert t_rows % c_halo == 0 and t_rows % d_halo == 0 and dc == dd

    def col_spec(c):
        return pl.BlockSpec((t_rows, dc), lambda b, t: (b * tiles + t, c))

    def halo_spec(c, rows):
        per = t_rows // rows
        return pl.BlockSpec((rows, dc),
                            lambda b, t: (jnp.maximum((b * tiles + t) * per - 1, 0), c))

    def full(shape):
        return pl.BlockSpec(shape, lambda b, t: (0,) * len(shape))

    return pl.pallas_call(
        _mix_cd_prompt_kernel,
        grid=(n_seq, tiles),
        in_specs=[col_spec(0), col_spec(1), col_spec(2),
                  halo_spec(0, c_halo), halo_spec(1, c_halo), halo_spec(2, d_halo),
                  full((k_c, dc)), full((1, dc)), full((1, dc)), full((1, dc)),
                  full((n_pool, gd, gd)), full((1, dd))],
        out_specs=[pl.BlockSpec((t_rows, dc + dd), lambda b, t: (b * tiles + t, 0)),
                   pl.BlockSpec((1, k_c - 1, dc), lambda b, t: (b, 0, 0)),
                   pl.BlockSpec((1, max(POOL_WINDOWS) - 1, dd), lambda b, t: (b, 0, 0))],
        out_shape=[jax.ShapeDtypeStruct((m, dc + dd), BF16),
                   jax.ShapeDtypeStruct((n_seq, k_c - 1, dc), F32),
                   jax.ShapeDtypeStruct((n_seq, max(POOL_WINDOWS) - 1, dd), F32)],
        scratch_shapes=[pltpu.VMEM((c_halo + t_rows, dc), F32),
                        pltpu.VMEM((d_halo + t_rows, dd), F32),
                        pltpu.VMEM((t_rows, dc), F32),
                        pltpu.VMEM((SUBLANES - 1, c_halo + t_rows - SUBLANES, dc), F32)],
        compiler_params=_params("parallel", "arbitrary"),
        name="mix_cd_prompt",
    )(proj, proj, proj, proj, proj, proj, conv_c, conv_c_b.reshape(1, dc), ln_g.reshape(1, dc),
      ln_b.reshape(1, dc), pool_w, pool_scale.reshape(1, dd))


def _mix_cd_sample_kernel(mixed_hbm_ref, ga_ref, gb_ref, p_ref, stc_ref, std_ref,
                          cw_ref, cb_ref, lng_ref, lnb_ref, pw_ref, ps_ref,
                          o_ref, g_ref, cbuf_ref, dbuf_ref, *, dl, nseq, start_pos):
    del mixed_hbm_ref
    gi = pl.program_id(0)
    n_groups = pl.num_programs(0)
    gd = ga_ref.shape[1]
    k_c = cw_ref.shape[0]
    n_c = stc_ref.shape[0]
    n_d = std_ref.shape[0]

    def slab(l):
        return slice(l * nseq, (l + 1) * nseq)

    g = _f32(ga_ref[...]) * jax.nn.sigmoid(_f32(gb_ref[...]))
    g_ref[...] = g
    c_pad = [stc_ref[k] for k in range(n_c)] + [g[slab(l)] for l in range(dl)]
    d_pad = [std_ref[k] for k in range(n_d)] + [_f32(p_ref[slab(l), :]) for l in range(dl)]
    for l in range(dl):
        acc = cb_ref[...] + cw_ref[0:1, :] * c_pad[l]
        for k in range(1, k_c):
            acc = acc + cw_ref[k:k + 1, :] * c_pad[l + k]
        cbuf_ref[gi, slab(l), :] = acc

        cur = d_pad[n_d + l]
        s = cur
        pooled = jnp.zeros_like(cur)
        for k in range(1, max(POOL_WINDOWS) + 1):
            if k in POOL_WINDOWS:
                cnt = float(min(k, start_pos + l + 1))
                pooled = jnp.where(gi == POOL_WINDOWS.index(k), s / cnt, pooled)
            if k < max(POOL_WINDOWS):
                s = s + d_pad[n_d + l - k]
        diff = (pooled - cur).astype(BF16)
        dbuf_ref[gi, slab(l), :] = (_dot(diff, pw_ref[0].astype(BF16)) * ps_ref[...]).astype(BF16)

    @pl.when(gi == n_groups - 1)
    def _():
        n_blk = cbuf_ref.shape[0]
        dc = n_blk * gd
        total = cbuf_ref[0].sum(axis=-1, keepdims=True)
        for b in range(1, n_blk):
            total = total + cbuf_ref[b].sum(axis=-1, keepdims=True)
        mu = total / dc
        var = None
        for b in range(n_blk):
            xc = cbuf_ref[b] - mu
            sq = (xc * xc).sum(axis=-1, keepdims=True)
            var = sq if var is None else var + sq
        rstd = lax.rsqrt(var / dc + EPS)
        for b in range(n_blk):
            cols = slice(b * gd, (b + 1) * gd)
            y = (cbuf_ref[b] - mu) * rstd * lng_ref[:, cols] + lnb_ref[:, cols]
            o_ref[:, cols] = _silu(y).astype(BF16)
            o_ref[:, dc + b * gd:dc + (b + 1) * gd] = dbuf_ref[b]


def _mix_cd_sample(mixed, proj, stc_lm, std_lm, conv_c, conv_c_b, ln_g, ln_b, pool_w, pool_scale,
                   row0, dl, nseq):
    ms = dl * nseq
    k_c, dc = conv_c.shape
    dd = pool_scale.shape[0]
    n_pool, gd = pool_w.shape[0], pool_w.shape[1]
    assert dc == dd == n_pool * gd and n_pool == len(POOL_WINDOWS)
    rb = row0 // ms

    return pl.pallas_call(
        functools.partial(_mix_cd_sample_kernel, dl=dl, nseq=nseq, start_pos=PAST_LEN),
        grid=(n_pool,),
        in_specs=[pl.BlockSpec(memory_space=pl.ANY),
                  pl.BlockSpec((ms, gd), lambda i: (rb, i)),
                  pl.BlockSpec((ms, gd), lambda i: (rb, n_pool + i)),
                  pl.BlockSpec((ms, gd), lambda i: (rb, 2 * n_pool + i)),
                  pl.BlockSpec((stc_lm.shape[0], nseq, gd), lambda i: (0, 0, i)),
                  pl.BlockSpec((std_lm.shape[0], nseq, gd), lambda i: (0, 0, i)),
                  pl.BlockSpec((k_c, gd), lambda i: (0, i)),
                  pl.BlockSpec((1, gd), lambda i: (0, i)),
                  pl.BlockSpec((1, dc), lambda i: (0, 0)),
                  pl.BlockSpec((1, dc), lambda i: (0, 0)),
                  pl.BlockSpec((1, gd, gd), lambda i: (i, 0, 0)),
                  pl.BlockSpec((1, gd), lambda i: (0, i))],
        out_specs=[pl.BlockSpec((ms, dc + dd), lambda i: (rb, 0)),
                   pl.BlockSpec((ms, gd), lambda i: (0, i))],
        out_shape=[jax.ShapeDtypeStruct(mixed.shape, BF16),
                   jax.ShapeDtypeStruct((ms, dc), F32)],
        scratch_shapes=[pltpu.VMEM((n_pool, ms, gd), F32),
                        pltpu.VMEM((n_pool, ms, gd), BF16)],
        input_output_aliases={0: 0},
        compiler_params=_params("arbitrary"),
        name="mix_cd_sample",
    )(mixed, proj, proj, proj, stc_lm, std_lm, conv_c, conv_c_b.reshape(1, dc), ln_g.reshape(1, dc),
      ln_b.reshape(1, dc), pool_w, pool_scale.reshape(1, dd))


def _norm_in_kernel(xp_ref, xs_ref, g_ref, o_ref, *, n_a):
    i = pl.program_id(0)

    @pl.when(i < n_a)
    def _():
        o_ref[...] = _rms(xp_ref[...], g_ref[...]).astype(BF16)

    @pl.when(i >= n_a)
    def _():
        o_ref[...] = _rms(xs_ref[...], g_ref[...]).astype(BF16)


def _norm_in(xp, xs, g, tm):
    mp, d = xp.shape
    ms = xs.shape[0]
    n_a, n_b = mp // tm, ms // tm
    return pl.pallas_call(
        functools.partial(_norm_in_kernel, n_a=n_a),
        grid=(n_a + n_b,),
        in_specs=[pl.BlockSpec((tm, d), lambda i: (jnp.minimum(i, n_a - 1), 0)),
                  pl.BlockSpec((tm, d), lambda i: (jnp.maximum(i - n_a, 0), 0)),
                  pl.BlockSpec((1, d), lambda i: (0, 0))],
        out_specs=pl.BlockSpec((tm, d), lambda i: (i, 0)),
        out_shape=jax.ShapeDtypeStruct((mp + ms, d), BF16),
        compiler_params=_params("arbitrary"),
        name="norm_in",
    )(xp, xs, g.reshape(1, d))


def _matmul_kernel(a_ref, w_ref, o_ref):
    o_ref[...] = _dot(a_ref[...], w_ref[...].astype(BF16)).astype(o_ref.dtype)


def _matmul(a, w, tm, tn):
    m, k = a.shape
    n = w.shape[1]
    return pl.pallas_call(
        _matmul_kernel,
        grid=(m // tm, n // tn),
        in_specs=[pl.BlockSpec((tm, k), lambda i, j: (i, 0)),
                  pl.BlockSpec((k, tn), lambda i, j: (0, j))],
        out_specs=pl.BlockSpec((tm, tn), lambda i, j: (i, j)),
        out_shape=jax.ShapeDtypeStruct((m, n), BF16),
        compiler_params=_params("parallel", "arbitrary"),
        name="in_proj",
    )(a, w)


def _ffn_up_kernel(h_ref, w1_ref, w3_ref, o_ref):
    w1 = w1_ref[0].astype(BF16)
    w3 = w3_ref[0].astype(BF16)
    tm = h_ref.shape[0]
    n_parts = 2 if tm % (2 * 2 * SUBLANES) == 0 else 1
    rows_per = tm // n_parts
    for part in range(n_parts):
        rows = slice(part * rows_per, (part + 1) * rows_per)
        h = h_ref[rows, :]
        o_ref[rows, :] = (_silu(_dot(h, w1)) * _dot(h, w3)).astype(BF16)


def _ffn_up(h, w1, w3, tm, tf):
    m, d = h.shape
    f = w1.shape[2]
    return pl.pallas_call(
        _ffn_up_kernel,
        grid=(m // tm, f // tf),
        in_specs=[pl.BlockSpec((tm, d), lambda i, j: (i, 0)),
                  pl.BlockSpec((1, d, tf), lambda i, j: (0, 0, j)),
                  pl.BlockSpec((1, d, tf), lambda i, j: (0, 0, j))],
        out_specs=pl.BlockSpec((tm, tf), lambda i, j: (i, j)),
        out_shape=jax.ShapeDtypeStruct((m, f), BF16),
        compiler_params=_params("parallel", "arbitrary"),
        name="ffn_up",
    )(h, w1, w3)


def _ffn_down_kernel(a_ref, w_ref, r_ref, o_ref):
    o_ref[...] = r_ref[...] + _dot(a_ref[...], w_ref[0].astype(BF16))


def _ffn_down(act, w2, res, tm, tn):
    m, f = act.shape
    d = w2.shape[2]
    return pl.pallas_call(
        _ffn_down_kernel,
        grid=(m // tm, d // tn),
        in_specs=[pl.BlockSpec((tm, f), lambda i, j: (i, 0)),
                  pl.BlockSpec((1, f, tn), lambda i, j: (0, 0, j)),
                  pl.BlockSpec((tm, tn), lambda i, j: (i, j))],
        out_specs=pl.BlockSpec((tm, tn), lambda i, j: (i, j)),
        out_shape=jax.ShapeDtypeStruct((m, d), F32),
        compiler_params=_params("parallel", "arbitrary"),
        name="ffn_down",
    )(act, w2, res)


def _norm_rows_kernel(x_ref, g_ref, o_ref):
    o_ref[...] = _rms(x_ref[...], g_ref[...]).astype(BF16)


def _norm_rows(x, g, tm):
    m, d = x.shape
    return pl.pallas_call(
        _norm_rows_kernel,
        grid=(m // tm,),
        in_specs=[pl.BlockSpec((tm, d), lambda i: (i, 0)),
                  pl.BlockSpec((1, d), lambda i: (0, 0))],
        out_specs=pl.BlockSpec((tm, d), lambda i: (i, 0)),
        out_shape=jax.ShapeDtypeStruct((m, d), BF16),
        compiler_params=_params("parallel"),
        name="norm_rows",
    )(x, g.reshape(1, d))


def _out_proj_kernel(a_ref, w_ref, ra_ref, rb_ref, *rest, n_a, with_h):
    i = pl.program_id(0)
    if with_h:
        g_ref, x_ref, h_ref = rest
    else:
        (x_ref,) = rest
    tm = a_ref.shape[0]
    n_parts = 2 if tm % (2 * 2 * SUBLANES) == 0 else 1
    rows_per = tm // n_parts
    for part in range(n_parts):
        rows = slice(part * rows_per, (part + 1) * rows_per)
        res = jnp.where(i < n_a, ra_ref[rows, :], rb_ref[rows, :])
        x = res + _dot(a_ref[rows, :], w_ref[...])
        x_ref[rows, :] = x
        if with_h:
            h_ref[rows, :] = _rms(x, g_ref[...]).astype(BF16)


def _out_proj(a, w_bf16, res_a, res_b, g, tm):
    m, k = a.shape
    d = w_bf16.shape[1]
    n_a = res_a.shape[0] // tm
    n_b = m // tm - n_a
    assert n_a * tm == res_a.shape[0] and (n_b == 0 or n_b * tm == res_b.shape[0])
    with_h = g is not None
    in_specs = [pl.BlockSpec((tm, k), lambda i: (i, 0)),
                pl.BlockSpec((k, d), lambda i: (0, 0), pipeline_mode=pl.Buffered(1)),
                pl.BlockSpec((tm, d), lambda i: (jnp.minimum(i, n_a - 1), 0)),
                pl.BlockSpec((tm, d), lambda i: (jnp.maximum(i - n_a, 0), 0))]
    args = [a, w_bf16, res_a, res_b]
    row_spec = pl.BlockSpec((tm, d), lambda i: (i, 0))
    out_specs = [row_spec]
    out_shape = [jax.ShapeDtypeStruct((m, d), F32)]
    if with_h:
        in_specs.append(pl.BlockSpec((1, d), lambda i: (0, 0)))
        args.append(g.reshape(1, d))
        out_specs.append(row_spec)
        out_shape.append(jax.ShapeDtypeStruct((m, d), BF16))
    return pl.pallas_call(
        functools.partial(_out_proj_kernel, n_a=n_a, with_h=with_h),
        grid=(m // tm,),
        in_specs=in_specs, out_specs=out_specs, out_shape=out_shape,
        compiler_params=_params("arbitrary"),
        name="out_proj_h" if with_h else "out_proj",
    )(*args)


def _router_kernel(x_ref, g_ref, r_ref, idx_ref, gate_ref):
    h = _rms(x_ref[...], g_ref[...])
    h_hi = h.astype(BF16)
    h_lo = (h - h_hi.astype(F32)).astype(BF16)
    r = r_ref[...]
    r_hi = r.astype(BF16)
    r_lo = (r - r_hi.astype(F32)).astype(BF16)
    logits = _dot(h_hi, r_hi) + (_dot(h_hi, r_lo) + _dot(h_lo, r_hi))
    n_exp = float(logits.shape[1])
    lane = lax.broadcasted_iota(I32, logits.shape, 1).astype(F32)
    m1 = jnp.max(logits, axis=-1, keepdims=True)
    i1 = jnp.min(jnp.where(logits == m1, lane, n_exp), axis=-1, keepdims=True)
    rest = jnp.where(lane == i1, -jnp.inf, logits)
    m2 = jnp.max(rest, axis=-1, keepdims=True)
    i2 = jnp.min(jnp.where(rest == m2, lane, n_exp), axis=-1, keepdims=True)
    e = jnp.exp(m2 - m1)
    two = lax.broadcasted_iota(I32, idx_ref.shape, 1)
    idx_ref[...] = jnp.where(two == 0, i1, i2).astype(I32)
    gate_ref[...] = jnp.where(two == 0, 1.0, e) / (1.0 + e)


def _router(x, g, router, tm):
    m, d = x.shape
    n_exp = router.shape[1]
    return pl.pallas_call(
        _router_kernel,
        grid=(m // tm,),
        in_specs=[pl.BlockSpec((tm, d), lambda i: (i, 0)),
                  pl.BlockSpec((1, d), lambda i: (0, 0)),
                  pl.BlockSpec((d, n_exp), lambda i: (0, 0))],
        out_specs=[pl.BlockSpec((tm, TOP_K), lambda i: (i, 0)),
                   pl.BlockSpec((tm, TOP_K), lambda i: (i, 0))],
        out_shape=[jax.ShapeDtypeStruct((m, TOP_K), I32),
                   jax.ShapeDtypeStruct((m, TOP_K), F32)],
        compiler_params=_params("parallel"),
        name="router",
    )(x, g.reshape(1, d), router)


def _experts_kernel(bexp_ref, nsub_ref, nused_ref, *refs, sub, s0):
    if s0:
        refs = refs[1:]
    xs_ref, w1_ref, w3_ref, w2_ref, o_ref, w1b_ref, w3b_ref, w2b_ref = refs
    s = pl.program_id(0) + s0
    j = pl.program_id(1)

    def rows_of(i):
        return pl.ds(pl.multiple_of(i * sub, sub), sub)

    def group(first, count, w1v, w3v, w2v):
        acts = []
        for t in range(count):
            x = xs_ref[rows_of(first + t), :]
            acts.append((_silu(_dot(x, w1v)) * _dot(x, w3v)).astype(BF16))
        for t in range(count):
            o_ref[rows_of(first + t), :] += _dot(acts[t], w2v)

    def from_scratch(first, count):
        group(first, count, w1b_ref[...], w3b_ref[...], w2b_ref[...])

    @pl.when(s < nused_ref[0])
    def _():
        n = nsub_ref[s]

        @pl.when(j == 0)
        def _():
            def init(i, carry):
                o_ref[rows_of(i), :] = jnp.zeros((sub, o_ref.shape[1]), F32)
                return carry
            lax.fori_loop(0, n, init, 0)

        @pl.when(n >= 3)
        def _():
            w1v = w1_ref[0].astype(BF16)
            w3v = w3_ref[0].astype(BF16)
            w2v = w2_ref[0].astype(BF16)
            w1b_ref[...] = w1v
            w3b_ref[...] = w3v
            w2b_ref[...] = w2v
            group(0, 3, w1v, w3v, w2v)

        @pl.when(n < 3)
        def _():
            w1b_ref[...] = w1_ref[0].astype(BF16)
            w3b_ref[...] = w3_ref[0].astype(BF16)
            w2b_ref[...] = w2_ref[0].astype(BF16)

        def triple(p, carry):
            from_scratch(3 * p, 3)
            return carry
        lax.fori_loop(1, n // 3, triple, 0)

        done = (n // 3) * 3
        rem = n - done

        @pl.when(rem == 1)
        def _():
            from_scratch(done, 1)

        @pl.when(rem == 2)
        def _():
            from_scratch(done, 2)


def _experts(xs, w1, w3, w2, bexp, nsub, nused, blk, sub, tf, s0, n_s, ys=None):
    n_rows, d = xs.shape
    f = w1.shape[2]
    n_f = f // tf
    assert (ys is None) == (s0 == 0)

    def blk_idx(s, nu):
        return jnp.minimum(s + s0, jnp.maximum(nu[0] - 1, s0))

    def exp_idx(s, be, nu):
        return be[blk_idx(s, nu)]

    def jj(s, j, nu):
        return jnp.where(s + s0 < nu[0], j, n_f - 1)

    row_spec = pl.BlockSpec((blk, d), lambda s, j, be, ns, nu: (blk_idx(s, nu), 0),
                            pipeline_mode=pl.Buffered(1))
    in_specs = [pl.BlockSpec((blk, d), lambda s, j, be, ns, nu: (blk_idx(s, nu), 0)),
                pl.BlockSpec((1, d, tf), lambda s, j, be, ns, nu: (exp_idx(s, be, nu), 0, jj(s, j, nu))),
                pl.BlockSpec((1, d, tf), lambda s, j, be, ns, nu: (exp_idx(s, be, nu), 0, jj(s, j, nu))),
                pl.BlockSpec((1, tf, d), lambda s, j, be, ns, nu: (exp_idx(s, be, nu), jj(s, j, nu), 0))]
    args = [xs, w1, w3, w2]
    aliases = {}
    if s0:
        in_specs = [pl.BlockSpec(memory_space=pl.ANY)] + in_specs
        args = [ys] + args
        aliases = {3: 0}
    return pl.pallas_call(
        functools.partial(_experts_kernel, sub=sub, s0=s0),
        grid_spec=pltpu.PrefetchScalarGridSpec(
            num_scalar_prefetch=3,
            grid=(n_s, n_f),
            in_specs=in_specs,
            out_specs=row_spec,
            scratch_shapes=[pltpu.VMEM((d, tf), BF16), pltpu.VMEM((d, tf), BF16),
                            pltpu.VMEM((tf, d), BF16)]),
        out_shape=jax.ShapeDtypeStruct((n_rows, d), F32),
        input_output_aliases=aliases,
        compiler_params=_params("arbitrary", "arbitrary"),
        name="moe_experts" if s0 == 0 else "moe_experts_overflow",
    )(bexp, nsub, nused, *args)


def _dispatch_kernel(tile_ids_ref, n_tiles_ref, src_ref, x_hbm_ref, g_ref, o_ref, buf_ref, sem_ref, *, sub):
    k = pl.program_id(0)
    n_valid = n_tiles_ref[0]

    def start(step, slot):
        base = tile_ids_ref[step] * sub

        def body(r, carry):
            tok = src_ref[base + r]
            pltpu.make_async_copy(x_hbm_ref.at[pl.ds(tok, 1)], buf_ref.at[slot, pl.ds(r, 1)],
                                  sem_ref.at[slot]).start()
            return carry
        lax.fori_loop(0, sub, body, 0, unroll=8)

    @pl.when(jnp.logical_and(k == 0, n_valid > 0))
    def _():
        start(0, 0)

    @pl.when(k + 1 < n_valid)
    def _():
        start(k + 1, (k + 1) % 2)

    @pl.when(k < n_valid)
    def _():
        slot = k % 2
        pltpu.make_async_copy(x_hbm_ref.at[pl.ds(0, sub)], buf_ref.at[slot], sem_ref.at[slot]).wait()
        o_ref[...] = _rms(buf_ref[slot], g_ref[...]).astype(BF16)


def _dispatch(x, g, tile_ids, n_tiles, src, n_slots, sub):
    m, d = x.shape
    max_tiles = tile_ids.shape[0]
    return pl.pallas_call(
        functools.partial(_dispatch_kernel, sub=sub),
        grid_spec=pltpu.PrefetchScalarGridSpec(
            num_scalar_prefetch=3,
            grid=(max_tiles,),
            in_specs=[pl.BlockSpec(memory_space=pl.ANY),
                      pl.BlockSpec((1, d), lambda k, ids, nt, src: (0, 0))],
            out_specs=pl.BlockSpec((sub, d), lambda k, ids, nt, src: (ids[k], 0)),
            scratch_shapes=[pltpu.VMEM((2, sub, d), F32), pltpu.SemaphoreType.DMA((2,))]),
        out_shape=jax.ShapeDtypeStruct((n_slots, d), BF16),
        compiler_params=_params("arbitrary"),
        name="moe_dispatch",
    )(tile_ids, n_tiles, src, x, g.reshape(1, d))


def _combine_kernel(pos_ref, x_ref, gate_ref, g_ref, ys_hbm_ref, ya_ref, yb_ref, buf_ref, sem_ref, *, tm, n_a):
    i = pl.program_id(0)
    n = pl.num_programs(0)

    def start(step, slot):
        base = step * (tm * TOP_K)

        def body(r, carry):
            for kk in range(TOP_K):
                p = pos_ref[base + r * TOP_K + kk]
                pltpu.make_async_copy(ys_hbm_ref.at[pl.ds(p, 1)], buf_ref.at[slot, kk, pl.ds(r, 1)],
                                      sem_ref.at[slot]).start()
            return carry
        lax.fori_loop(0, tm, body, 0, unroll=4)

    @pl.when(i == 0)
    def _():
        start(0, 0)

    @pl.when(i + 1 < n)
    def _():
        start(i + 1, (i + 1) % 2)

    slot = i % 2
    for kk in range(TOP_K):
        pltpu.make_async_copy(ys_hbm_ref.at[pl.ds(0, tm)], buf_ref.at[slot, kk], sem_ref.at[slot]).wait()
    y = x_ref[...]
    for kk in range(TOP_K):
        y = y + gate_ref[:, kk:kk + 1] * buf_ref[slot, kk]
    y = _rms(y, g_ref[...])

    @pl.when(i < n_a)
    def _():
        ya_ref[...] = y

    @pl.when(i >= n_a)
    def _():
        yb_ref[...] = y


def _combine(x, gates, pos, ys, g, tm, m_a):
    m, d = x.shape
    n_a = m_a // tm
    assert n_a * tm == m_a and m % tm == 0
    return pl.pallas_call(
        functools.partial(_combine_kernel, tm=tm, n_a=n_a),
        grid_spec=pltpu.PrefetchScalarGridSpec(
            num_scalar_prefetch=1,
            grid=(m // tm,),
            in_specs=[pl.BlockSpec((tm, d), lambda i, pos: (i, 0)),
                      pl.BlockSpec((tm, TOP_K), lambda i, pos: (i, 0)),
                      pl.BlockSpec((1, d), lambda i, pos: (0, 0)),
                      pl.BlockSpec(memory_space=pl.ANY)],
            out_specs=[pl.BlockSpec((tm, d), lambda i, pos: (jnp.minimum(i, n_a - 1), 0)),
                       pl.BlockSpec((tm, d), lambda i, pos: (jnp.maximum(i - n_a, 0), 0))],
            scratch_shapes=[pltpu.VMEM((2, TOP_K, tm, d), F32), pltpu.SemaphoreType.DMA((2,))]),
        out_shape=[jax.ShapeDtypeStruct((m_a, d), F32), jax.ShapeDtypeStruct((m - m_a, d), F32)],
        compiler_params=_params("arbitrary"),
        name="moe_combine",
    )(pos, x, gates, g.reshape(1, d), ys)


def _moe_plan(top_i, n_exp, blk, sub, n_blocks, max_tiles):
    m = top_i.shape[0]
    e_flat = top_i.reshape(-1)
    onehot = (e_flat[:, None] == jnp.arange(n_exp, dtype=I32)[None, :]).astype(I32)
    csum = jnp.cumsum(onehot, axis=0)
    rank = jnp.sum((csum - 1) * onehot, axis=1)
    counts = csum[-1]
    blocks_per = (counts + blk - 1) // blk
    block_end = jnp.cumsum(blocks_per)
    block_start = block_end - blocks_per
    n_used = block_end[-1]
    pos = block_start[e_flat] * blk + rank

    s = jnp.arange(n_blocks, dtype=I32)
    bexp = jnp.minimum(jnp.sum((s[:, None] >= block_end[None, :]).astype(I32), axis=1), n_exp - 1)
    valid = jnp.clip(counts[bexp] - (s - block_start[bexp]) * blk, 0, blk)
    valid = jnp.where(s < n_used, valid, 0)

    by_expert = jnp.argsort(e_flat, stable=True).astype(I32) // TOP_K
    by_expert = jnp.concatenate([by_expert, jnp.zeros((m,), I32)])
    first_pair = jnp.cumsum(counts) - counts
    src = jnp.zeros((n_blocks * blk + m,), I32)
    for e in range(n_exp):
        run = lax.dynamic_slice(by_expert, (first_pair[e],), (m,))
        src = lax.dynamic_update_slice(src, run, (block_start[e] * blk,))
    src = src[:n_blocks * blk]
    nsub = (valid + sub - 1) // sub
    last_exp = bexp[jnp.maximum(n_used - 1, 0)]
    bexp = jnp.where(s < n_used, bexp, last_exp)

    per_blk = blk // sub
    t = jnp.arange(n_blocks * per_blk, dtype=I32)
    tile_valid = (t % per_blk) < nsub[t // per_blk]
    n_tiles = jnp.sum(tile_valid.astype(I32))
    order = jnp.argsort(jnp.logical_not(tile_valid), stable=True).astype(I32)[:max_tiles]
    last_tile = order[jnp.maximum(n_tiles - 1, 0)]
    tile_ids = jnp.where(jnp.arange(max_tiles, dtype=I32) < n_tiles, order, last_tile)
    return pos, src, bexp, nsub, n_used.reshape(1), tile_ids, n_tiles.reshape(1)


def _moe(x, top_i, gates, g_ffn, g_final, w1, w3, w2, blk, sub, tf, tm_c, m_a):
    m, d = x.shape
    n_exp = w1.shape[0]
    n_blocks = (m * TOP_K + n_exp * (blk - 1)) // blk
    max_tiles = (m * TOP_K + n_exp * (sub - 1)) // sub
    pos, src, bexp, nsub, nused, tile_ids, n_tiles = _moe_plan(top_i, n_exp, blk, sub, n_blocks, max_tiles)
    xs = _dispatch(x, g_ffn, tile_ids, n_tiles, src, n_blocks * blk, sub)
    n_main = min(n_blocks, n_exp * (-(-(m * TOP_K) // (n_exp * blk))) + 2)
    ys = _experts(xs, w1, w3, w2, bexp, nsub, nused, blk, sub, tf, 0, n_main)
    if n_main < n_blocks:
        ys = lax.cond(
            nused[0] > n_main,
            lambda y: _experts(xs, w1, w3, w2, bexp, nsub, nused, blk, sub, tf, n_main, n_blocks - n_main, ys=y),
            lambda y: y, ys)
    return _combine(x, gates, pos, ys, g_final, tm_c, m_a)


def kernel(x_prompt, x_sample, state_conv_a, state_conv_c, state_pool_d, norm_mix_g, norm_ffn_g, norm_final_g, w_in_ab, conv_a, ln_b_g, ln_b_b, w_s, b_s, w_out_ab, ffn_w1, ffn_w3, ffn_w2, w_in_cd, conv_c, conv_c_b, ln_c_g, ln_c_b, pool_w, pool_scale, w_out_cd, router, exp_w1, exp_w3, exp_w2):
    n_seq, seq_len, d = x_prompt.shape
    nseq_s, dl, _ = x_sample.shape
    mp, ms = n_seq * seq_len, nseq_s * dl
    m = mp + ms
    assert norm_mix_g.shape[0] == 2, "two layers: conv/chunk-MLP + dense FFN, then conformer/pool + MoE"
    assert seq_len % CHUNK == 0 and dl <= CHUNK and mp % ms == 0

    big = m * TOP_K >= 4 * 2560
    tm_in = _pick(m, (2176, 1088, 544, 272))
    tm_mid = _pick(m, (1088, 544, 272))
    tm_row = _pick(ms, (512, 256, 128, 64))
    t_rows = _pick(seq_len, (256, 128))
    tn = 512
    tf_up = _pick(ffn_w1.shape[2], (512, 256))
    tn_down = 256
    tf_e = 256
    blk_e, sub_e = (2560, 256) if big else (512, 128)
    tm_c = _pick(ms, (256, 128, 64))

    xp = x_prompt.reshape(mp, d)
    xs_lm = jnp.transpose(x_sample, (1, 0, 2)).reshape(ms, d)

    h = _norm_in(xp, xs_lm, norm_mix_g[0], tm_row)
    proj = _matmul(h, w_in_ab[0], tm_in, tn)
    mixed, a_p, v_p = _mix_ab_prompt(proj, conv_a[0], ln_b_g[0], ln_b_b[0], w_s[0], b_s[0],
                                     n_seq, seq_len, t_rows)
    mixed, a_s, v_s = _mix_ab_sample(mixed, proj, jnp.transpose(state_conv_a[0], (1, 0, 2)),
                                     conv_a[0], ln_b_g[0], ln_b_b[0], w_s[0], b_s[0], mp, dl, nseq_s)
    x, h = _out_proj(mixed, w_out_ab[0].astype(BF16), xp, xs_lm, norm_ffn_g[0], tm_row)
    act = _ffn_up(h, ffn_w1, ffn_w3, tm_in, tf_up)
    x = _ffn_down(act, ffn_w2, x, tm_mid, tn_down)

    h = _norm_rows(x, norm_mix_g[1], tm_mid)
    proj = _matmul(h, w_in_cd[0], tm_in, tn)
    mixed, c_p, d_p = _mix_cd_prompt(proj, conv_c[0], conv_c_b[0], ln_c_g[0], ln_c_b[0], pool_w[0],
                                     pool_scale[0], n_seq, seq_len, t_rows)
    mixed, g_s = _mix_cd_sample(mixed, proj, jnp.transpose(state_conv_c[0], (1, 0, 2)),
                                jnp.transpose(state_pool_d[0], (1, 0, 2)), conv_c[0], conv_c_b[0],
                                ln_c_g[0], ln_c_b[0], pool_w[0], pool_scale[0], mp, dl, nseq_s)
    (x,) = _out_proj(mixed, w_out_cd[0].astype(BF16), x, x, None, tm_row)
    top_i, gates = _router(x, norm_ffn_g[1], router[0], tm_mid)
    y_p, y_s = _moe(x, top_i, gates, norm_ffn_g[1], norm_final_g, exp_w1[0], exp_w3[0], exp_w2[0],
                    blk_e, sub_e, tf_e, tm_c, mp)

    dc = conv_c.shape[2]
    y_prompt = y_p.reshape(n_seq, seq_len, d)
    y_sample = jnp.transpose(y_s.reshape(dl, nseq_s, d), (1, 0, 2))
    new_a_s = jnp.transpose(a_s, (1, 0, 2))
    new_v_s = jnp.transpose(v_s.reshape(dl, nseq_s, -1), (1, 0, 2))
    g_rows = jnp.transpose(g_s.reshape(dl, nseq_s, dc), (1, 0, 2))
    p_rows = jnp.transpose(proj[mp:, 2 * dc:].astype(F32).reshape(dl, nseq_s, -1), (1, 0, 2))
    n_c = state_conv_c.shape[2]
    n_d = state_pool_d.shape[2]
    new_c_s = jnp.concatenate([state_conv_c[0], g_rows], axis=1)[:, -n_c:]
    new_d_s = jnp.concatenate([state_pool_d[0], p_rows], axis=1)[:, -n_d:]
    return (y_prompt, y_sample,
            a_p[None], new_a_s[None],
            v_p[None], new_v_s[None],
            c_p[None], new_c_s[None],
            d_p[None], new_d_s[None])
```

```python
import functools

import jax
import jax.numpy as jnp
from jax import lax
from jax.experimental import pallas as pl
from jax.experimental.pallas import tpu as pltpu

F32 = jnp.float32
BF16 = jnp.bfloat16
I32 = jnp.int32

EPS = 1e-6
PAST_LEN = 16384
CHUNK = 128
HEAD_B = 128
POOL_WINDOWS = (2, 4, 8, 16)
TOP_K = 2

VMEM_LIMIT_BYTES = 60 * 1024 * 1024
SUBLANES = 8
LANES = 128


def _params(*sem):
    return pltpu.CompilerParams(dimension_semantics=sem, vmem_limit_bytes=VMEM_LIMIT_BYTES)


def _pick(n, candidates):
    for c in candidates:
        if n % c == 0:
            return c
    raise ValueError(f"no tile in {candidates} divides {n}")


def _rms(x, g):
    return x * lax.rsqrt(jnp.mean(x * x, axis=-1, keepdims=True) + EPS) * g


def _layernorm(x, g, b):
    mu = jnp.mean(x, axis=-1, keepdims=True)
    xc = x - mu
    return xc * lax.rsqrt(jnp.mean(xc * xc, axis=-1, keepdims=True) + EPS) * g + b


def _silu(x):
    return x * jax.nn.sigmoid(x)


def _dot(a, b):
    return jnp.dot(a, b, preferred_element_type=F32)


def _f32(x):
    return x.astype(F32)


def _mix_ab_prompt_kernel(bg_ref, cg_ref, xin_ref, u_ref, v_ref, cgh_ref, xinh_ref,
                          cw_ref, lng_ref, lnb_ref, ws_ref, bsb_ref,
                          o_ref, astate_ref, vstate_ref, abuf_ref):
    t = pl.program_id(1)
    t_rows, da = bg_ref.shape
    n_heads = ws_ref.shape[0]
    halo = cgh_ref.shape[0]

    a = _f32(cg_ref[...]) * _f32(xin_ref[...])
    prev = jnp.where(t == 0, 0.0, _f32(cgh_ref[...]) * _f32(xinh_ref[...]))
    abuf_ref[0:halo, :] = prev
    abuf_ref[halo:halo + t_rows, :] = a
    conv = (cw_ref[0:1, :] * abuf_ref[halo - 2:halo - 2 + t_rows, :]
            + cw_ref[1:2, :] * abuf_ref[halo - 1:halo - 1 + t_rows, :]
            + cw_ref[2:3, :] * a)
    o_ref[:, 0:da] = (_f32(bg_ref[...]) * conv).astype(BF16)
    astate_ref[0] = abuf_ref[halo + t_rows - 2:halo + t_rows, :]

    row = lax.broadcasted_iota(I32, (CHUNK, CHUNK), 0)
    col = lax.broadcasted_iota(I32, (CHUNK, CHUNK), 1)
    lower = row >= col
    for h in range(n_heads):
        cols = slice(h * HEAD_B, (h + 1) * HEAD_B)
        vn = _layernorm(jax.nn.gelu(_f32(v_ref[:, cols])), lng_ref[:, cols], lnb_ref[:, cols])
        vstate_ref[0, :, cols] = vn[t_rows - CHUNK:t_rows]
        w = jnp.where(lower, ws_ref[h], 0.0).astype(BF16)
        for c in range(t_rows // CHUNK):
            rows = slice(c * CHUNK, (c + 1) * CHUNK)
            mix = _dot(w, vn[rows].astype(BF16)) + bsb_ref[h]
            o_ref[rows, da + h * HEAD_B:da + (h + 1) * HEAD_B] = (
                jax.nn.gelu(_f32(u_ref[rows, cols])) * mix).astype(BF16)


def _mix_ab_prompt(proj, conv_a, ln_g, ln_b, w_s, b_s, n_seq, seq_len, t_rows):
    m = proj.shape[0]
    da = conv_a.shape[1]
    db = ln_g.shape[0]
    n_heads = w_s.shape[0]
    tiles = seq_len // t_rows
    halo_rows = 2 * SUBLANES
    halo_blocks = t_rows // halo_rows

    def col_spec(c):
        return pl.BlockSpec((t_rows, da), lambda b, t: (b * tiles + t, c))

    def halo_spec(c):
        return pl.BlockSpec(
            (halo_rows, da),
            lambda b, t: (jnp.maximum((b * tiles + t) * halo_blocks - 1, 0), c))

    def full(shape):
        return pl.BlockSpec(shape, lambda b, t: (0,) * len(shape))

    bsb = jnp.broadcast_to(b_s[:, :CHUNK, None], (n_heads, CHUNK, HEAD_B))
    return pl.pallas_call(
        _mix_ab_prompt_kernel,
        grid=(n_seq, tiles),
        in_specs=[col_spec(0), col_spec(1), col_spec(2), col_spec(3), col_spec(4),
                  halo_spec(1), halo_spec(2),
                  full((conv_a.shape[0], da)), full((1, db)), full((1, db)),
                  full((n_heads, CHUNK, CHUNK)), full((n_heads, CHUNK, HEAD_B))],
        out_specs=[pl.BlockSpec((t_rows, da + db), lambda b, t: (b * tiles + t, 0)),
                   pl.BlockSpec((1, 2, da), lambda b, t: (b, 0, 0)),
                   pl.BlockSpec((1, CHUNK, db), lambda b, t: (b, 0, 0))],
        out_shape=[jax.ShapeDtypeStruct((m, da + db), BF16),
                   jax.ShapeDtypeStruct((n_seq, 2, da), F32),
                   jax.ShapeDtypeStruct((n_seq, CHUNK, db), F32)],
        scratch_shapes=[pltpu.VMEM((halo_rows + t_rows, da), F32)],
        compiler_params=_params("parallel", "arbitrary"),
        name="mix_ab_prompt",
    )(proj, proj, proj, proj, proj, proj, proj, conv_a, ln_g.reshape(1, db), ln_b.reshape(1, db),
      w_s[:, :CHUNK, :CHUNK], bsb)


def _mix_ab_sample_kernel(mixed_hbm_ref, bg_ref, cg_ref, xin_ref, u_ref, v_ref, st_ref,
                          cw_ref, lng_ref, lnb_ref, wsl_ref, bsl_ref,
                          o_ref, astate_ref, vstate_ref, *, dl, nseq):
    del mixed_hbm_ref
    da = bg_ref.shape[1]
    n_heads = lng_ref.shape[1] // HEAD_B
    n_state = st_ref.shape[0]

    def slab(l):
        return slice(l * nseq, (l + 1) * nseq)

    a_pad = [st_ref[k] for k in range(n_state)]
    a_pad += [_f32(cg_ref[slab(l), :]) * _f32(xin_ref[slab(l), :]) for l in range(dl)]
    for l in range(dl):
        conv = (cw_ref[0:1, :] * a_pad[l] + cw_ref[1:2, :] * a_pad[l + 1]
                + cw_ref[2:3, :] * a_pad[l + 2])
        o_ref[slab(l), 0:da] = (_f32(bg_ref[slab(l), :]) * conv).astype(BF16)
    for k in range(n_state):
        astate_ref[k] = a_pad[dl + k]

    for h in range(n_heads):
        cols = slice(h * HEAD_B, (h + 1) * HEAD_B)
        vn = _layernorm(jax.nn.gelu(_f32(v_ref[:, cols])), lng_ref[:, cols], lnb_ref[:, cols])
        vstate_ref[:, cols] = vn
        for i in range(dl):
            mix = bsl_ref[i:i + 1, cols]
            for j in range(i + 1):
                mix = mix + wsl_ref[i * dl + j:i * dl + j + 1, cols] * vn[slab(j)]
            o_ref[slab(i), da + h * HEAD_B:da + (h + 1) * HEAD_B] = (
                jax.nn.gelu(_f32(u_ref[slab(i), cols])) * mix).astype(BF16)


def _mix_ab_sample(mixed, proj, state_lm, conv_a, ln_g, ln_b, w_s, b_s, row0, dl, nseq):
    ms = dl * nseq
    da = conv_a.shape[1]
    db = ln_g.shape[0]
    n_heads = w_s.shape[0]
    rb = row0 // ms

    def col_spec(c):
        return pl.BlockSpec((ms, da), lambda i: (rb, c))

    def full(shape):
        return pl.BlockSpec(shape, lambda i: (0,) * len(shape))

    wsl = jnp.repeat(jnp.transpose(w_s[:, :dl, :dl], (1, 2, 0)).reshape(dl * dl, n_heads), HEAD_B, axis=1)
    bsl = jnp.repeat(jnp.transpose(b_s[:, :dl], (1, 0)), HEAD_B, axis=1)
    return pl.pallas_call(
        functools.partial(_mix_ab_sample_kernel, dl=dl, nseq=nseq),
        grid=(1,),
        in_specs=[pl.BlockSpec(memory_space=pl.ANY),
                  col_spec(0), col_spec(1), col_spec(2), col_spec(3), col_spec(4),
                  full(state_lm.shape), full((conv_a.shape[0], da)), full((1, db)), full((1, db)),
                  full((dl * dl, db)), full((dl, db))],
        out_specs=[pl.BlockSpec((ms, da + db), lambda i: (rb, 0)),
                   full(state_lm.shape),
                   full((ms, db))],
        out_shape=[jax.ShapeDtypeStruct(mixed.shape, BF16),
                   jax.ShapeDtypeStruct(state_lm.shape, F32),
                   jax.ShapeDtypeStruct((ms, db), F32)],
        input_output_aliases={0: 0},
        compiler_params=_params("arbitrary"),
        name="mix_ab_sample",
    )(mixed, proj, proj, proj, proj, proj, state_lm, conv_a, ln_g.reshape(1, db), ln_b.reshape(1, db),
      wsl, bsl)


def _mix_cd_prompt_kernel(ga_ref, gb_ref, p_ref, gah_ref, gbh_ref, ph_ref,
                          cw_ref, cb_ref, lng_ref, lnb_ref, pw_ref, ps_ref,
                          o_ref, cstate_ref, dstate_ref, gbuf_ref, pbuf_ref, cbuf_ref, shift_ref):
    t = pl.program_id(1)
    t_rows, dc = ga_ref.shape
    k_c = cw_ref.shape[0]
    c_halo = gah_ref.shape[0]
    d_halo = ph_ref.shape[0]
    gd = pw_ref.shape[1]

    g = _f32(ga_ref[...]) * jax.nn.sigmoid(_f32(gb_ref[...]))
    gbuf_ref[0:c_halo, :] = jnp.where(
        t == 0, 0.0, _f32(gah_ref[...]) * jax.nn.sigmoid(_f32(gbh_ref[...])))
    gbuf_ref[c_halo:c_halo + t_rows, :] = g
    n_shift = shift_ref.shape[1]
    for r in range(1, SUBLANES):
        shift_ref[r - 1] = gbuf_ref[r:r + n_shift, :]
    first = c_halo - (k_c - 1)
    for cblk in range(dc // LANES):
        cols = slice(cblk * LANES, (cblk + 1) * LANES)
        acc = cb_ref[:, cols]
        for k in range(k_c):
            q, r = divmod(first + k, SUBLANES)
            if r == 0:
                win = gbuf_ref[first + k:first + k + t_rows, cols]
            else:
                win = shift_ref[r - 1, q * SUBLANES:q * SUBLANES + t_rows, cols]
            acc = acc + cw_ref[k:k + 1, cols] * win
        cbuf_ref[:, cols] = acc
    o_ref[:, 0:dc] = _silu(_layernorm(cbuf_ref[...], lng_ref[...], lnb_ref[...])).astype(BF16)
    cstate_ref[0] = gbuf_ref[c_halo + t_rows - (k_c - 1):c_halo + t_rows, :]

    pbuf_ref[0:d_halo, :] = jnp.where(t == 0, 0.0, _f32(ph_ref[...]))
    pbuf_ref[d_halo:d_halo + t_rows, :] = _f32(p_ref[...])
    pos = t * t_rows + lax.broadcasted_iota(I32, (t_rows, 1), 0)
    for gi, w in enumerate(POOL_WINDOWS):
        cols = slice(gi * gd, (gi + 1) * gd)
        cur = pbuf_ref[d_halo:d_halo + t_rows, cols]
        s = cur
        for k in range(1, w):
            s = s + pbuf_ref[d_halo - k:d_halo - k + t_rows, cols]
        cnt = jnp.minimum(w, pos + 1).astype(F32)
        diff = (s / cnt - cur).astype(BF16)
        d = _dot(diff, pw_ref[gi].astype(BF16)) * ps_ref[:, cols]
        o_ref[:, dc + gi * gd:dc + (gi + 1) * gd] = d.astype(BF16)
    n_keep = dstate_ref.shape[1]
    dstate_ref[0] = pbuf_ref[d_halo + t_rows - n_keep:d_halo + t_rows, :]


def _mix_cd_prompt(proj, conv_c, conv_c_b, ln_g, ln_b, pool_w, pool_scale, n_seq, seq_len, t_rows):
    m = proj.shape[0]
    k_c, dc = conv_c.shape
    dd = pool_scale.shape[0]
    n_pool, gd = pool_w.shape[0], pool_w.shape[1]
    tiles = seq_len // t_rows
    c_halo = -(-(k_c - 1) // SUBLANES) * SUBLANES
    d_halo = -(-(max(POOL_WINDOWS) - 1) // SUBLANES) * SUBLANES
    assert t_rows % c_halo == 0 and t_rows % d_halo == 0 and dc == dd

    def col_spec(c):
        return pl.BlockSpec((t_rows, dc), lambda b, t: (b * tiles + t, c))

    def halo_spec(c, rows):
        per = t_rows // rows
        return pl.BlockSpec((rows, dc),
                            lambda b, t: (jnp.maximum((b * tiles + t) * per - 1, 0), c))

    def full(shape):
        return pl.BlockSpec(shape, lambda b, t: (0,) * len(shape))

    return pl.pallas_call(
        _mix_cd_prompt_kernel,
        grid=(n_seq, tiles),
        in_specs=[col_spec(0), col_spec(1), col_spec(2),
                  halo_spec(0, c_halo), halo_spec(1, c_halo), halo_spec(2, d_halo),
                  full((k_c, dc)), full((1, dc)), full((1, dc)), full((1, dc)),
                  full((n_pool, gd, gd)), full((1, dd))],
        out_specs=[pl.BlockSpec((t_rows, dc + dd), lambda b, t: (b * tiles + t, 0)),
                   pl.BlockSpec((1, k_c - 1, dc), lambda b, t: (b, 0, 0)),
                   pl.BlockSpec((1, max(POOL_WINDOWS) - 1, dd), lambda b, t: (b, 0, 0))],
        out_shape=[jax.ShapeDtypeStruct((m, dc + dd), BF16),
                   jax.ShapeDtypeStruct((n_seq, k_c - 1, dc), F32),
                   jax.ShapeDtypeStruct((n_seq, max(POOL_WINDOWS) - 1, dd), F32)],
        scratch_shapes=[pltpu.VMEM((c_halo + t_rows, dc), F32),
                        pltpu.VMEM((d_halo + t_rows, dd), F32),
                        pltpu.VMEM((t_rows, dc), F32),
                        pltpu.VMEM((SUBLANES - 1, c_halo + t_rows - SUBLANES, dc), F32)],
        compiler_params=_params("parallel", "arbitrary"),
        name="mix_cd_prompt",
    )(proj, proj, proj, proj, proj, proj, conv_c, conv_c_b.reshape(1, dc), ln_g.reshape(1, dc),
      ln_b.reshape(1, dc), pool_w, pool_scale.reshape(1, dd))


def _mix_cd_sample_kernel(mixed_hbm_ref, ga_ref, gb_ref, p_ref, stc_ref, std_ref,
                          cw_ref, cb_ref, lng_ref, lnb_ref, pw_ref, ps_ref,
                          o_ref, g_ref, cbuf_ref, dbuf_ref, *, dl, nseq, start_pos):
    del mixed_hbm_ref
    gi = pl.program_id(0)
    n_groups = pl.num_programs(0)
    gd = ga_ref.shape[1]
    k_c = cw_ref.shape[0]
    n_c = stc_ref.shape[0]
    n_d = std_ref.shape[0]

    def slab(l):
        return slice(l * nseq, (l + 1) * nseq)

    g = _f32(ga_ref[...]) * jax.nn.sigmoid(_f32(gb_ref[...]))
    g_ref[...] = g
    c_pad = [stc_ref[k] for k in range(n_c)] + [g[slab(l)] for l in range(dl)]
    d_pad = [std_ref[k] for k in range(n_d)] + [_f32(p_ref[slab(l), :]) for l in range(dl)]
    for l in range(dl):
        acc = cb_ref[...] + cw_ref[0:1, :] * c_pad[l]
        for k in range(1, k_c):
            acc = acc + cw_ref[k:k + 1, :] * c_pad[l + k]
        cbuf_ref[gi, slab(l), :] = acc

        cur = d_pad[n_d + l]
        s = cur
        pooled = jnp.zeros_like(cur)
        for k in range(1, max(POOL_WINDOWS) + 1):
            if k in POOL_WINDOWS:
                cnt = float(min(k, start_pos + l + 1))
                pooled = jnp.where(gi == POOL_WINDOWS.index(k), s / cnt, pooled)
            if k < max(POOL_WINDOWS):
                s = s + d_pad[n_d + l - k]
        diff = (pooled - cur).astype(BF16)
        dbuf_ref[gi, slab(l), :] = (_dot(diff, pw_ref[0].astype(BF16)) * ps_ref[...]).astype(BF16)

    @pl.when(gi == n_groups - 1)
    def _():
        n_blk = cbuf_ref.shape[0]
        dc = n_blk * gd
        total = cbuf_ref[0].sum(axis=-1, keepdims=True)
        for b in range(1, n_blk):
            total = total + cbuf_ref[b].sum(axis=-1, keepdims=True)
        mu = total / dc
        var = None
        for b in range(n_blk):
            xc = cbuf_ref[b] - mu
            sq = (xc * xc).sum(axis=-1, keepdims=True)
            var = sq if var is None else var + sq
        rstd = lax.rsqrt(var / dc + EPS)
        for b in range(n_blk):
            cols = slice(b * gd, (b + 1) * gd)
            y = (cbuf_ref[b] - mu) * rstd * lng_ref[:, cols] + lnb_ref[:, cols]
            o_ref[:, cols] = _silu(y).astype(BF16)
            o_ref[:, dc + b * gd:dc + (b + 1) * gd] = dbuf_ref[b]


def _mix_cd_sample(mixed, proj, stc_lm, std_lm, conv_c, conv_c_b, ln_g, ln_b, pool_w, pool_scale,
                   row0, dl, nseq):
    ms = dl * nseq
    k_c, dc = conv_c.shape
    dd = pool_scale.shape[0]
    n_pool, gd = pool_w.shape[0], pool_w.shape[1]
    assert dc == dd == n_pool * gd and n_pool == len(POOL_WINDOWS)
    rb = row0 // ms

    return pl.pallas_call(
        functools.partial(_mix_cd_sample_kernel, dl=dl, nseq=nseq, start_pos=PAST_LEN),
        grid=(n_pool,),
        in_specs=[pl.BlockSpec(memory_space=pl.ANY),
                  pl.BlockSpec((ms, gd), lambda i: (rb, i)),
                  pl.BlockSpec((ms, gd), lambda i: (rb, n_pool + i)),
                  pl.BlockSpec((ms, gd), lambda i: (rb, 2 * n_pool + i)),
                  pl.BlockSpec((stc_lm.shape[0], nseq, gd), lambda i: (0, 0, i)),
                  pl.BlockSpec((std_lm.shape[0], nseq, gd), lambda i: (0, 0, i)),
                  pl.BlockSpec((k_c, gd), lambda i: (0, i)),
                  pl.BlockSpec((1, gd), lambda i: (0, i)),
                  pl.BlockSpec((1, dc), lambda i: (0, 0)),
                  pl.BlockSpec((1, dc), lambda i: (0, 0)),
                  pl.BlockSpec((1, gd, gd), lambda i: (i, 0, 0)),
                  pl.BlockSpec((1, gd), lambda i: (0, i))],
        out_specs=[pl.BlockSpec((ms, dc + dd), lambda i: (rb, 0)),
                   pl.BlockSpec((ms, gd), lambda i: (0, i))],
        out_shape=[jax.ShapeDtypeStruct(mixed.shape, BF16),
                   jax.ShapeDtypeStruct((ms, dc), F32)],
        scratch_shapes=[pltpu.VMEM((n_pool, ms, gd), F32),
                        pltpu.VMEM((n_pool, ms, gd), BF16)],
        input_output_aliases={0: 0},
        compiler_params=_params("arbitrary"),
        name="mix_cd_sample",
    )(mixed, proj, proj, proj, stc_lm, std_lm, conv_c, conv_c_b.reshape(1, dc), ln_g.reshape(1, dc),
      ln_b.reshape(1, dc), pool_w, pool_scale.reshape(1, dd))


def _norm_in_kernel(xp_ref, xs_ref, g_ref, o_ref, *, n_a):
    i = pl.program_id(0)

    @pl.when(i < n_a)
    def _():
        o_ref[...] = _rms(xp_ref[...], g_ref[...]).astype(BF16)

    @pl.when(i >= n_a)
    def _():
        o_ref[...] = _rms(xs_ref[...], g_ref[...]).astype(BF16)


def _norm_in(xp, xs, g, tm):
    mp, d = xp.shape
    ms = xs.shape[0]
    n_a, n_b = mp // tm, ms // tm
    return pl.pallas_call(
        functools.partial(_norm_in_kernel, n_a=n_a),
        grid=(n_a + n_b,),
        in_specs=[pl.BlockSpec((tm, d), lambda i: (jnp.minimum(i, n_a - 1), 0)),
                  pl.BlockSpec((tm, d), lambda i: (jnp.maximum(i - n_a, 0), 0)),
                  pl.BlockSpec((1, d), lambda i: (0, 0))],
        out_specs=pl.BlockSpec((tm, d), lambda i: (i, 0)),
        out_shape=jax.ShapeDtypeStruct((mp + ms, d), BF16),
        compiler_params=_params("arbitrary"),
        name="norm_in",
    )(xp, xs, g.reshape(1, d))


def _matmul_kernel(a_ref, w_ref, o_ref):
    o_ref[...] = _dot(a_ref[...], w_ref[...].astype(BF16)).astype(o_ref.dtype)


def _matmul(a, w, tm, tn):
    m, k = a.shape
    n = w.shape[1]
    return pl.pallas_call(
        _matmul_kernel,
        grid=(m // tm, n // tn),
        in_specs=[pl.BlockSpec((tm, k), lambda i, j: (i, 0)),
                  pl.BlockSpec((k, tn), lambda i, j: (0, j))],
        out_specs=pl.BlockSpec((tm, tn), lambda i, j: (i, j)),
        out_shape=jax.ShapeDtypeStruct((m, n), BF16),
        compiler_params=_params("parallel", "arbitrary"),
        name="in_proj",
    )(a, w)


def _ffn_up_kernel(h_ref, w1_ref, w3_ref, o_ref):
    w1 = w1_ref[0].astype(BF16)
    w3 = w3_ref[0].astype(BF16)
    tm = h_ref.shape[0]
    n_parts = 2 if tm % (2 * 2 * SUBLANES) == 0 else 1
    rows_per = tm // n_parts
    for part in range(n_parts):
        rows = slice(part * rows_per, (part + 1) * rows_per)
        h = h_ref[rows, :]
        o_ref[rows, :] = (_silu(_dot(h, w1)) * _dot(h, w3)).astype(BF16)


def _ffn_up(h, w1, w3, tm, tf):
    m, d = h.shape
    f = w1.shape[2]
    return pl.pallas_call(
        _ffn_up_kernel,
        grid=(m // tm, f // tf),
        in_specs=[pl.BlockSpec((tm, d), lambda i, j: (i, 0)),
                  pl.BlockSpec((1, d, tf), lambda i, j: (0, 0, j)),
                  pl.BlockSpec((1, d, tf), lambda i, j: (0, 0, j))],
        out_specs=pl.BlockSpec((tm, tf), lambda i, j: (i, j)),
        out_shape=jax.ShapeDtypeStruct((m, f), BF16),
        compiler_params=_params("parallel", "arbitrary"),
        name="ffn_up",
    )(h, w1, w3)


def _ffn_down_kernel(a_ref, w_ref, r_ref, o_ref):
    o_ref[...] = r_ref[...] + _dot(a_ref[...], w_ref[0].astype(BF16))


def _ffn_down(act, w2, res, tm, tn):
    m, f = act.shape
    d = w2.shape[2]
    return pl.pallas_call(
        _ffn_down_kernel,
        grid=(m // tm, d // tn),
        in_specs=[pl.BlockSpec((tm, f), lambda i, j: (i, 0)),
                  pl.BlockSpec((1, f, tn), lambda i, j: (0, 0, j)),
                  pl.BlockSpec((tm, tn), lambda i, j: (i, j))],
        out_specs=pl.BlockSpec((tm, tn), lambda i, j: (i, j)),
        out_shape=jax.ShapeDtypeStruct((m, d), F32),
        compiler_params=_params("parallel", "arbitrary"),
        name="ffn_down",
    )(act, w2, res)


def _norm_rows_kernel(x_ref, g_ref, o_ref):
    o_ref[...] = _rms(x_ref[...], g_ref[...]).astype(BF16)


def _norm_rows(x, g, tm):
    m, d = x.shape
    return pl.pallas_call(
        _norm_rows_kernel,
        grid=(m // tm,),
        in_specs=[pl.BlockSpec((tm, d), lambda i: (i, 0)),
                  pl.BlockSpec((1, d), lambda i: (0, 0))],
        out_specs=pl.BlockSpec((tm, d), lambda i: (i, 0)),
        out_shape=jax.ShapeDtypeStruct((m, d), BF16),
        compiler_params=_params("parallel"),
        name="norm_rows",
    )(x, g.reshape(1, d))


def _out_proj_kernel(a_ref, w_ref, ra_ref, rb_ref, *rest, n_a, with_h):
    i = pl.program_id(0)
    if with_h:
        g_ref, x_ref, h_ref = rest
    else:
        (x_ref,) = rest
    tm = a_ref.shape[0]
    n_parts = 2 if tm % (2 * 2 * SUBLANES) == 0 else 1
    rows_per = tm // n_parts
    for part in range(n_parts):
        rows = slice(part * rows_per, (part + 1) * rows_per)
        res = jnp.where(i < n_a, ra_ref[rows, :], rb_ref[rows, :])
        x = res + _dot(a_ref[rows, :], w_ref[...])
        x_ref[rows, :] = x
        if with_h:
            h_ref[rows, :] = _rms(x, g_ref[...]).astype(BF16)


def _out_proj(a, w_bf16, res_a, res_b, g, tm):
    m, k = a.shape
    d = w_bf16.shape[1]
    n_a = res_a.shape[0] // tm
    n_b = m // tm - n_a
    assert n_a * tm == res_a.shape[0] and (n_b == 0 or n_b * tm == res_b.shape[0])
    with_h = g is not None
    in_specs = [pl.BlockSpec((tm, k), lambda i: (i, 0)),
                pl.BlockSpec((k, d), lambda i: (0, 0), pipeline_mode=pl.Buffered(1)),
                pl.BlockSpec((tm, d), lambda i: (jnp.minimum(i, n_a - 1), 0)),
                pl.BlockSpec((tm, d), lambda i: (jnp.maximum(i - n_a, 0), 0))]
    args = [a, w_bf16, res_a, res_b]
    row_spec = pl.BlockSpec((tm, d), lambda i: (i, 0))
    out_specs = [row_spec]
    out_shape = [jax.ShapeDtypeStruct((m, d), F32)]
    if with_h:
        in_specs.append(pl.BlockSpec((1, d), lambda i: (0, 0)))
        args.append(g.reshape(1, d))
        out_specs.append(row_spec)
        out_shape.append(jax.ShapeDtypeStruct((m, d), BF16))
    return pl.pallas_call(
        functools.partial(_out_proj_kernel, n_a=n_a, with_h=with_h),
        grid=(m // tm,),
        in_specs=in_specs, out_specs=out_specs, out_shape=out_shape,
        compiler_params=_params("arbitrary"),
        name="out_proj_h" if with_h else "out_proj",
    )(*args)


def _router_kernel(x_ref, g_ref, r_ref, idx_ref, gate_ref):
    h = _rms(x_ref[...], g_ref[...])
    h_hi = h.astype(BF16)
    h_lo = (h - h_hi.astype(F32)).astype(BF16)
    r = r_ref[...]
    r_hi = r.astype(BF16)
    r_lo = (r - r_hi.astype(F32)).astype(BF16)
    logits = _dot(h_hi, r_hi) + (_dot(h_hi, r_lo) + _dot(h_lo, r_hi))
    n_exp = float(logits.shape[1])
    lane = lax.broadcasted_iota(I32, logits.shape, 1).astype(F32)
    m1 = jnp.max(logits, axis=-1, keepdims=True)
    i1 = jnp.min(jnp.where(logits == m1, lane, n_exp), axis=-1, keepdims=True)
    rest = jnp.where(lane == i1, -jnp.inf, logits)
    m2 = jnp.max(rest, axis=-1, keepdims=True)
    i2 = jnp.min(jnp.where(rest == m2, lane, n_exp), axis=-1, keepdims=True)
    e = jnp.exp(m2 - m1)
    two = lax.broadcasted_iota(I32, idx_ref.shape, 1)
    idx_ref[...] = jnp.where(two == 0, i1, i2).astype(I32)
    gate_ref[...] = jnp.where(two == 0, 1.0, e) / (1.0 + e)


def _router(x, g, router, tm):
    m, d = x.shape
    n_exp = router.shape[1]
    return pl.pallas_call(
        _router_kernel,
        grid=(m // tm,),
        in_specs=[pl.BlockSpec((tm, d), lambda i: (i, 0)),
                  pl.BlockSpec((1, d), lambda i: (0, 0)),
                  pl.BlockSpec((d, n_exp), lambda i: (0, 0))],
        out_specs=[pl.BlockSpec((tm, TOP_K), lambda i: (i, 0)),
                   pl.BlockSpec((tm, TOP_K), lambda i: (i, 0))],
        out_shape=[jax.ShapeDtypeStruct((m, TOP_K), I32),
                   jax.ShapeDtypeStruct((m, TOP_K), F32)],
        compiler_params=_params("parallel"),
        name="router",
    )(x, g.reshape(1, d), router)


def _experts_kernel(bexp_ref, nsub_ref, nused_ref, *refs, sub, s0):
    if s0:
        refs = refs[1:]
    xs_ref, w1_ref, w3_ref, w2_ref, ys_hbm_ref, o_ref, w1b_ref, w3b_ref, w2b_ref, sem_ref = refs
    s = pl.program_id(0) + s0
    j = pl.program_id(1)
    last_j = j == pl.num_programs(1) - 1
    blk = o_ref.shape[0]

    def flush(first, count):
        r0 = pl.multiple_of(first * sub, sub)
        g0 = pl.multiple_of(s * blk, sub) + r0
        return pltpu.make_async_copy(o_ref.at[pl.ds(r0, count * sub)],
                                     ys_hbm_ref.at[pl.ds(g0, count * sub)], sem_ref.at[0])

    def rows_of(i):
        return pl.ds(pl.multiple_of(i * sub, sub), sub)

    def group(first, count, w1v, w3v, w2v):
        acts = []
        for t in range(count):
            x = xs_ref[rows_of(first + t), :]
            acts.append((_silu(_dot(x, w1v)) * _dot(x, w3v)).astype(BF16))
        for t in range(count):
            o_ref[rows_of(first + t), :] += _dot(acts[t], w2v)

        @pl.when(last_j)
        def _():
            flush(first, count).start()

    def from_scratch(first, count):
        group(first, count, w1b_ref[...], w3b_ref[...], w2b_ref[...])

    @pl.when(s < nused_ref[0])
    def _():
        n = nsub_ref[s]

        @pl.when(j == 0)
        def _():
            def init(i, carry):
                o_ref[rows_of(i), :] = jnp.zeros((sub, o_ref.shape[1]), F32)
                return carry
            lax.fori_loop(0, n, init, 0)

        @pl.when(n >= 3)
        def _():
            w1v = w1_ref[0].astype(BF16)
            w3v = w3_ref[0].astype(BF16)
            w2v = w2_ref[0].astype(BF16)
            w1b_ref[...] = w1v
            w3b_ref[...] = w3v
            w2b_ref[...] = w2v
            group(0, 3, w1v, w3v, w2v)

        @pl.when(n < 3)
        def _():
            w1b_ref[...] = w1_ref[0].astype(BF16)
            w3b_ref[...] = w3_ref[0].astype(BF16)
            w2b_ref[...] = w2_ref[0].astype(BF16)

        def triple(p, carry):
            from_scratch(3 * p, 3)
            return carry
        lax.fori_loop(1, n // 3, triple, 0)

        done = (n // 3) * 3
        rem = n - done

        @pl.when(rem == 1)
        def _():
            from_scratch(done, 1)

        @pl.when(rem == 2)
        def _():
            from_scratch(done, 2)

        @pl.when(last_j)
        def _():
            def wait3(p, carry):
                flush(3 * p, 3).wait()
                return carry
            lax.fori_loop(0, n // 3, wait3, 0)

            @pl.when(rem == 1)
            def _():
                flush(done, 1).wait()

            @pl.when(rem == 2)
            def _():
                flush(done, 2).wait()


def _experts(xs, w1, w3, w2, bexp, nsub, nused, blk, sub, tf, s0, n_s, ys=None):
    n_rows, d = xs.shape
    f = w1.shape[2]
    n_f = f // tf
    assert (ys is None) == (s0 == 0)

    def blk_idx(s, nu):
        return jnp.minimum(s + s0, jnp.maximum(nu[0] - 1, s0))

    def exp_idx(s, be, nu):
        return be[blk_idx(s, nu)]

    def jj(s, j, nu):
        return jnp.where(s + s0 < nu[0], j, n_f - 1)

    in_specs = [pl.BlockSpec((blk, d), lambda s, j, be, ns, nu: (blk_idx(s, nu), 0)),
                pl.BlockSpec((1, d, tf), lambda s, j, be, ns, nu: (exp_idx(s, be, nu), 0, jj(s, j, nu))),
                pl.BlockSpec((1, d, tf), lambda s, j, be, ns, nu: (exp_idx(s, be, nu), 0, jj(s, j, nu))),
                pl.BlockSpec((1, tf, d), lambda s, j, be, ns, nu: (exp_idx(s, be, nu), jj(s, j, nu), 0))]
    args = [xs, w1, w3, w2]
    aliases = {}
    if s0:
        in_specs = [pl.BlockSpec(memory_space=pl.ANY)] + in_specs
        args = [ys] + args
        aliases = {3: 0}
    return pl.pallas_call(
        functools.partial(_experts_kernel, sub=sub, s0=s0),
        grid_spec=pltpu.PrefetchScalarGridSpec(
            num_scalar_prefetch=3,
            grid=(n_s, n_f),
            in_specs=in_specs,
            out_specs=pl.BlockSpec(memory_space=pl.ANY),
            scratch_shapes=[pltpu.VMEM((blk, d), F32),
                            pltpu.VMEM((d, tf), BF16), pltpu.VMEM((d, tf), BF16),
                            pltpu.VMEM((tf, d), BF16), pltpu.SemaphoreType.DMA((1,))]),
        out_shape=jax.ShapeDtypeStruct((n_rows, d), F32),
        input_output_aliases=aliases,
        compiler_params=_params("arbitrary", "arbitrary"),
        name="moe_experts" if s0 == 0 else "moe_experts_overflow",
    )(bexp, nsub, nused, *args)


def _dispatch_kernel(tile_ids_ref, n_tiles_ref, src_ref, x_hbm_ref, g_ref, o_ref, buf_ref, sem_ref, *, sub):
    k = pl.program_id(0)
    n_valid = n_tiles_ref[0]

    def start(step, slot):
        base = tile_ids_ref[step] * sub

        def body(r, carry):
            tok = src_ref[base + r]
            pltpu.make_async_copy(x_hbm_ref.at[pl.ds(tok, 1)], buf_ref.at[slot, pl.ds(r, 1)],
                                  sem_ref.at[slot]).start()
            return carry
        lax.fori_loop(0, sub, body, 0, unroll=8)

    @pl.when(jnp.logical_and(k == 0, n_valid > 0))
    def _():
        start(0, 0)

    @pl.when(k + 1 < n_valid)
    def _():
        start(k + 1, (k + 1) % 2)

    @pl.when(k < n_valid)
    def _():
        slot = k % 2
        pltpu.make_async_copy(x_hbm_ref.at[pl.ds(0, sub)], buf_ref.at[slot], sem_ref.at[slot]).wait()
        o_ref[...] = _rms(buf_ref[slot], g_ref[...]).astype(BF16)


def _dispatch(x, g, tile_ids, n_tiles, src, n_slots, sub):
    m, d = x.shape
    max_tiles = tile_ids.shape[0]
    return pl.pallas_call(
        functools.partial(_dispatch_kernel, sub=sub),
        grid_spec=pltpu.PrefetchScalarGridSpec(
            num_scalar_prefetch=3,
            grid=(max_tiles,),
            in_specs=[pl.BlockSpec(memory_space=pl.ANY),
                      pl.BlockSpec((1, d), lambda k, ids, nt, src: (0, 0))],
            out_specs=pl.BlockSpec((sub, d), lambda k, ids, nt, src: (ids[k], 0)),
            scratch_shapes=[pltpu.VMEM((2, sub, d), F32), pltpu.SemaphoreType.DMA((2,))]),
        out_shape=jax.ShapeDtypeStruct((n_slots, d), BF16),
        compiler_params=_params("arbitrary"),
        name="moe_dispatch",
    )(tile_ids, n_tiles, src, x, g.reshape(1, d))


def _combine_kernel(pos_ref, x_ref, gate_ref, g_ref, ys_hbm_ref, ya_ref, yb_ref, buf_ref, sem_ref, *, tm, n_a):
    i = pl.program_id(0)
    n = pl.num_programs(0)

    def start(step, slot):
        base = step * (tm * TOP_K)

        def body(r, carry):
            for kk in range(TOP_K):
                p = pos_ref[base + r * TOP_K + kk]
                pltpu.make_async_copy(ys_hbm_ref.at[pl.ds(p, 1)], buf_ref.at[slot, kk, pl.ds(r, 1)],
                                      sem_ref.at[slot]).start()
            return carry
        lax.fori_loop(0, tm, body, 0, unroll=4)

    @pl.when(i == 0)
    def _():
        start(0, 0)

    @pl.when(i + 1 < n)
    def _():
        start(i + 1, (i + 1) % 2)

    slot = i % 2
    for kk in range(TOP_K):
        pltpu.make_async_copy(ys_hbm_ref.at[pl.ds(0, tm)], buf_ref.at[slot, kk], sem_ref.at[slot]).wait()
    y = x_ref[...]
    for kk in range(TOP_K):
        y = y + gate_ref[:, kk:kk + 1] * buf_ref[slot, kk]
    y = _rms(y, g_ref[...])

    @pl.when(i < n_a)
    def _():
        ya_ref[...] = y

    @pl.when(i >= n_a)
    def _():
        yb_ref[...] = y


def _combine(x, gates, pos, ys, g, tm, m_a):
    m, d = x.shape
    n_a = m_a // tm
    assert n_a * tm == m_a and m % tm == 0
    return pl.pallas_call(
        functools.partial(_combine_kernel, tm=tm, n_a=n_a),
        grid_spec=pltpu.PrefetchScalarGridSpec(
            num_scalar_prefetch=1,
            grid=(m // tm,),
            in_specs=[pl.BlockSpec((tm, d), lambda i, pos: (i, 0)),
                      pl.BlockSpec((tm, TOP_K), lambda i, pos: (i, 0)),
                      pl.BlockSpec((1, d), lambda i, pos: (0, 0)),
                      pl.BlockSpec(memory_space=pl.ANY)],
            out_specs=[pl.BlockSpec((tm, d), lambda i, pos: (jnp.minimum(i, n_a - 1), 0)),
                       pl.BlockSpec((tm, d), lambda i, pos: (jnp.maximum(i - n_a, 0), 0))],
            scratch_shapes=[pltpu.VMEM((2, TOP_K, tm, d), F32), pltpu.SemaphoreType.DMA((2,))]),
        out_shape=[jax.ShapeDtypeStruct((m_a, d), F32), jax.ShapeDtypeStruct((m - m_a, d), F32)],
        compiler_params=_params("arbitrary"),
        name="moe_combine",
    )(pos, x, gates, g.reshape(1, d), ys)


def _moe_plan(top_i, n_exp, blk, sub, n_blocks, max_tiles):
    m = top_i.shape[0]
    e_flat = top_i.reshape(-1)
    onehot = (e_flat[:, None] == jnp.arange(n_exp, dtype=I32)[None, :]).astype(I32)
    csum = jnp.cumsum(onehot, axis=0)
    rank = jnp.sum((csum - 1) * onehot, axis=1)
    counts = csum[-1]
    blocks_per = (counts + blk - 1) // blk
    block_end = jnp.cumsum(blocks_per)
    block_start = block_end - blocks_per
    n_used = block_end[-1]
    pos = block_start[e_flat] * blk + rank

    s = jnp.arange(n_blocks, dtype=I32)
    bexp = jnp.minimum(jnp.sum((s[:, None] >= block_end[None, :]).astype(I32), axis=1), n_exp - 1)
    valid = jnp.clip(counts[bexp] - (s - block_start[bexp]) * blk, 0, blk)
    valid = jnp.where(s < n_used, valid, 0)

    by_expert = jnp.argsort(e_flat, stable=True).astype(I32) // TOP_K
    by_expert = jnp.concatenate([by_expert, jnp.zeros((m,), I32)])
    first_pair = jnp.cumsum(counts) - counts
    src = jnp.zeros((n_blocks * blk + m,), I32)
    for e in range(n_exp):
        run = lax.dynamic_slice(by_expert, (first_pair[e],), (m,))
        src = lax.dynamic_update_slice(src, run, (block_start[e] * blk,))
    src = src[:n_blocks * blk]
    nsub = (valid + sub - 1) // sub
    last_exp = bexp[jnp.maximum(n_used - 1, 0)]
    bexp = jnp.where(s < n_used, bexp, last_exp)

    per_blk = blk // sub
    t = jnp.arange(n_blocks * per_blk, dtype=I32)
    tile_valid = (t % per_blk) < nsub[t // per_blk]
    n_tiles = jnp.sum(tile_valid.astype(I32))
    order = jnp.argsort(jnp.logical_not(tile_valid), stable=True).astype(I32)[:max_tiles]
    last_tile = order[jnp.maximum(n_tiles - 1, 0)]
    tile_ids = jnp.where(jnp.arange(max_tiles, dtype=I32) < n_tiles, order, last_tile)
    return pos, src, bexp, nsub, n_used.reshape(1), tile_ids, n_tiles.reshape(1)


def _moe(x, top_i, gates, g_ffn, g_final, w1, w3, w2, blk, sub, tf, tm_c, m_a):
    m, d = x.shape
    n_exp = w1.shape[0]
    n_blocks = (m * TOP_K + n_exp * (blk - 1)) // blk
    max_tiles = (m * TOP_K + n_exp * (sub - 1)) // sub
    pos, src, bexp, nsub, nused, tile_ids, n_tiles = _moe_plan(top_i, n_exp, blk, sub, n_blocks, max_tiles)
    xs = _dispatch(x, g_ffn, tile_ids, n_tiles, src, n_blocks * blk, sub)
    n_main = min(n_blocks, n_exp * (-(-(m * TOP_K) // (n_exp * blk))) + 2)
    ys = _experts(xs, w1, w3, w2, bexp, nsub, nused, blk, sub, tf, 0, n_main)
    if n_main < n_blocks:
        ys = lax.cond(
            nused[0] > n_main,
            lambda y: _experts(xs, w1, w3, w2, bexp, nsub, nused, blk, sub, tf, n_main, n_blocks - n_main, ys=y),
            lambda y: y, ys)
    return _combine(x, gates, pos, ys, g_final, tm_c, m_a)


def kernel(x_prompt, x_sample, state_conv_a, state_conv_c, state_pool_d, norm_mix_g, norm_ffn_g, norm_final_g, w_in_ab, conv_a, ln_b_g, ln_b_b, w_s, b_s, w_out_ab, ffn_w1, ffn_w3, ffn_w2, w_in_cd, conv_c, conv_c_b, ln_c_g, ln_c_b, pool_w, pool_scale, w_out_cd, router, exp_w1, exp_w3, exp_w2):
    n_seq, seq_len, d = x_prompt.shape
    nseq_s, dl, _ = x_sample.shape
    mp, ms = n_seq * seq_len, nseq_s * dl
    m = mp + ms
    assert norm_mix_g.shape[0] == 2, "two layers: conv/chunk-MLP + dense FFN, then conformer/pool + MoE"
    assert seq_len % CHUNK == 0 and dl <= CHUNK and mp % ms == 0

    big = m * TOP_K >= 4 * 2560
    tm_in = _pick(m, (2176, 1088, 544, 272))
    tm_mid = _pick(m, (1088, 544, 272))
    tm_row = _pick(ms, (512, 256, 128, 64))
    t_rows = _pick(seq_len, (256, 128))
    tn = 512
    tf_up = _pick(ffn_w1.shape[2], (512, 256))
    tn_down = 256
    tf_e = 256
    blk_e, sub_e = (2560, 256) if big else (512, 128)
    tm_c = _pick(ms, (256, 128, 64))

    xp = x_prompt.reshape(mp, d)
    xs_lm = jnp.transpose(x_sample, (1, 0, 2)).reshape(ms, d)

    h = _norm_in(xp, xs_lm, norm_mix_g[0], tm_row)
    proj = _matmul(h, w_in_ab[0], tm_in, tn)
    mixed, a_p, v_p = _mix_ab_prompt(proj, conv_a[0], ln_b_g[0], ln_b_b[0], w_s[0], b_s[0],
                                     n_seq, seq_len, t_rows)
    mixed, a_s, v_s = _mix_ab_sample(mixed, proj, jnp.transpose(state_conv_a[0], (1, 0, 2)),
                                     conv_a[0], ln_b_g[0], ln_b_b[0], w_s[0], b_s[0], mp, dl, nseq_s)
    x, h = _out_proj(mixed, w_out_ab[0].astype(BF16), xp, xs_lm, norm_ffn_g[0], tm_row)
    act = _ffn_up(h, ffn_w1, ffn_w3, tm_in, tf_up)
    x = _ffn_down(act, ffn_w2, x, tm_mid, tn_down)

    h = _norm_rows(x, norm_mix_g[1], tm_mid)
    proj = _matmul(h, w_in_cd[0], tm_in, tn)
    mixed, c_p, d_p = _mix_cd_prompt(proj, conv_c[0], conv_c_b[0], ln_c_g[0], ln_c_b[0], pool_w[0],
                                     pool_scale[0], n_seq, seq_len, t_rows)
    mixed, g_s = _mix_cd_sample(mixed, proj, jnp.transpose(state_conv_c[0], (1, 0, 2)),
                                jnp.transpose(state_pool_d[0], (1, 0, 2)), conv_c[0], conv_c_b[0],
                                ln_c_g[0], ln_c_b[0], pool_w[0], pool_scale[0], mp, dl, nseq_s)
    (x,) = _out_proj(mixed, w_out_cd[0].astype(BF16), x, x, None, tm_row)
    top_i, gates = _router(x, norm_ffn_g[1], router[0], tm_mid)
    y_p, y_s = _moe(x, top_i, gates, norm_ffn_g[1], norm_final_g, exp_w1[0], exp_w3[0], exp_w2[0],
                    blk_e, sub_e, tf_e, tm_c, mp)

    dc = conv_c.shape[2]
    y_prompt = y_p.reshape(n_seq, seq_len, d)
    y_sample = jnp.transpose(y_s.reshape(dl, nseq_s, d), (1, 0, 2))
    new_a_s = jnp.transpose(a_s, (1, 0, 2))
    new_v_s = jnp.transpose(v_s.reshape(dl, nseq_s, -1), (1, 0, 2))
    g_rows = jnp.transpose(g_s.reshape(dl, nseq_s, dc), (1, 0, 2))
    p_rows = jnp.transpose(proj[mp:, 2 * dc:].astype(F32).reshape(dl, nseq_s, -1), (1, 0, 2))
    n_c = state_conv_c.shape[2]
    n_d = state_pool_d.shape[2]
    new_c_s = jnp.concatenate([state_conv_c[0], g_rows], axis=1)[:, -n_c:]
    new_d_s = jnp.concatenate([state_pool_d[0], p_rows], axis=1)[:, -n_d:]
    return (y_prompt, y_sample,
            a_p[None], new_a_s[None],
            v_p[None], new_v_s[None],
            c_p[None], new_c_s[None],
            d_p[None], new_d_s[None])
```

```python
import functools

import jax
import jax.numpy as jnp
from jax import lax
from jax.experimental import pallas as pl
from jax.experimental.pallas import tpu as pltpu

F32 = jnp.float32
BF16 = jnp.bfloat16
I32 = jnp.int32

EPS = 1e-6
PAST_LEN = 16384
CHUNK = 128
HEAD_B = 128
POOL_WINDOWS = (2, 4, 8, 16)
TOP_K = 2

VMEM_LIMIT_BYTES = 60 * 1024 * 1024
SUBLANES = 8
LANES = 128


def _params(*sem):
    return pltpu.CompilerParams(dimension_semantics=sem, vmem_limit_bytes=VMEM_LIMIT_BYTES)


def _pick(n, candidates):
    for c in candidates:
        if n % c == 0:
            return c
    raise ValueError(f"no tile in {candidates} divides {n}")


def _rms(x, g):
    return x * lax.rsqrt(jnp.mean(x * x, axis=-1, keepdims=True) + EPS) * g


def _layernorm(x, g, b):
    mu = jnp.mean(x, axis=-1, keepdims=True)
    xc = x - mu
    return xc * lax.rsqrt(jnp.mean(xc * xc, axis=-1, keepdims=True) + EPS) * g + b


def _silu(x):
    return x * jax.nn.sigmoid(x)


def _dot(a, b):
    return jnp.dot(a, b, preferred_element_type=F32)


def _f32(x):
    return x.astype(F32)


def _mix_ab_prompt_kernel(bg_ref, cg_ref, xin_ref, u_ref, v_ref, cgh_ref, xinh_ref,
                          cw_ref, lng_ref, lnb_ref, ws_ref, bsb_ref,
                          o_ref, astate_ref, vstate_ref, abuf_ref):
    t = pl.program_id(1)
    t_rows, da = bg_ref.shape
    n_heads = ws_ref.shape[0]
    halo = cgh_ref.shape[0]

    a = _f32(cg_ref[...]) * _f32(xin_ref[...])
    prev = jnp.where(t == 0, 0.0, _f32(cgh_ref[...]) * _f32(xinh_ref[...]))
    abuf_ref[0:halo, :] = prev
    abuf_ref[halo:halo + t_rows, :] = a
    conv = (cw_ref[0:1, :] * abuf_ref[halo - 2:halo - 2 + t_rows, :]
            + cw_ref[1:2, :] * abuf_ref[halo - 1:halo - 1 + t_rows, :]
            + cw_ref[2:3, :] * a)
    o_ref[:, 0:da] = (_f32(bg_ref[...]) * conv).astype(BF16)
    astate_ref[0] = abuf_ref[halo + t_rows - 2:halo + t_rows, :]

    row = lax.broadcasted_iota(I32, (CHUNK, CHUNK), 0)
    col = lax.broadcasted_iota(I32, (CHUNK, CHUNK), 1)
    lower = row >= col
    for h in range(n_heads):
        cols = slice(h * HEAD_B, (h + 1) * HEAD_B)
        vn = _layernorm(jax.nn.gelu(_f32(v_ref[:, cols])), lng_ref[:, cols], lnb_ref[:, cols])
        vstate_ref[0, :, cols] = vn[t_rows - CHUNK:t_rows]
        w = jnp.where(lower, ws_ref[h], 0.0).astype(BF16)
        for c in range(t_rows // CHUNK):
            rows = slice(c * CHUNK, (c + 1) * CHUNK)
            mix = _dot(w, vn[rows].astype(BF16)) + bsb_ref[h]
            o_ref[rows, da + h * HEAD_B:da + (h + 1) * HEAD_B] = (
                jax.nn.gelu(_f32(u_ref[rows, cols])) * mix).astype(BF16)


def _mix_ab_prompt(proj, conv_a, ln_g, ln_b, w_s, b_s, n_seq, seq_len, t_rows):
    m = proj.shape[0]
    da = conv_a.shape[1]
    db = ln_g.shape[0]
    n_heads = w_s.shape[0]
    tiles = seq_len // t_rows
    halo_rows = 2 * SUBLANES
    halo_blocks = t_rows // halo_rows

    def col_spec(c):
        return pl.BlockSpec((t_rows, da), lambda b, t: (b * tiles + t, c))

    def halo_spec(c):
        return pl.BlockSpec(
            (halo_rows, da),
            lambda b, t: (jnp.maximum((b * tiles + t) * halo_blocks - 1, 0), c))

    def full(shape):
        return pl.BlockSpec(shape, lambda b, t: (0,) * len(shape))

    bsb = jnp.broadcast_to(b_s[:, :CHUNK, None], (n_heads, CHUNK, HEAD_B))
    return pl.pallas_call(
        _mix_ab_prompt_kernel,
        grid=(n_seq, tiles),
        in_specs=[col_spec(0), col_spec(1), col_spec(2), col_spec(3), col_spec(4),
                  halo_spec(1), halo_spec(2),
                  full((conv_a.shape[0], da)), full((1, db)), full((1, db)),
                  full((n_heads, CHUNK, CHUNK)), full((n_heads, CHUNK, HEAD_B))],
        out_specs=[pl.BlockSpec((t_rows, da + db), lambda b, t: (b * tiles + t, 0)),
                   pl.BlockSpec((1, 2, da), lambda b, t: (b, 0, 0)),
                   pl.BlockSpec((1, CHUNK, db), lambda b, t: (b, 0, 0))],
        out_shape=[jax.ShapeDtypeStruct((m, da + db), BF16),
                   jax.ShapeDtypeStruct((n_seq, 2, da), F32),
                   jax.ShapeDtypeStruct((n_seq, CHUNK, db), F32)],
        scratch_shapes=[pltpu.VMEM((halo_rows + t_rows, da), F32)],
        compiler_params=_params("parallel", "arbitrary"),
        name="mix_ab_prompt",
    )(proj, proj, proj, proj, proj, proj, proj, conv_a, ln_g.reshape(1, db), ln_b.reshape(1, db),
      w_s[:, :CHUNK, :CHUNK], bsb)


def _mix_ab_sample_kernel(mixed_hbm_ref, bg_ref, cg_ref, xin_ref, u_ref, v_ref, st_ref,
                          cw_ref, lng_ref, lnb_ref, wsl_ref, bsl_ref,
                          o_ref, astate_ref, vstate_ref, *, dl, nseq):
    del mixed_hbm_ref
    da = bg_ref.shape[1]
    n_heads = lng_ref.shape[1] // HEAD_B
    n_state = st_ref.shape[0]

    def slab(l):
        return slice(l * nseq, (l + 1) * nseq)

    a_pad = [st_ref[k] for k in range(n_state)]
    a_pad += [_f32(cg_ref[slab(l), :]) * _f32(xin_ref[slab(l), :]) for l in range(dl)]
    for l in range(dl):
        conv = (cw_ref[0:1, :] * a_pad[l] + cw_ref[1:2, :] * a_pad[l + 1]
                + cw_ref[2:3, :] * a_pad[l + 2])
        o_ref[slab(l), 0:da] = (_f32(bg_ref[slab(l), :]) * conv).astype(BF16)
    for k in range(n_state):
        astate_ref[k] = a_pad[dl + k]

    for h in range(n_heads):
        cols = slice(h * HEAD_B, (h + 1) * HEAD_B)
        vn = _layernorm(jax.nn.gelu(_f32(v_ref[:, cols])), lng_ref[:, cols], lnb_ref[:, cols])
        vstate_ref[:, cols] = vn
        for i in range(dl):
            mix = bsl_ref[i:i + 1, cols]
            for j in range(i + 1):
                mix = mix + wsl_ref[i * dl + j:i * dl + j + 1, cols] * vn[slab(j)]
            o_ref[slab(i), da + h * HEAD_B:da + (h + 1) * HEAD_B] = (
                jax.nn.gelu(_f32(u_ref[slab(i), cols])) * mix).astype(BF16)


def _mix_ab_sample(mixed, proj, state_lm, conv_a, ln_g, ln_b, w_s, b_s, row0, dl, nseq):
    ms = dl * nseq
    da = conv_a.shape[1]
    db = ln_g.shape[0]
    n_heads = w_s.shape[0]
    rb = row0 // ms

    def col_spec(c):
        return pl.BlockSpec((ms, da), lambda i: (rb, c))

    def full(shape):
        return pl.BlockSpec(shape, lambda i: (0,) * len(shape))

    wsl = jnp.repeat(jnp.transpose(w_s[:, :dl, :dl], (1, 2, 0)).reshape(dl * dl, n_heads), HEAD_B, axis=1)
    bsl = jnp.repeat(jnp.transpose(b_s[:, :dl], (1, 0)), HEAD_B, axis=1)
    return pl.pallas_call(
        functools.partial(_mix_ab_sample_kernel, dl=dl, nseq=nseq),
        grid=(1,),
        in_specs=[pl.BlockSpec(memory_space=pl.ANY),
                  col_spec(0), col_spec(1), col_spec(2), col_spec(3), col_spec(4),
                  full(state_lm.shape), full((conv_a.shape[0], da)), full((1, db)), full((1, db)),
                  full((dl * dl, db)), full((dl, db))],
        out_specs=[pl.BlockSpec((ms, da + db), lambda i: (rb, 0)),
                   full(state_lm.shape),
                   full((ms, db))],
        out_shape=[jax.ShapeDtypeStruct(mixed.shape, BF16),
                   jax.ShapeDtypeStruct(state_lm.shape, F32),
                   jax.ShapeDtypeStruct((ms, db), F32)],
        input_output_aliases={0: 0},
        compiler_params=_params("arbitrary"),
        name="mix_ab_sample",
    )(mixed, proj, proj, proj, proj, proj, state_lm, conv_a, ln_g.reshape(1, db), ln_b.reshape(1, db),
      wsl, bsl)


def _mix_cd_prompt_kernel(ga_ref, gb_ref, p_ref, gah_ref, gbh_ref, ph_ref,
                          cw_ref, cb_ref, lng_ref, lnb_ref, pw_ref, ps_ref,
                          o_ref, cstate_ref, dstate_ref, gbuf_ref, pbuf_ref, cbuf_ref, shift_ref):
    t = pl.program_id(1)
    t_rows, dc = ga_ref.shape
    k_c = cw_ref.shape[0]
    c_halo = gah_ref.shape[0]
    d_halo = ph_ref.shape[0]
    gd = pw_ref.shape[1]

    g = _f32(ga_ref[...]) * jax.nn.sigmoid(_f32(gb_ref[...]))
    gbuf_ref[0:c_halo, :] = jnp.where(
        t == 0, 0.0, _f32(gah_ref[...]) * jax.nn.sigmoid(_f32(gbh_ref[...])))
    gbuf_ref[c_halo:c_halo + t_rows, :] = g
    n_shift = shift_ref.shape[1]
    for r in range(1, SUBLANES):
        shift_ref[r - 1] = gbuf_ref[r:r + n_shift, :]
    first = c_halo - (k_c - 1)
    for cblk in range(dc // LANES):
        cols = slice(cblk * LANES, (cblk + 1) * LANES)
        acc = cb_ref[:, cols]
        for k in range(k_c):
            q, r = divmod(first + k, SUBLANES)
            if r == 0:
                win = gbuf_ref[first + k:first + k + t_rows, cols]
            else:
                win = shift_ref[r - 1, q * SUBLANES:q * SUBLANES + t_rows, cols]
            acc = acc + cw_ref[k:k + 1, cols] * win
        cbuf_ref[:, cols] = acc
    o_ref[:, 0:dc] = _silu(_layernorm(cbuf_ref[...], lng_ref[...], lnb_ref[...])).astype(BF16)
    cstate_ref[0] = gbuf_ref[c_halo + t_rows - (k_c - 1):c_halo + t_rows, :]

    pbuf_ref[0:d_halo, :] = jnp.where(t == 0, 0.0, _f32(ph_ref[...]))
    pbuf_ref[d_halo:d_halo + t_rows, :] = _f32(p_ref[...])
    pos = t * t_rows + lax.broadcasted_iota(I32, (t_rows, 1), 0)
    for gi, w in enumerate(POOL_WINDOWS):
        cols = slice(gi * gd, (gi + 1) * gd)
        cur = pbuf_ref[d_halo:d_halo + t_rows, cols]
        s = cur
        for k in range(1, w):
            s = s + pbuf_ref[d_halo - k:d_halo - k + t_rows, cols]
        cnt = jnp.minimum(w, pos + 1).astype(F32)
        diff = (s / cnt - cur).astype(BF16)
        d = _dot(diff, pw_ref[gi].astype(BF16)) * ps_ref[:, cols]
        o_ref[:, dc + gi * gd:dc + (gi + 1) * gd] = d.astype(BF16)
    n_keep = dstate_ref.shape[1]
    dstate_ref[0] = pbuf_ref[d_halo + t_rows - n_keep:d_halo + t_rows, :]


def _mix_cd_prompt(proj, conv_c, conv_c_b, ln_g, ln_b, pool_w, pool_scale, n_seq, seq_len, t_rows):
    m = proj.shape[0]
    k_c, dc = conv_c.shape
    dd = pool_scale.shape[0]
    n_pool, gd = pool_w.shape[0], pool_w.shape[1]
    tiles = seq_len // t_rows
    c_halo = -(-(k_c - 1) // SUBLANES) * SUBLANES
    d_halo = -(-(max(POOL_WINDOWS) - 1) // SUBLANES) * SUBLANES
    assert t_rows % c_halo == 0 and t_rows % d_halo == 0 and dc == dd

    def col_spec(c):
        return pl.BlockSpec((t_rows, dc), lambda b, t: (b * tiles + t, c))

    def halo_spec(c, rows):
        per = t_rows // rows
        return pl.BlockSpec((rows, dc),
                            lambda b, t: (jnp.maximum((b * tiles + t) * per - 1, 0), c))

    def full(shape):
        return pl.BlockSpec(shape, lambda b, t: (0,) * len(shape))

    return pl.pallas_call(
        _mix_cd_prompt_kernel,
        grid=(n_seq, tiles),
        in_specs=[col_spec(0), col_spec(1), col_spec(2),
                  halo_spec(0, c_halo), halo_spec(1, c_halo), halo_spec(2, d_halo),
                  full((k_c, dc)), full((1, dc)), full((1, dc)), full((1, dc)),
                  full((n_pool, gd, gd)), full((1, dd))],
        out_specs=[pl.BlockSpec((t_rows, dc + dd), lambda b, t: (b * tiles + t, 0)),
                   pl.BlockSpec((1, k_c - 1, dc), lambda b, t: (b, 0, 0)),
                   pl.BlockSpec((1, max(POOL_WINDOWS) - 1, dd), lambda b, t: (b, 0, 0))],
        out_shape=[jax.ShapeDtypeStruct((m, dc + dd), BF16),
                   jax.ShapeDtypeStruct((n_seq, k_c - 1, dc), F32),
                   jax.ShapeDtypeStruct((n_seq, max(POOL_WINDOWS) - 1, dd), F32)],
        scratch_shapes=[pltpu.VMEM((c_halo + t_rows, dc), F32),
                        pltpu.VMEM((d_halo + t_rows, dd), F32),
                        pltpu.VMEM((t_rows, dc), F32),
                        pltpu.VMEM((SUBLANES - 1, c_halo + t_rows - SUBLANES, dc), F32)],
        compiler_params=_params("parallel", "arbitrary"),
        name="mix_cd_prompt",
    )(proj, proj, proj, proj, proj, proj, conv_c, conv_c_b.reshape(1, dc), ln_g.reshape(1, dc),
      ln_b.reshape(1, dc), pool_w, pool_scale.reshape(1, dd))


def _mix_cd_sample_kernel(mixed_hbm_ref, ga_ref, gb_ref, p_ref, stc_ref, std_ref,
                          cw_ref, cb_ref, lng_ref, lnb_ref, pw_ref, ps_ref,
                          o_ref, g_ref, cbuf_ref, dbuf_ref, *, dl, nseq, start_pos):
    del mixed_hbm_ref
    gi = pl.program_id(0)
    n_groups = pl.num_programs(0)
    gd = ga_ref.shape[1]
    k_c = cw_ref.shape[0]
    n_c = stc_ref.shape[0]
    n_d = std_ref.shape[0]

    def slab(l):
        return slice(l * nseq, (l + 1) * nseq)

    g = _f32(ga_ref[...]) * jax.nn.sigmoid(_f32(gb_ref[...]))
    g_ref[...] = g
    c_pad = [stc_ref[k] for k in range(n_c)] + [g[slab(l)] for l in range(dl)]
    d_pad = [std_ref[k] for k in range(n_d)] + [_f32(p_ref[slab(l), :]) for l in range(dl)]
    for l in range(dl):
        acc = cb_ref[...] + cw_ref[0:1, :] * c_pad[l]
        for k in range(1, k_c):
            acc = acc + cw_ref[k:k + 1, :] * c_pad[l + k]
        cbuf_ref[gi, slab(l), :] = acc

        cur = d_pad[n_d + l]
        s = cur
        pooled = jnp.zeros_like(cur)
        for k in range(1, max(POOL_WINDOWS) + 1):
            if k in POOL_WINDOWS:
                cnt = float(min(k, start_pos + l + 1))
                pooled = jnp.where(gi == POOL_WINDOWS.index(k), s / cnt, pooled)
            if k < max(POOL_WINDOWS):
                s = s + d_pad[n_d + l - k]
        diff = (pooled - cur).astype(BF16)
        dbuf_ref[gi, slab(l), :] = (_dot(diff, pw_ref[0].astype(BF16)) * ps_ref[...]).astype(BF16)

    @pl.when(gi == n_groups - 1)
    def _():
        n_blk = cbuf_ref.shape[0]
        dc = n_blk * gd
        total = cbuf_ref[0].sum(axis=-1, keepdims=True)
        for b in range(1, n_blk):
            total = total + cbuf_ref[b].sum(axis=-1, keepdims=True)
        mu = total / dc
        var = None
        for b in range(n_blk):
            xc = cbuf_ref[b] - mu
            sq = (xc * xc).sum(axis=-1, keepdims=True)
            var = sq if var is None else var + sq
        rstd = lax.rsqrt(var / dc + EPS)
        for b in range(n_blk):
            cols = slice(b * gd, (b + 1) * gd)
            y = (cbuf_ref[b] - mu) * rstd * lng_ref[:, cols] + lnb_ref[:, cols]
            o_ref[:, cols] = _silu(y).astype(BF16)
            o_ref[:, dc + b * gd:dc + (b + 1) * gd] = dbuf_ref[b]


def _mix_cd_sample(mixed, proj, stc_lm, std_lm, conv_c, conv_c_b, ln_g, ln_b, pool_w, pool_scale,
                   row0, dl, nseq):
    ms = dl * nseq
    k_c, dc = conv_c.shape
    dd = pool_scale.shape[0]
    n_pool, gd = pool_w.shape[0], pool_w.shape[1]
    assert dc == dd == n_pool * gd and n_pool == len(POOL_WINDOWS)
    rb = row0 // ms

    return pl.pallas_call(
        functools.partial(_mix_cd_sample_kernel, dl=dl, nseq=nseq, start_pos=PAST_LEN),
        grid=(n_pool,),
        in_specs=[pl.BlockSpec(memory_space=pl.ANY),
                  pl.BlockSpec((ms, gd), lambda i: (rb, i)),
                  pl.BlockSpec((ms, gd), lambda i: (rb, n_pool + i)),
                  pl.BlockSpec((ms, gd), lambda i: (rb, 2 * n_pool + i)),
                  pl.BlockSpec((stc_lm.shape[0], nseq, gd), lambda i: (0, 0, i)),
                  pl.BlockSpec((std_lm.shape[0], nseq, gd), lambda i: (0, 0, i)),
                  pl.BlockSpec((k_c, gd), lambda i: (0, i)),
                  pl.BlockSpec((1, gd), lambda i: (0, i)),
                  pl.BlockSpec((1, dc), lambda i: (0, 0)),
                  pl.BlockSpec((1, dc), lambda i: (0, 0)),
                  pl.BlockSpec((1, gd, gd), lambda i: (i, 0, 0)),
                  pl.BlockSpec((1, gd), lambda i: (0, i))],
        out_specs=[pl.BlockSpec((ms, dc + dd), lambda i: (rb, 0)),
                   pl.BlockSpec((ms, gd), lambda i: (0, i))],
        out_shape=[jax.ShapeDtypeStruct(mixed.shape, BF16),
                   jax.ShapeDtypeStruct((ms, dc), F32)],
        scratch_shapes=[pltpu.VMEM((n_pool, ms, gd), F32),
                        pltpu.VMEM((n_pool, ms, gd), BF16)],
        input_output_aliases={0: 0},
        compiler_params=_params("arbitrary"),
        name="mix_cd_sample",
    )(mixed, proj, proj, proj, stc_lm, std_lm, conv_c, conv_c_b.reshape(1, dc), ln_g.reshape(1, dc),
      ln_b.reshape(1, dc), pool_w, pool_scale.reshape(1, dd))


def _norm_in_kernel(xp_ref, xs_ref, g_ref, o_ref, *, n_a):
    i = pl.program_id(0)

    @pl.when(i < n_a)
    def _():
        o_ref[...] = _rms(xp_ref[...], g_ref[...]).astype(BF16)

    @pl.when(i >= n_a)
    def _():
        o_ref[...] = _rms(xs_ref[...], g_ref[...]).astype(BF16)


def _norm_in(xp, xs, g, tm):
    mp, d = xp.shape
    ms = xs.shape[0]
    n_a, n_b = mp // tm, ms // tm
    return pl.pallas_call(
        functools.partial(_norm_in_kernel, n_a=n_a),
        grid=(n_a + n_b,),
        in_specs=[pl.BlockSpec((tm, d), lambda i: (jnp.minimum(i, n_a - 1), 0)),
                  pl.BlockSpec((tm, d), lambda i: (jnp.maximum(i - n_a, 0), 0)),
                  pl.BlockSpec((1, d), lambda i: (0, 0))],
        out_specs=pl.BlockSpec((tm, d), lambda i: (i, 0)),
        out_shape=jax.ShapeDtypeStruct((mp + ms, d), BF16),
        compiler_params=_params("arbitrary"),
        name="norm_in",
    )(xp, xs, g.reshape(1, d))


def _matmul_kernel(a_ref, w_ref, o_ref):
    o_ref[...] = _dot(a_ref[...], w_ref[...].astype(BF16)).astype(o_ref.dtype)


def _matmul(a, w, tm, tn):
    m, k = a.shape
    n = w.shape[1]
    return pl.pallas_call(
        _matmul_kernel,
        grid=(m // tm, n // tn),
        in_specs=[pl.BlockSpec((tm, k), lambda i, j: (i, 0)),
                  pl.BlockSpec((k, tn), lambda i, j: (0, j))],
        out_specs=pl.BlockSpec((tm, tn), lambda i, j: (i, j)),
        out_shape=jax.ShapeDtypeStruct((m, n), BF16),
        compiler_params=_params("parallel", "arbitrary"),
        name="in_proj",
    )(a, w)


def _ffn_up_kernel(h_ref, w1_ref, w3_ref, o_ref):
    w1 = w1_ref[0].astype(BF16)
    w3 = w3_ref[0].astype(BF16)
    tm = h_ref.shape[0]
    n_parts = 2 if tm % (2 * 2 * SUBLANES) == 0 else 1
    rows_per = tm // n_parts
    for part in range(n_parts):
        rows = slice(part * rows_per, (part + 1) * rows_per)
        h = h_ref[rows, :]
        o_ref[rows, :] = (_silu(_dot(h, w1)) * _dot(h, w3)).astype(BF16)


def _ffn_up(h, w1, w3, tm, tf):
    m, d = h.shape
    f = w1.shape[2]
    return pl.pallas_call(
        _ffn_up_kernel,
        grid=(m // tm, f // tf),
        in_specs=[pl.BlockSpec((tm, d), lambda i, j: (i, 0)),
                  pl.BlockSpec((1, d, tf), lambda i, j: (0, 0, j)),
                  pl.BlockSpec((1, d, tf), lambda i, j: (0, 0, j))],
        out_specs=pl.BlockSpec((tm, tf), lambda i, j: (i, j)),
        out_shape=jax.ShapeDtypeStruct((m, f), BF16),
        compiler_params=_params("parallel", "arbitrary"),
        name="ffn_up",
    )(h, w1, w3)


def _ffn_down_kernel(a_ref, w_ref, r_ref, o_ref):
    o_ref[...] = r_ref[...] + _dot(a_ref[...], w_ref[0].astype(BF16))


def _ffn_down(act, w2, res, tm, tn):
    m, f = act.shape
    d = w2.shape[2]
    return pl.pallas_call(
        _ffn_down_kernel,
        grid=(m // tm, d // tn),
        in_specs=[pl.BlockSpec((tm, f), lambda i, j: (i, 0)),
                  pl.BlockSpec((1, f, tn), lambda i, j: (0, 0, j)),
                  pl.BlockSpec((tm, tn), lambda i, j: (i, j))],
        out_specs=pl.BlockSpec((tm, tn), lambda i, j: (i, j)),
        out_shape=jax.ShapeDtypeStruct((m, d), F32),
        compiler_params=_params("parallel", "arbitrary"),
        name="ffn_down",
    )(act, w2, res)


def _norm_rows_kernel(x_ref, g_ref, o_ref):
    o_ref[...] = _rms(x_ref[...], g_ref[...]).astype(BF16)


def _norm_rows(x, g, tm):
    m, d = x.shape
    return pl.pallas_call(
        _norm_rows_kernel,
        grid=(m // tm,),
        in_specs=[pl.BlockSpec((tm, d), lambda i: (i, 0)),
                  pl.BlockSpec((1, d), lambda i: (0, 0))],
        out_specs=pl.BlockSpec((tm, d), lambda i: (i, 0)),
        out_shape=jax.ShapeDtypeStruct((m, d), BF16),
        compiler_params=_params("parallel"),
        name="norm_rows",
    )(x, g.reshape(1, d))


def _out_proj_kernel(a_ref, w_ref, ra_ref, rb_ref, *rest, n_a, with_h):
    i = pl.program_id(0)
    if with_h:
        g_ref, x_ref, h_ref = rest
    else:
        (x_ref,) = rest
    tm = a_ref.shape[0]
    n_parts = 2 if tm % (2 * 2 * SUBLANES) == 0 else 1
    rows_per = tm // n_parts
    for part in range(n_parts):
        rows = slice(part * rows_per, (part + 1) * rows_per)
        res = jnp.where(i < n_a, ra_ref[rows, :], rb_ref[rows, :])
        x = res + _dot(a_ref[rows, :], w_ref[...])
        x_ref[rows, :] = x
        if with_h:
            h_ref[rows, :] = _rms(x, g_ref[...]).astype(BF16)


def _out_proj(a, w_bf16, res_a, res_b, g, tm):
    m, k = a.shape
    d = w_bf16.shape[1]
    n_a = res_a.shape[0] // tm
    n_b = m // tm - n_a
    assert n_a * tm == res_a.shape[0] and (n_b == 0 or n_b * tm == res_b.shape[0])
    with_h = g is not None
    in_specs = [pl.BlockSpec((tm, k), lambda i: (i, 0)),
                pl.BlockSpec((k, d), lambda i: (0, 0), pipeline_mode=pl.Buffered(1)),
                pl.BlockSpec((tm, d), lambda i: (jnp.minimum(i, n_a - 1), 0)),
                pl.BlockSpec((tm, d), lambda i: (jnp.maximum(i - n_a, 0), 0))]
    args = [a, w_bf16, res_a, res_b]
    row_spec = pl.BlockSpec((tm, d), lambda i: (i, 0))
    out_specs = [row_spec]
    out_shape = [jax.ShapeDtypeStruct((m, d), F32)]
    if with_h:
        in_specs.append(pl.BlockSpec((1, d), lambda i: (0, 0)))
        args.append(g.reshape(1, d))
        out_specs.append(row_spec)
        out_shape.append(jax.ShapeDtypeStruct((m, d), BF16))
    return pl.pallas_call(
        functools.partial(_out_proj_kernel, n_a=n_a, with_h=with_h),
        grid=(m // tm,),
        in_specs=in_specs, out_specs=out_specs, out_shape=out_shape,
        compiler_params=_params("arbitrary"),
        name="out_proj_h" if with_h else "out_proj",
    )(*args)


def _router_kernel(x_ref, g_ref, r_ref, idx_ref, gate_ref):
    h = _rms(x_ref[...], g_ref[...])
    h_hi = h.astype(BF16)
    h_lo = (h - h_hi.astype(F32)).astype(BF16)
    r = r_ref[...]
    r_hi = r.astype(BF16)
    r_lo = (r - r_hi.astype(F32)).astype(BF16)
    logits = _dot(h_hi, r_hi) + (_dot(h_hi, r_lo) + _dot(h_lo, r_hi))
    n_exp = float(logits.shape[1])
    lane = lax.broadcasted_iota(I32, logits.shape, 1).astype(F32)
    m1 = jnp.max(logits, axis=-1, keepdims=True)
    i1 = jnp.min(jnp.where(logits == m1, lane, n_exp), axis=-1, keepdims=True)
    rest = jnp.where(lane == i1, -jnp.inf, logits)
    m2 = jnp.max(rest, axis=-1, keepdims=True)
    i2 = jnp.min(jnp.where(rest == m2, lane, n_exp), axis=-1, keepdims=True)
    e = jnp.exp(m2 - m1)
    two = lax.broadcasted_iota(I32, idx_ref.shape, 1)
    idx_ref[...] = jnp.where(two == 0, i1, i2).astype(I32)
    gate_ref[...] = jnp.where(two == 0, 1.0, e) / (1.0 + e)


def _router(x, g, router, tm):
    m, d = x.shape
    n_exp = router.shape[1]
    return pl.pallas_call(
        _router_kernel,
        grid=(m // tm,),
        in_specs=[pl.BlockSpec((tm, d), lambda i: (i, 0)),
                  pl.BlockSpec((1, d), lambda i: (0, 0)),
                  pl.BlockSpec((d, n_exp), lambda i: (0, 0))],
        out_specs=[pl.BlockSpec((tm, TOP_K), lambda i: (i, 0)),
                   pl.BlockSpec((tm, TOP_K), lambda i: (i, 0))],
        out_shape=[jax.ShapeDtypeStruct((m, TOP_K), I32),
                   jax.ShapeDtypeStruct((m, TOP_K), F32)],
        compiler_params=_params("parallel"),
        name="router",
    )(x, g.reshape(1, d), router)


def _experts_kernel(bexp_ref, nsub_ref, nused_ref, *refs, sub, s0):
    if s0:
        refs = refs[1:]
    xs_ref, w1_ref, w3_ref, w2_ref, ys_hbm_ref, o_ref, w1b_ref, w3b_ref, w2b_ref, sem_ref = refs
    s = pl.program_id(0) + s0
    j = pl.program_id(1)
    last_j = j == pl.num_programs(1) - 1
    blk = o_ref.shape[0]

    def flush(first, count):
        r0 = pl.multiple_of(first * sub, sub)
        g0 = pl.multiple_of(s * blk, sub) + r0
        return pltpu.make_async_copy(o_ref.at[pl.ds(r0, count * sub)],
                                     ys_hbm_ref.at[pl.ds(g0, count * sub)], sem_ref.at[0])

    def rows_of(i):
        return pl.ds(pl.multiple_of(i * sub, sub), sub)

    def group(first, count, w1v, w3v, w2v):
        acts = []
        for t in range(count):
            x = xs_ref[rows_of(first + t), :]
            acts.append((_silu(_dot(x, w1v)) * _dot(x, w3v)).astype(BF16))
        for t in range(count):
            o_ref[rows_of(first + t), :] += _dot(acts[t], w2v)

        @pl.when(last_j)
        def _():
            flush(first, count).start()

    def from_scratch(first, count):
        group(first, count, w1b_ref[...], w3b_ref[...], w2b_ref[...])

    @pl.when(s < nused_ref[0])
    def _():
        n = nsub_ref[s]

        @pl.when(j == 0)
        def _():
            def init(i, carry):
                o_ref[rows_of(i), :] = jnp.zeros((sub, o_ref.shape[1]), F32)
                return carry
            lax.fori_loop(0, n, init, 0)

        @pl.when(n >= 3)
        def _():
            w1v = w1_ref[0].astype(BF16)
            w3v = w3_ref[0].astype(BF16)
            w2v = w2_ref[0].astype(BF16)
            w1b_ref[...] = w1v
            w3b_ref[...] = w3v
            w2b_ref[...] = w2v
            group(0, 3, w1v, w3v, w2v)

        @pl.when(n < 3)
        def _():
            w1b_ref[...] = w1_ref[0].astype(BF16)
            w3b_ref[...] = w3_ref[0].astype(BF16)
            w2b_ref[...] = w2_ref[0].astype(BF16)

        def triple(p, carry):
            from_scratch(3 * p, 3)
            return carry
        lax.fori_loop(1, n // 3, triple, 0)

        done = (n // 3) * 3
        rem = n - done

        @pl.when(rem == 1)
        def _():
            from_scratch(done, 1)

        @pl.when(rem == 2)
        def _():
            from_scratch(done, 2)

        @pl.when(last_j)
        def _():
            def wait3(p, carry):
                flush(3 * p, 3).wait()
                return carry
            lax.fori_loop(0, n // 3, wait3, 0)

            @pl.when(rem == 1)
            def _():
                flush(done, 1).wait()

            @pl.when(rem == 2)
            def _():
                flush(done, 2).wait()


def _experts(xs, w1, w3, w2, bexp, nsub, nused, blk, sub, tf, s0, n_s, ys=None):
    n_rows, d = xs.shape
    f = w1.shape[2]
    n_f = f // tf
    assert (ys is None) == (s0 == 0)

    def blk_idx(s, nu):
        return jnp.minimum(s + s0, jnp.maximum(nu[0] - 1, s0))

    def exp_idx(s, be, nu):
        return be[blk_idx(s, nu)]

    def jj(s, j, nu):
        return jnp.where(s + s0 < nu[0], j, n_f - 1)

    in_specs = [pl.BlockSpec((blk, d), lambda s, j, be, ns, nu: (blk_idx(s, nu), 0)),
                pl.BlockSpec((1, d, tf), lambda s, j, be, ns, nu: (exp_idx(s, be, nu), 0, jj(s, j, nu))),
                pl.BlockSpec((1, d, tf), lambda s, j, be, ns, nu: (exp_idx(s, be, nu), 0, jj(s, j, nu))),
                pl.BlockSpec((1, tf, d), lambda s, j, be, ns, nu: (exp_idx(s, be, nu), jj(s, j, nu), 0))]
    args = [xs, w1, w3, w2]
    aliases = {}
    if s0:
        in_specs = [pl.BlockSpec(memory_space=pl.ANY)] + in_specs
        args = [ys] + args
        aliases = {3: 0}
    return pl.pallas_call(
        functools.partial(_experts_kernel, sub=sub, s0=s0),
        grid_spec=pltpu.PrefetchScalarGridSpec(
            num_scalar_prefetch=3,
            grid=(n_s, n_f),
            in_specs=in_specs,
            out_specs=pl.BlockSpec(memory_space=pl.ANY),
            scratch_shapes=[pltpu.VMEM((blk, d), F32),
                            pltpu.VMEM((d, tf), BF16), pltpu.VMEM((d, tf), BF16),
                            pltpu.VMEM((tf, d), BF16), pltpu.SemaphoreType.DMA((1,))]),
        out_shape=jax.ShapeDtypeStruct((n_rows, d), F32),
        input_output_aliases=aliases,
        compiler_params=_params("arbitrary", "arbitrary"),
        name="moe_experts" if s0 == 0 else "moe_experts_overflow",
    )(bexp, nsub, nused, *args)


def _dispatch_kernel(tile_ids_ref, n_tiles_ref, src_ref, x_hbm_ref, g_ref, o_ref, buf_ref, sem_ref, *, sub):
    k = pl.program_id(0)
    n_valid = n_tiles_ref[0]

    def start(step, slot):
        base = tile_ids_ref[step] * sub

        def body(r, carry):
            tok = src_ref[base + r]
            pltpu.make_async_copy(x_hbm_ref.at[pl.ds(tok, 1)], buf_ref.at[slot, pl.ds(r, 1)],
                                  sem_ref.at[slot]).start()
            return carry
        lax.fori_loop(0, sub, body, 0, unroll=8)

    @pl.when(jnp.logical_and(k == 0, n_valid > 0))
    def _():
        start(0, 0)

    @pl.when(k + 1 < n_valid)
    def _():
        start(k + 1, (k + 1) % 2)

    @pl.when(k < n_valid)
    def _():
        slot = k % 2
        pltpu.make_async_copy(x_hbm_ref.at[pl.ds(0, sub)], buf_ref.at[slot], sem_ref.at[slot]).wait()
        o_ref[...] = _rms(buf_ref[slot], g_ref[...]).astype(BF16)


def _dispatch(x, g, tile_ids, n_tiles, src, n_slots, sub):
    m, d = x.shape
    max_tiles = tile_ids.shape[0]
    return pl.pallas_call(
        functools.partial(_dispatch_kernel, sub=sub),
        grid_spec=pltpu.PrefetchScalarGridSpec(
            num_scalar_prefetch=3,
            grid=(max_tiles,),
            in_specs=[pl.BlockSpec(memory_space=pl.ANY),
                      pl.BlockSpec((1, d), lambda k, ids, nt, src: (0, 0))],
            out_specs=pl.BlockSpec((sub, d), lambda k, ids, nt, src: (ids[k], 0)),
            scratch_shapes=[pltpu.VMEM((2, sub, d), F32), pltpu.SemaphoreType.DMA((2,))]),
        out_shape=jax.ShapeDtypeStruct((n_slots, d), BF16),
        compiler_params=_params("arbitrary"),
        name="moe_dispatch",
    )(tile_ids, n_tiles, src, x, g.reshape(1, d))


def _combine_kernel(pos_ref, x_ref, gate_ref, g_ref, ys_hbm_ref, ya_ref, yb_ref, buf_ref, sem_ref, *, tm, n_a):
    i = pl.program_id(0)
    n = pl.num_programs(0)

    def start(step, slot):
        base = step * (tm * TOP_K)

        def body(r, carry):
            for kk in range(TOP_K):
                p = pos_ref[base + r * TOP_K + kk]
                pltpu.make_async_copy(ys_hbm_ref.at[pl.ds(p, 1)], buf_ref.at[slot, kk, pl.ds(r, 1)],
                                      sem_ref.at[slot]).start()
            return carry
        lax.fori_loop(0, tm, body, 0, unroll=4)

    @pl.when(i == 0)
    def _():
        start(0, 0)

    @pl.when(i + 1 < n)
    def _():
        start(i + 1, (i + 1) % 2)

    slot = i % 2
    for kk in range(TOP_K):
        pltpu.make_async_copy(ys_hbm_ref.at[pl.ds(0, tm)], buf_ref.at[slot, kk], sem_ref.at[slot]).wait()
    y = x_ref[...]
    for kk in range(TOP_K):
        y = y + gate_ref[:, kk:kk + 1] * buf_ref[slot, kk]
    y = _rms(y, g_ref[...])

    @pl.when(i < n_a)
    def _():
        ya_ref[...] = y

    @pl.when(i >= n_a)
    def _():
        yb_ref[...] = y


def _combine(x, gates, pos, ys, g, tm, m_a):
    m, d = x.shape
    n_a = m_a // tm
    assert n_a * tm == m_a and m % tm == 0
    return pl.pallas_call(
        functools.partial(_combine_kernel, tm=tm, n_a=n_a),
        grid_spec=pltpu.PrefetchScalarGridSpec(
            num_scalar_prefetch=1,
            grid=(m // tm,),
            in_specs=[pl.BlockSpec((tm, d), lambda i, pos: (i, 0)),
                      pl.BlockSpec((tm, TOP_K), lambda i, pos: (i, 0)),
                      pl.BlockSpec((1, d), lambda i, pos: (0, 0)),
                      pl.BlockSpec(memory_space=pl.ANY)],
            out_specs=[pl.BlockSpec((tm, d), lambda i, pos: (jnp.minimum(i, n_a - 1), 0)),
                       pl.BlockSpec((tm, d), lambda i, pos: (jnp.maximum(i - n_a, 0), 0))],
            scratch_shapes=[pltpu.VMEM((2, TOP_K, tm, d), F32), pltpu.SemaphoreType.DMA((2,))]),
        out_shape=[jax.ShapeDtypeStruct((m_a, d), F32), jax.ShapeDtypeStruct((m - m_a, d), F32)],
        compiler_params=_params("arbitrary"),
        name="moe_combine",
    )(pos, x, gates, g.reshape(1, d), ys)


def _moe_plan(top_i, n_exp, blk, sub, n_blocks, max_tiles):
    m = top_i.shape[0]
    e_flat = top_i.reshape(-1)
    onehot = (jnp.arange(n_exp, dtype=I32)[:, None] == e_flat[None, :]).astype(I32)
    csum = jnp.cumsum(onehot, axis=1)
    rank = jnp.sum((csum - 1) * onehot, axis=0)
    counts = csum[:, -1]
    blocks_per = (counts + blk - 1) // blk
    block_end = jnp.cumsum(blocks_per)
    block_start = block_end - blocks_per
    n_used = block_end[-1]
    pos = block_start[e_flat] * blk + rank

    s = jnp.arange(n_blocks, dtype=I32)
    bexp = jnp.minimum(jnp.sum((s[:, None] >= block_end[None, :]).astype(I32), axis=1), n_exp - 1)
    valid = jnp.clip(counts[bexp] - (s - block_start[bexp]) * blk, 0, blk)
    valid = jnp.where(s < n_used, valid, 0)

    by_expert = jnp.argsort(e_flat, stable=True).astype(I32) // TOP_K
    by_expert = jnp.concatenate([by_expert, jnp.zeros((m,), I32)])
    first_pair = jnp.cumsum(counts) - counts
    src = jnp.zeros((n_blocks * blk + m,), I32)
    for e in range(n_exp):
        run = lax.dynamic_slice(by_expert, (first_pair[e],), (m,))
        src = lax.dynamic_update_slice(src, run, (block_start[e] * blk,))
    src = src[:n_blocks * blk]
    nsub = (valid + sub - 1) // sub
    last_exp = bexp[jnp.maximum(n_used - 1, 0)]
    bexp = jnp.where(s < n_used, bexp, last_exp)

    per_blk = blk // sub
    t = jnp.arange(n_blocks * per_blk, dtype=I32)
    tile_valid = (t % per_blk) < nsub[t // per_blk]
    n_tiles = jnp.sum(tile_valid.astype(I32))
    order = jnp.argsort(jnp.logical_not(tile_valid), stable=True).astype(I32)[:max_tiles]
    last_tile = order[jnp.maximum(n_tiles - 1, 0)]
    tile_ids = jnp.where(jnp.arange(max_tiles, dtype=I32) < n_tiles, order, last_tile)
    return pos, src, bexp, nsub, n_used.reshape(1), tile_ids, n_tiles.reshape(1)


def _moe(x, top_i, gates, g_ffn, g_final, w1, w3, w2, blk, sub, tf, tm_c, m_a):
    m, d = x.shape
    n_exp = w1.shape[0]
    n_blocks = (m * TOP_K + n_exp * (blk - 1)) // blk
    max_tiles = (m * TOP_K + n_exp * (sub - 1)) // sub
    pos, src, bexp, nsub, nused, tile_ids, n_tiles = _moe_plan(top_i, n_exp, blk, sub, n_blocks, max_tiles)
    xs = _dispatch(x, g_ffn, tile_ids, n_tiles, src, n_blocks * blk, sub)
    n_main = min(n_blocks, n_exp * (-(-(m * TOP_K) // (n_exp * blk))) + 2)
    ys = _experts(xs, w1, w3, w2, bexp, nsub, nused, blk, sub, tf, 0, n_main)
    if n_main < n_blocks:
        ys = lax.cond(
            nused[0] > n_main,
            lambda y: _experts(xs, w1, w3, w2, bexp, nsub, nused, blk, sub, tf, n_main, n_blocks - n_main, ys=y),
            lambda y: y, ys)
    return _combine(x, gates, pos, ys, g_final, tm_c, m_a)


def kernel(x_prompt, x_sample, state_conv_a, state_conv_c, state_pool_d, norm_mix_g, norm_ffn_g, norm_final_g, w_in_ab, conv_a, ln_b_g, ln_b_b, w_s, b_s, w_out_ab, ffn_w1, ffn_w3, ffn_w2, w_in_cd, conv_c, conv_c_b, ln_c_g, ln_c_b, pool_w, pool_scale, w_out_cd, router, exp_w1, exp_w3, exp_w2):
    n_seq, seq_len, d = x_prompt.shape
    nseq_s, dl, _ = x_sample.shape
    mp, ms = n_seq * seq_len, nseq_s * dl
    m = mp + ms
    assert norm_mix_g.shape[0] == 2, "two layers: conv/chunk-MLP + dense FFN, then conformer/pool + MoE"
    assert seq_len % CHUNK == 0 and dl <= CHUNK and mp % ms == 0

    big = m * TOP_K >= 4 * 2560
    tm_in = _pick(m, (2176, 1088, 544, 272))
    tm_mid = _pick(m, (1088, 544, 272))
    tm_row = _pick(ms, (512, 256, 128, 64))
    t_rows = _pick(seq_len, (256, 128))
    tn = 1024 if big else 512
    tf_up = _pick(ffn_w1.shape[2], (512, 256))
    tn_down = 256
    tf_e = 256
    blk_e, sub_e = (2560, 256) if big else (512, 128)
    tm_c = _pick(ms, (256, 128, 64))

    xp = x_prompt.reshape(mp, d)
    xs_lm = jnp.transpose(x_sample, (1, 0, 2)).reshape(ms, d)

    h = _norm_in(xp, xs_lm, norm_mix_g[0], tm_row)
    proj = _matmul(h, w_in_ab[0], tm_in, tn)
    mixed, a_p, v_p = _mix_ab_prompt(proj, conv_a[0], ln_b_g[0], ln_b_b[0], w_s[0], b_s[0],
                                     n_seq, seq_len, t_rows)
    mixed, a_s, v_s = _mix_ab_sample(mixed, proj, jnp.transpose(state_conv_a[0], (1, 0, 2)),
                                     conv_a[0], ln_b_g[0], ln_b_b[0], w_s[0], b_s[0], mp, dl, nseq_s)
    x, h = _out_proj(mixed, w_out_ab[0].astype(BF16), xp, xs_lm, norm_ffn_g[0], tm_row)
    act = _ffn_up(h, ffn_w1, ffn_w3, tm_in, tf_up)
    x = _ffn_down(act, ffn_w2, x, tm_mid, tn_down)

    h = _norm_rows(x, norm_mix_g[1], tm_mid)
    proj = _matmul(h, w_in_cd[0], tm_in, tn)
    mixed, c_p, d_p = _mix_cd_prompt(proj, conv_c[0], conv_c_b[0], ln_c_g[0], ln_c_b[0], pool_w[0],
                                     pool_scale[0], n_seq, seq_len, t_rows)
    mixed, g_s = _mix_cd_sample(mixed, proj, jnp.transpose(state_conv_c[0], (1, 0, 2)),
                                jnp.transpose(state_pool_d[0], (1, 0, 2)), conv_c[0], conv_c_b[0],
                                ln_c_g[0], ln_c_b[0], pool_w[0], pool_scale[0], mp, dl, nseq_s)
    (x,) = _out_proj(mixed, w_out_cd[0].astype(BF16), x, x, None, tm_row)
    top_i, gates = _router(x, norm_ffn_g[1], router[0], tm_mid)
    y_p, y_s = _moe(x, top_i, gates, norm_ffn_g[1], norm_final_g, exp_w1[0], exp_w3[0], exp_w2[0],
                    blk_e, sub_e, tf_e, tm_c, mp)

    dc = conv_c.shape[2]
    y_prompt = y_p.reshape(n_seq, seq_len, d)
    y_sample = jnp.transpose(y_s.reshape(dl, nseq_s, d), (1, 0, 2))
    new_a_s = jnp.transpose(a_s, (1, 0, 2))
    new_v_s = jnp.transpose(v_s.reshape(dl, nseq_s, -1), (1, 0, 2))
    g_rows = jnp.transpose(g_s.reshape(dl, nseq_s, dc), (1, 0, 2))
    p_rows = jnp.transpose(proj[mp:, 2 * dc:].astype(F32).reshape(dl, nseq_s, -1), (1, 0, 2))
    n_c = state_conv_c.shape[2]
    n_d = state_pool_d.shape[2]
    new_c_s = jnp.concatenate([state_conv_c[0], g_rows], axis=1)[:, -n_c:]
    new_d_s = jnp.concatenate([state_pool_d[0], p_rows], axis=1)[:, -n_d:]
    return (y_prompt, y_sample,
            a_p[None], new_a_s[None],
            v_p[None], new_v_s[None],
            c_p[None], new_c_s[None],
            d_p[None], new_d_s[None])
```

```python
import functools

import jax
import jax.numpy as jnp
from jax import lax
from jax.experimental import pallas as pl
from jax.experimental.pallas import tpu as pltpu

F32 = jnp.float32
BF16 = jnp.bfloat16
I32 = jnp.int32

EPS = 1e-6
PAST_LEN = 16384
CHUNK = 128
HEAD_B = 128
POOL_WINDOWS = (2, 4, 8, 16)
TOP_K = 2

VMEM_LIMIT_BYTES = 60 * 1024 * 1024
SUBLANES = 8
LANES = 128


def _params(*sem):
    return pltpu.CompilerParams(dimension_semantics=sem, vmem_limit_bytes=VMEM_LIMIT_BYTES)


def _pick(n, candidates):
    for c in candidates:
        if n % c == 0:
            return c
    raise ValueError(f"no tile in {candidates} divides {n}")


def _rms(x, g):
    return x * lax.rsqrt(jnp.mean(x * x, axis=-1, keepdims=True) + EPS) * g


def _layernorm(x, g, b):
    mu = jnp.mean(x, axis=-1, keepdims=True)
    xc = x - mu
    return xc * lax.rsqrt(jnp.mean(xc * xc, axis=-1, keepdims=True) + EPS) * g + b


def _silu(x):
    return x * jax.nn.sigmoid(x)


def _dot(a, b):
    return jnp.dot(a, b, preferred_element_type=F32)


def _f32(x):
    return x.astype(F32)


def _mix_ab_prompt_kernel(bg_ref, cg_ref, xin_ref, u_ref, v_ref, cgh_ref, xinh_ref,
                          cw_ref, lng_ref, lnb_ref, ws_ref, bsb_ref,
                          o_ref, astate_ref, vstate_ref, abuf_ref):
    t = pl.program_id(1)
    t_rows, da = bg_ref.shape
    n_heads = ws_ref.shape[0]
    halo = cgh_ref.shape[0]

    a = _f32(cg_ref[...]) * _f32(xin_ref[...])
    prev = jnp.where(t == 0, 0.0, _f32(cgh_ref[...]) * _f32(xinh_ref[...]))
    abuf_ref[0:halo, :] = prev
    abuf_ref[halo:halo + t_rows, :] = a
    conv = (cw_ref[0:1, :] * abuf_ref[halo - 2:halo - 2 + t_rows, :]
            + cw_ref[1:2, :] * abuf_ref[halo - 1:halo - 1 + t_rows, :]
            + cw_ref[2:3, :] * a)
    o_ref[:, 0:da] = (_f32(bg_ref[...]) * conv).astype(BF16)
    astate_ref[0] = abuf_ref[halo + t_rows - 2:halo + t_rows, :]

    row = lax.broadcasted_iota(I32, (CHUNK, CHUNK), 0)
    col = lax.broadcasted_iota(I32, (CHUNK, CHUNK), 1)
    lower = row >= col
    for h in range(n_heads):
        cols = slice(h * HEAD_B, (h + 1) * HEAD_B)
        vn = _layernorm(jax.nn.gelu(_f32(v_ref[:, cols])), lng_ref[:, cols], lnb_ref[:, cols])
        vstate_ref[0, :, cols] = vn[t_rows - CHUNK:t_rows]
        w = jnp.where(lower, ws_ref[h], 0.0).astype(BF16)
        for c in range(t_rows // CHUNK):
            rows = slice(c * CHUNK, (c + 1) * CHUNK)
            mix = _dot(w, vn[rows].astype(BF16)) + bsb_ref[h]
            o_ref[rows, da + h * HEAD_B:da + (h + 1) * HEAD_B] = (
                jax.nn.gelu(_f32(u_ref[rows, cols])) * mix).astype(BF16)


def _mix_ab_prompt(proj, conv_a, ln_g, ln_b, w_s, b_s, n_seq, seq_len, t_rows):
    m = proj.shape[0]
    da = conv_a.shape[1]
    db = ln_g.shape[0]
    n_heads = w_s.shape[0]
    tiles = seq_len // t_rows
    halo_rows = 2 * SUBLANES
    halo_blocks = t_rows // halo_rows

    def col_spec(c):
        return pl.BlockSpec((t_rows, da), lambda b, t: (b * tiles + t, c))

    def halo_spec(c):
        return pl.BlockSpec(
            (halo_rows, da),
            lambda b, t: (jnp.maximum((b * tiles + t) * halo_blocks - 1, 0), c))

    def full(shape):
        return pl.BlockSpec(shape, lambda b, t: (0,) * len(shape))

    bsb = jnp.broadcast_to(b_s[:, :CHUNK, None], (n_heads, CHUNK, HEAD_B))
    return pl.pallas_call(
        _mix_ab_prompt_kernel,
        grid=(n_seq, tiles),
        in_specs=[col_spec(0), col_spec(1), col_spec(2), col_spec(3), col_spec(4),
                  halo_spec(1), halo_spec(2),
                  full((conv_a.shape[0], da)), full((1, db)), full((1, db)),
                  full((n_heads, CHUNK, CHUNK)), full((n_heads, CHUNK, HEAD_B))],
        out_specs=[pl.BlockSpec((t_rows, da + db), lambda b, t: (b * tiles + t, 0)),
                   pl.BlockSpec((1, 2, da), lambda b, t: (b, 0, 0)),
                   pl.BlockSpec((1, CHUNK, db), lambda b, t: (b, 0, 0))],
        out_shape=[jax.ShapeDtypeStruct((m, da + db), BF16),
                   jax.ShapeDtypeStruct((n_seq, 2, da), F32),
                   jax.ShapeDtypeStruct((n_seq, CHUNK, db), F32)],
        scratch_shapes=[pltpu.VMEM((halo_rows + t_rows, da), F32)],
        compiler_params=_params("parallel", "arbitrary"),
        name="mix_ab_prompt",
    )(proj, proj, proj, proj, proj, proj, proj, conv_a, ln_g.reshape(1, db), ln_b.reshape(1, db),
      w_s[:, :CHUNK, :CHUNK], bsb)


def _mix_ab_sample_kernel(mixed_hbm_ref, bg_ref, cg_ref, xin_ref, u_ref, v_ref, st_ref,
                          cw_ref, lng_ref, lnb_ref, wsl_ref, bsl_ref,
                          o_ref, astate_ref, vstate_ref, *, dl, nseq):
    del mixed_hbm_ref
    da = bg_ref.shape[1]
    n_heads = lng_ref.shape[1] // HEAD_B
    n_state = st_ref.shape[0]

    def slab(l):
        return slice(l * nseq, (l + 1) * nseq)

    a_pad = [st_ref[k] for k in range(n_state)]
    a_pad += [_f32(cg_ref[slab(l), :]) * _f32(xin_ref[slab(l), :]) for l in range(dl)]
    for l in range(dl):
        conv = (cw_ref[0:1, :] * a_pad[l] + cw_ref[1:2, :] * a_pad[l + 1]
                + cw_ref[2:3, :] * a_pad[l + 2])
        o_ref[slab(l), 0:da] = (_f32(bg_ref[slab(l), :]) * conv).astype(BF16)
    for k in range(n_state):
        astate_ref[k] = a_pad[dl + k]

    for h in range(n_heads):
        cols = slice(h * HEAD_B, (h + 1) * HEAD_B)
        vn = _layernorm(jax.nn.gelu(_f32(v_ref[:, cols])), lng_ref[:, cols], lnb_ref[:, cols])
        vstate_ref[:, cols] = vn
        for i in range(dl):
            mix = bsl_ref[i:i + 1, cols]
            for j in range(i + 1):
                mix = mix + wsl_ref[i * dl + j:i * dl + j + 1, cols] * vn[slab(j)]
            o_ref[slab(i), da + h * HEAD_B:da + (h + 1) * HEAD_B] = (
                jax.nn.gelu(_f32(u_ref[slab(i), cols])) * mix).astype(BF16)


def _mix_ab_sample(mixed, proj, state_lm, conv_a, ln_g, ln_b, w_s, b_s, row0, dl, nseq):
    ms = dl * nseq
    da = conv_a.shape[1]
    db = ln_g.shape[0]
    n_heads = w_s.shape[0]
    rb = row0 // ms

    def col_spec(c):
        return pl.BlockSpec((ms, da), lambda i: (rb, c))

    def full(shape):
        return pl.BlockSpec(shape, lambda i: (0,) * len(shape))

    wsl = jnp.repeat(jnp.transpose(w_s[:, :dl, :dl], (1, 2, 0)).reshape(dl * dl, n_heads), HEAD_B, axis=1)
    bsl = jnp.repeat(jnp.transpose(b_s[:, :dl], (1, 0)), HEAD_B, axis=1)
    return pl.pallas_call(
        functools.partial(_mix_ab_sample_kernel, dl=dl, nseq=nseq),
        grid=(1,),
        in_specs=[pl.BlockSpec(memory_space=pl.ANY),
                  col_spec(0), col_spec(1), col_spec(2), col_spec(3), col_spec(4),
                  full(state_lm.shape), full((conv_a.shape[0], da)), full((1, db)), full((1, db)),
                  full((dl * dl, db)), full((dl, db))],
        out_specs=[pl.BlockSpec((ms, da + db), lambda i: (rb, 0)),
                   full(state_lm.shape),
                   full((ms, db))],
        out_shape=[jax.ShapeDtypeStruct(mixed.shape, BF16),
                   jax.ShapeDtypeStruct(state_lm.shape, F32),
                   jax.ShapeDtypeStruct((ms, db), F32)],
        input_output_aliases={0: 0},
        compiler_params=_params("arbitrary"),
        name="mix_ab_sample",
    )(mixed, proj, proj, proj, proj, proj, state_lm, conv_a, ln_g.reshape(1, db), ln_b.reshape(1, db),
      wsl, bsl)


def _mix_cd_prompt_kernel(ga_ref, gb_ref, p_ref, gah_ref, gbh_ref, ph_ref,
                          cw_ref, cb_ref, lng_ref, lnb_ref, pw_ref, ps_ref,
                          o_ref, cstate_ref, dstate_ref, gbuf_ref, pbuf_ref, cbuf_ref, shift_ref):
    t = pl.program_id(1)
    t_rows, dc = ga_ref.shape
    k_c = cw_ref.shape[0]
    c_halo = gah_ref.shape[0]
    d_halo = ph_ref.shape[0]
    gd = pw_ref.shape[1]

    g = _f32(ga_ref[...]) * jax.nn.sigmoid(_f32(gb_ref[...]))
    gbuf_ref[0:c_halo, :] = jnp.where(
        t == 0, 0.0, _f32(gah_ref[...]) * jax.nn.sigmoid(_f32(gbh_ref[...])))
    gbuf_ref[c_halo:c_halo + t_rows, :] = g
    n_shift = shift_ref.shape[1]
    for r in range(1, SUBLANES):
        shift_ref[r - 1] = gbuf_ref[r:r + n_shift, :]
    first = c_halo - (k_c - 1)
    for cblk in range(dc // LANES):
        cols = slice(cblk * LANES, (cblk + 1) * LANES)
        acc = cb_ref[:, cols]
        for k in range(k_c):
            q, r = divmod(first + k, SUBLANES)
            if r == 0:
                win = gbuf_ref[first + k:first + k + t_rows, cols]
            else:
                win = shift_ref[r - 1, q * SUBLANES:q * SUBLANES + t_rows, cols]
            acc = acc + cw_ref[k:k + 1, cols] * win
        cbuf_ref[:, cols] = acc
    o_ref[:, 0:dc] = _silu(_layernorm(cbuf_ref[...], lng_ref[...], lnb_ref[...])).astype(BF16)
    cstate_ref[0] = gbuf_ref[c_halo + t_rows - (k_c - 1):c_halo + t_rows, :]

    pbuf_ref[0:d_halo, :] = jnp.where(t == 0, 0.0, _f32(ph_ref[...]))
    pbuf_ref[d_halo:d_halo + t_rows, :] = _f32(p_ref[...])
    pos = t * t_rows + lax.broadcasted_iota(I32, (t_rows, 1), 0)
    for gi, w in enumerate(POOL_WINDOWS):
        cols = slice(gi * gd, (gi + 1) * gd)
        cur = pbuf_ref[d_halo:d_halo + t_rows, cols]
        s = cur
        for k in range(1, w):
            s = s + pbuf_ref[d_halo - k:d_halo - k + t_rows, cols]
        cnt = jnp.minimum(w, pos + 1).astype(F32)
        diff = (s / cnt - cur).astype(BF16)
        d = _dot(diff, pw_ref[gi].astype(BF16)) * ps_ref[:, cols]
        o_ref[:, dc + gi * gd:dc + (gi + 1) * gd] = d.astype(BF16)
    n_keep = dstate_ref.shape[1]
    dstate_ref[0] = pbuf_ref[d_halo + t_rows - n_keep:d_halo + t_rows, :]


def _mix_cd_prompt(proj, conv_c, conv_c_b, ln_g, ln_b, pool_w, pool_scale, n_seq, seq_len, t_rows):
    m = proj.shape[0]
    k_c, dc = conv_c.shape
    dd = pool_scale.shape[0]
    n_pool, gd = pool_w.shape[0], pool_w.shape[1]
    tiles = seq_len // t_rows
    c_halo = -(-(k_c - 1) // SUBLANES) * SUBLANES
    d_halo = -(-(max(POOL_WINDOWS) - 1) // SUBLANES) * SUBLANES
    assert t_rows % c_halo == 0 and t_rows % d_halo == 0 and dc == dd

    def col_spec(c):
        return pl.BlockSpec((t_rows, dc), lambda b, t: (b * tiles + t, c))

    def halo_spec(c, rows):
        per = t_rows // rows
        return pl.BlockSpec((rows, dc),
                            lambda b, t: (jnp.maximum((b * tiles + t) * per - 1, 0), c))

    def full(shape):
        return pl.BlockSpec(shape, lambda b, t: (0,) * len(shape))

    return pl.pallas_call(
        _mix_cd_prompt_kernel,
        grid=(n_seq, tiles),
        in_specs=[col_spec(0), col_spec(1), col_spec(2),
                  halo_spec(0, c_halo), halo_spec(1, c_halo), halo_spec(2, d_halo),
                  full((k_c, dc)), full((1, dc)), full((1, dc)), full((1, dc)),
                  full((n_pool, gd, gd)), full((1, dd))],
        out_specs=[pl.BlockSpec((t_rows, dc + dd), lambda b, t: (b * tiles + t, 0)),
                   pl.BlockSpec((1, k_c - 1, dc), lambda b, t: (b, 0, 0)),
                   pl.BlockSpec((1, max(POOL_WINDOWS) - 1, dd), lambda b, t: (b, 0, 0))],
        out_shape=[jax.ShapeDtypeStruct((m, dc + dd), BF16),
                   jax.ShapeDtypeStruct((n_seq, k_c - 1, dc), F32),
                   jax.ShapeDtypeStruct((n_seq, max(POOL_WINDOWS) - 1, dd), F32)],
        scratch_shapes=[pltpu.VMEM((c_halo + t_rows, dc), F32),
                        pltpu.VMEM((d_halo + t_rows, dd), F32),
                        pltpu.VMEM((t_rows, dc), F32),
                        pltpu.VMEM((SUBLANES - 1, c_halo + t_rows - SUBLANES, dc), F32)],
        compiler_params=_params("parallel", "arbitrary"),
        name="mix_cd_prompt",
    )(proj, proj, proj, proj, proj, proj, conv_c, conv_c_b.reshape(1, dc), ln_g.reshape(1, dc),
      ln_b.reshape(1, dc), pool_w, pool_scale.reshape(1, dd))


def _mix_cd_sample_kernel(mixed_hbm_ref, ga_ref, gb_ref, p_ref, stc_ref, std_ref,
                          cw_ref, cb_ref, lng_ref, lnb_ref, pw_ref, ps_ref,
                          o_ref, g_ref, cbuf_ref, dbuf_ref, *, dl, nseq, start_pos):
    del mixed_hbm_ref
    gi = pl.program_id(0)
    n_groups = pl.num_programs(0)
    gd = ga_ref.shape[1]
    k_c = cw_ref.shape[0]
    n_c = stc_ref.shape[0]
    n_d = std_ref.shape[0]

    def slab(l):
        return slice(l * nseq, (l + 1) * nseq)

    g = _f32(ga_ref[...]) * jax.nn.sigmoid(_f32(gb_ref[...]))
    g_ref[...] = g
    c_pad = [stc_ref[k] for k in range(n_c)] + [g[slab(l)] for l in range(dl)]
    d_pad = [std_ref[k] for k in range(n_d)] + [_f32(p_ref[slab(l), :]) for l in range(dl)]
    for l in range(dl):
        acc = cb_ref[...] + cw_ref[0:1, :] * c_pad[l]
        for k in range(1, k_c):
            acc = acc + cw_ref[k:k + 1, :] * c_pad[l + k]
        cbuf_ref[gi, slab(l), :] = acc

        cur = d_pad[n_d + l]
        s = cur
        pooled = jnp.zeros_like(cur)
        for k in range(1, max(POOL_WINDOWS) + 1):
            if k in POOL_WINDOWS:
                cnt = float(min(k, start_pos + l + 1))
                pooled = jnp.where(gi == POOL_WINDOWS.index(k), s / cnt, pooled)
            if k < max(POOL_WINDOWS):
                s = s + d_pad[n_d + l - k]
        diff = (pooled - cur).astype(BF16)
        dbuf_ref[gi, slab(l), :] = (_dot(diff, pw_ref[0].astype(BF16)) * ps_ref[...]).astype(BF16)

    @pl.when(gi == n_groups - 1)
    def _():
        n_blk = cbuf_ref.shape[0]
        dc = n_blk * gd
        total = cbuf_ref[0].sum(axis=-1, keepdims=True)
        for b in range(1, n_blk):
            total = total + cbuf_ref[b].sum(axis=-1, keepdims=True)
        mu = total / dc
        var = None
        for b in range(n_blk):
            xc = cbuf_ref[b] - mu
            sq = (xc * xc).sum(axis=-1, keepdims=True)
            var = sq if var is None else var + sq
        rstd = lax.rsqrt(var / dc + EPS)
        for b in range(n_blk):
            cols = slice(b * gd, (b + 1) * gd)
            y = (cbuf_ref[b] - mu) * rstd * lng_ref[:, cols] + lnb_ref[:, cols]
            o_ref[:, cols] = _silu(y).astype(BF16)
            o_ref[:, dc + b * gd:dc + (b + 1) * gd] = dbuf_ref[b]


def _mix_cd_sample(mixed, proj, stc_lm, std_lm, conv_c, conv_c_b, ln_g, ln_b, pool_w, pool_scale,
                   row0, dl, nseq):
    ms = dl * nseq
    k_c, dc = conv_c.shape
    dd = pool_scale.shape[0]
    n_pool, gd = pool_w.shape[0], pool_w.shape[1]
    assert dc == dd == n_pool * gd and n_pool == len(POOL_WINDOWS)
    rb = row0 // ms

    return pl.pallas_call(
        functools.partial(_mix_cd_sample_kernel, dl=dl, nseq=nseq, start_pos=PAST_LEN),
        grid=(n_pool,),
        in_specs=[pl.BlockSpec(memory_space=pl.ANY),
                  pl.BlockSpec((ms, gd), lambda i: (rb, i)),
                  pl.BlockSpec((ms, gd), lambda i: (rb, n_pool + i)),
                  pl.BlockSpec((ms, gd), lambda i: (rb, 2 * n_pool + i)),
                  pl.BlockSpec((stc_lm.shape[0], nseq, gd), lambda i: (0, 0, i)),
                  pl.BlockSpec((std_lm.shape[0], nseq, gd), lambda i: (0, 0, i)),
                  pl.BlockSpec((k_c, gd), lambda i: (0, i)),
                  pl.BlockSpec((1, gd), lambda i: (0, i)),
                  pl.BlockSpec((1, dc), lambda i: (0, 0)),
                  pl.BlockSpec((1, dc), lambda i: (0, 0)),
                  pl.BlockSpec((1, gd, gd), lambda i: (i, 0, 0)),
                  pl.BlockSpec((1, gd), lambda i: (0, i))],
        out_specs=[pl.BlockSpec((ms, dc + dd), lambda i: (rb, 0)),
                   pl.BlockSpec((ms, gd), lambda i: (0, i))],
        out_shape=[jax.ShapeDtypeStruct(mixed.shape, BF16),
                   jax.ShapeDtypeStruct((ms, dc), F32)],
        scratch_shapes=[pltpu.VMEM((n_pool, ms, gd), F32),
                        pltpu.VMEM((n_pool, ms, gd), BF16)],
        input_output_aliases={0: 0},
        compiler_params=_params("arbitrary"),
        name="mix_cd_sample",
    )(mixed, proj, proj, proj, stc_lm, std_lm, conv_c, conv_c_b.reshape(1, dc), ln_g.reshape(1, dc),
      ln_b.reshape(1, dc), pool_w, pool_scale.reshape(1, dd))


def _norm_in_kernel(xp_ref, xs_ref, g_ref, o_ref, *, n_a):
    i = pl.program_id(0)

    @pl.when(i < n_a)
    def _():
        o_ref[...] = _rms(xp_ref[...], g_ref[...]).astype(BF16)

    @pl.when(i >= n_a)
    def _():
        o_ref[...] = _rms(xs_ref[...], g_ref[...]).astype(BF16)


def _norm_in(xp, xs, g, tm):
    mp, d = xp.shape
    ms = xs.shape[0]
    n_a, n_b = mp // tm, ms // tm
    return pl.pallas_call(
        functools.partial(_norm_in_kernel, n_a=n_a),
        grid=(n_a + n_b,),
        in_specs=[pl.BlockSpec((tm, d), lambda i: (jnp.minimum(i, n_a - 1), 0)),
                  pl.BlockSpec((tm, d), lambda i: (jnp.maximum(i - n_a, 0), 0)),
                  pl.BlockSpec((1, d), lambda i: (0, 0))],
        out_specs=pl.BlockSpec((tm, d), lambda i: (i, 0)),
        out_shape=jax.ShapeDtypeStruct((mp + ms, d), BF16),
        compiler_params=_params("arbitrary"),
        name="norm_in",
    )(xp, xs, g.reshape(1, d))


def _matmul_kernel(a_ref, w_ref, o_ref):
    o_ref[...] = _dot(a_ref[...], w_ref[...].astype(BF16)).astype(o_ref.dtype)


def _matmul(a, w, tm, tn):
    m, k = a.shape
    n = w.shape[1]
    return pl.pallas_call(
        _matmul_kernel,
        grid=(m // tm, n // tn),
        in_specs=[pl.BlockSpec((tm, k), lambda i, j: (i, 0)),
                  pl.BlockSpec((k, tn), lambda i, j: (0, j))],
        out_specs=pl.BlockSpec((tm, tn), lambda i, j: (i, j)),
        out_shape=jax.ShapeDtypeStruct((m, n), BF16),
        compiler_params=_params("parallel", "arbitrary"),
        name="in_proj",
    )(a, w)


def _ffn_up_kernel(h_ref, w1_ref, w3_ref, o_ref):
    w1 = w1_ref[0].astype(BF16)
    w3 = w3_ref[0].astype(BF16)
    tm = h_ref.shape[0]
    n_parts = 2 if tm % (2 * 2 * SUBLANES) == 0 else 1
    rows_per = tm // n_parts
    for part in range(n_parts):
        rows = slice(part * rows_per, (part + 1) * rows_per)
        h = h_ref[rows, :]
        o_ref[rows, :] = (_silu(_dot(h, w1)) * _dot(h, w3)).astype(BF16)


def _ffn_up(h, w1, w3, tm, tf):
    m, d = h.shape
    f = w1.shape[2]
    return pl.pallas_call(
        _ffn_up_kernel,
        grid=(m // tm, f // tf),
        in_specs=[pl.BlockSpec((tm, d), lambda i, j: (i, 0)),
                  pl.BlockSpec((1, d, tf), lambda i, j: (0, 0, j)),
                  pl.BlockSpec((1, d, tf), lambda i, j: (0, 0, j))],
        out_specs=pl.BlockSpec((tm, tf), lambda i, j: (i, j)),
        out_shape=jax.ShapeDtypeStruct((m, f), BF16),
        compiler_params=_params("parallel", "arbitrary"),
        name="ffn_up",
    )(h, w1, w3)


def _ffn_down_kernel(a_ref, w_ref, r_ref, o_ref):
    o_ref[...] = r_ref[...] + _dot(a_ref[...], w_ref[0].astype(BF16))


def _ffn_down(act, w2, res, tm, tn):
    m, f = act.shape
    d = w2.shape[2]
    return pl.pallas_call(
        _ffn_down_kernel,
        grid=(m // tm, d // tn),
        in_specs=[pl.BlockSpec((tm, f), lambda i, j: (i, 0)),
                  pl.BlockSpec((1, f, tn), lambda i, j: (0, 0, j)),
                  pl.BlockSpec((tm, tn), lambda i, j: (i, j))],
        out_specs=pl.BlockSpec((tm, tn), lambda i, j: (i, j)),
        out_shape=jax.ShapeDtypeStruct((m, d), F32),
        compiler_params=_params("parallel", "arbitrary"),
        name="ffn_down",
    )(act, w2, res)


def _norm_rows_kernel(x_ref, g_ref, o_ref):
    o_ref[...] = _rms(x_ref[...], g_ref[...]).astype(BF16)


def _norm_rows(x, g, tm):
    m, d = x.shape
    return pl.pallas_call(
        _norm_rows_kernel,
        grid=(m // tm,),
        in_specs=[pl.BlockSpec((tm, d), lambda i: (i, 0)),
                  pl.BlockSpec((1, d), lambda i: (0, 0))],
        out_specs=pl.BlockSpec((tm, d), lambda i: (i, 0)),
        out_shape=jax.ShapeDtypeStruct((m, d), BF16),
        compiler_params=_params("parallel"),
        name="norm_rows",
    )(x, g.reshape(1, d))


def _out_proj_kernel(a_ref, w_ref, ra_ref, rb_ref, *rest, n_a, with_h):
    i = pl.program_id(0)
    if with_h:
        g_ref, x_ref, h_ref = rest
    else:
        (x_ref,) = rest
    tm = a_ref.shape[0]
    n_parts = 2 if tm % (2 * 2 * SUBLANES) == 0 else 1
    rows_per = tm // n_parts
    for part in range(n_parts):
        rows = slice(part * rows_per, (part + 1) * rows_per)
        res = jnp.where(i < n_a, ra_ref[rows, :], rb_ref[rows, :])
        x = res + _dot(a_ref[rows, :], w_ref[...])
        x_ref[rows, :] = x
        if with_h:
            h_ref[rows, :] = _rms(x, g_ref[...]).astype(BF16)


def _out_proj(a, w_bf16, res_a, res_b, g, tm):
    m, k = a.shape
    d = w_bf16.shape[1]
    n_a = res_a.shape[0] // tm
    n_b = m // tm - n_a
    assert n_a * tm == res_a.shape[0] and (n_b == 0 or n_b * tm == res_b.shape[0])
    with_h = g is not None
    in_specs = [pl.BlockSpec((tm, k), lambda i: (i, 0)),
                pl.BlockSpec((k, d), lambda i: (0, 0), pipeline_mode=pl.Buffered(1)),
                pl.BlockSpec((tm, d), lambda i: (jnp.minimum(i, n_a - 1), 0)),
                pl.BlockSpec((tm, d), lambda i: (jnp.maximum(i - n_a, 0), 0))]
    args = [a, w_bf16, res_a, res_b]
    row_spec = pl.BlockSpec((tm, d), lambda i: (i, 0))
    out_specs = [row_spec]
    out_shape = [jax.ShapeDtypeStruct((m, d), F32)]
    if with_h:
        in_specs.append(pl.BlockSpec((1, d), lambda i: (0, 0)))
        args.append(g.reshape(1, d))
        out_specs.append(row_spec)
        out_shape.append(jax.ShapeDtypeStruct((m, d), BF16))
    return pl.pallas_call(
        functools.partial(_out_proj_kernel, n_a=n_a, with_h=with_h),
        grid=(m // tm,),
        in_specs=in_specs, out_specs=out_specs, out_shape=out_shape,
        compiler_params=_params("arbitrary"),
        name="out_proj_h" if with_h else "out_proj",
    )(*args)


def _router_kernel(x_ref, g_ref, r_ref, idx_ref, gate_ref):
    h = _rms(x_ref[...], g_ref[...])
    h_hi = h.astype(BF16)
    h_lo = (h - h_hi.astype(F32)).astype(BF16)
    r = r_ref[...]
    r_hi = r.astype(BF16)
    r_lo = (r - r_hi.astype(F32)).astype(BF16)
    logits = _dot(h_hi, r_hi) + (_dot(h_hi, r_lo) + _dot(h_lo, r_hi))
    n_exp = float(logits.shape[1])
    lane = lax.broadcasted_iota(I32, logits.shape, 1).astype(F32)
    m1 = jnp.max(logits, axis=-1, keepdims=True)
    i1 = jnp.min(jnp.where(logits == m1, lane, n_exp), axis=-1, keepdims=True)
    rest = jnp.where(lane == i1, -jnp.inf, logits)
    m2 = jnp.max(rest, axis=-1, keepdims=True)
    i2 = jnp.min(jnp.where(rest == m2, lane, n_exp), axis=-1, keepdims=True)
    e = jnp.exp(m2 - m1)
    two = lax.broadcasted_iota(I32, idx_ref.shape, 1)
    idx_ref[...] = jnp.where(two == 0, i1, i2).astype(I32)
    gate_ref[...] = jnp.where(two == 0, 1.0, e) / (1.0 + e)


def _router(x, g, router, tm):
    m, d = x.shape
    n_exp = router.shape[1]
    return pl.pallas_call(
        _router_kernel,
        grid=(m // tm,),
        in_specs=[pl.BlockSpec((tm, d), lambda i: (i, 0)),
                  pl.BlockSpec((1, d), lambda i: (0, 0)),
                  pl.BlockSpec((d, n_exp), lambda i: (0, 0))],
        out_specs=[pl.BlockSpec((tm, TOP_K), lambda i: (i, 0)),
                   pl.BlockSpec((tm, TOP_K), lambda i: (i, 0))],
        out_shape=[jax.ShapeDtypeStruct((m, TOP_K), I32),
                   jax.ShapeDtypeStruct((m, TOP_K), F32)],
        compiler_params=_params("parallel"),
        name="router",
    )(x, g.reshape(1, d), router)


def _experts_kernel(bexp_ref, nsub_ref, nused_ref, *refs, sub, s0):
    if s0:
        refs = refs[1:]
    xs_ref, w1_ref, w3_ref, w2_ref, ys_hbm_ref, o_ref, w1b_ref, w3b_ref, w2b_ref, sem_ref = refs
    s = pl.program_id(0) + s0
    j = pl.program_id(1)
    last_j = j == pl.num_programs(1) - 1
    blk = o_ref.shape[0]

    def flush(first, count):
        r0 = pl.multiple_of(first * sub, sub)
        g0 = pl.multiple_of(s * blk, sub) + r0
        return pltpu.make_async_copy(o_ref.at[pl.ds(r0, count * sub)],
                                     ys_hbm_ref.at[pl.ds(g0, count * sub)], sem_ref.at[0])

    def rows_of(i):
        return pl.ds(pl.multiple_of(i * sub, sub), sub)

    def group(first, count, w1v, w3v, w2v):
        acts = []
        for t in range(count):
            x = xs_ref[rows_of(first + t), :]
            acts.append((_silu(_dot(x, w1v)) * _dot(x, w3v)).astype(BF16))
        for t in range(count):
            o_ref[rows_of(first + t), :] += _dot(acts[t], w2v)

        @pl.when(last_j)
        def _():
            flush(first, count).start()

    def from_scratch(first, count):
        group(first, count, w1b_ref[...], w3b_ref[...], w2b_ref[...])

    @pl.when(s < nused_ref[0])
    def _():
        n = nsub_ref[s]

        @pl.when(j == 0)
        def _():
            def init(i, carry):
                o_ref[rows_of(i), :] = jnp.zeros((sub, o_ref.shape[1]), F32)
                return carry
            lax.fori_loop(0, n, init, 0)

        @pl.when(n >= 3)
        def _():
            w1v = w1_ref[0].astype(BF16)
            w3v = w3_ref[0].astype(BF16)
            w2v = w2_ref[0].astype(BF16)
            w1b_ref[...] = w1v
            w3b_ref[...] = w3v
            w2b_ref[...] = w2v
            group(0, 3, w1v, w3v, w2v)

        @pl.when(n < 3)
        def _():
            w1b_ref[...] = w1_ref[0].astype(BF16)
            w3b_ref[...] = w3_ref[0].astype(BF16)
            w2b_ref[...] = w2_ref[0].astype(BF16)

        def triple(p, carry):
            from_scratch(3 * p, 3)
            return carry
        lax.fori_loop(1, n // 3, triple, 0)

        done = (n // 3) * 3
        rem = n - done

        @pl.when(rem == 1)
        def _():
            from_scratch(done, 1)

        @pl.when(rem == 2)
        def _():
            from_scratch(done, 2)

        @pl.when(last_j)
        def _():
            def wait3(p, carry):
                flush(3 * p, 3).wait()
                return carry
            lax.fori_loop(0, n // 3, wait3, 0)

            @pl.when(rem == 1)
            def _():
                flush(done, 1).wait()

            @pl.when(rem == 2)
            def _():
                flush(done, 2).wait()


def _experts(xs, w1, w3, w2, bexp, nsub, nused, blk, sub, tf, s0, n_s, ys=None):
    n_rows, d = xs.shape
    f = w1.shape[2]
    n_f = f // tf
    assert (ys is None) == (s0 == 0)

    def blk_idx(s, nu):
        return jnp.minimum(s + s0, jnp.maximum(nu[0] - 1, s0))

    def exp_idx(s, be, nu):
        return be[blk_idx(s, nu)]

    def jj(s, j, nu):
        return jnp.where(s + s0 < nu[0], j, n_f - 1)

    in_specs = [pl.BlockSpec((blk, d), lambda s, j, be, ns, nu: (blk_idx(s, nu), 0)),
                pl.BlockSpec((1, d, tf), lambda s, j, be, ns, nu: (exp_idx(s, be, nu), 0, jj(s, j, nu))),
                pl.BlockSpec((1, d, tf), lambda s, j, be, ns, nu: (exp_idx(s, be, nu), 0, jj(s, j, nu))),
                pl.BlockSpec((1, tf, d), lambda s, j, be, ns, nu: (exp_idx(s, be, nu), jj(s, j, nu), 0))]
    args = [xs, w1, w3, w2]
    aliases = {}
    if s0:
        in_specs = [pl.BlockSpec(memory_space=pl.ANY)] + in_specs
        args = [ys] + args
        aliases = {3: 0}
    return pl.pallas_call(
        functools.partial(_experts_kernel, sub=sub, s0=s0),
        grid_spec=pltpu.PrefetchScalarGridSpec(
            num_scalar_prefetch=3,
            grid=(n_s, n_f),
            in_specs=in_specs,
            out_specs=pl.BlockSpec(memory_space=pl.ANY),
            scratch_shapes=[pltpu.VMEM((blk, d), F32),
                            pltpu.VMEM((d, tf), BF16), pltpu.VMEM((d, tf), BF16),
                            pltpu.VMEM((tf, d), BF16), pltpu.SemaphoreType.DMA((1,))]),
        out_shape=jax.ShapeDtypeStruct((n_rows, d), F32),
        input_output_aliases=aliases,
        compiler_params=_params("arbitrary", "arbitrary"),
        name="moe_experts" if s0 == 0 else "moe_experts_overflow",
    )(bexp, nsub, nused, *args)


def _dispatch_kernel(tile_ids_ref, n_tiles_ref, src_ref, x_hbm_ref, g_ref, o_ref, buf_ref, sem_ref, *, sub):
    k = pl.program_id(0)
    n_valid = n_tiles_ref[0]

    def start(step, slot):
        base = tile_ids_ref[step] * sub

        def body(q, carry):
            for half in range(2):
                r = 2 * q + half
                tok = src_ref[base + r]
                pltpu.make_async_copy(x_hbm_ref.at[pl.ds(tok, 1)], buf_ref.at[slot, pl.ds(r, 1)],
                                      sem_ref.at[slot]).start(priority=half)
            return carry
        lax.fori_loop(0, sub // 2, body, 0, unroll=4)

    @pl.when(jnp.logical_and(k == 0, n_valid > 0))
    def _():
        start(0, 0)

    @pl.when(k + 1 < n_valid)
    def _():
        start(k + 1, (k + 1) % 2)

    @pl.when(k < n_valid)
    def _():
        slot = k % 2
        pltpu.make_async_copy(x_hbm_ref.at[pl.ds(0, sub)], buf_ref.at[slot], sem_ref.at[slot]).wait()
        o_ref[...] = _rms(buf_ref[slot], g_ref[...]).astype(BF16)


def _dispatch(x, g, tile_ids, n_tiles, src, n_slots, sub):
    m, d = x.shape
    max_tiles = tile_ids.shape[0]
    return pl.pallas_call(
        functools.partial(_dispatch_kernel, sub=sub),
        grid_spec=pltpu.PrefetchScalarGridSpec(
            num_scalar_prefetch=3,
            grid=(max_tiles,),
            in_specs=[pl.BlockSpec(memory_space=pl.ANY),
                      pl.BlockSpec((1, d), lambda k, ids, nt, src: (0, 0))],
            out_specs=pl.BlockSpec((sub, d), lambda k, ids, nt, src: (ids[k], 0)),
            scratch_shapes=[pltpu.VMEM((2, sub, d), F32), pltpu.SemaphoreType.DMA((2,))]),
        out_shape=jax.ShapeDtypeStruct((n_slots, d), BF16),
        compiler_params=_params("arbitrary"),
        name="moe_dispatch",
    )(tile_ids, n_tiles, src, x, g.reshape(1, d))


def _combine_kernel(pos_ref, x_ref, gate_ref, g_ref, ys_hbm_ref, ya_ref, yb_ref, buf_ref, sem_ref, *, tm, n_a):
    i = pl.program_id(0)
    n = pl.num_programs(0)

    def start(step, slot):
        base = step * (tm * TOP_K)

        def body(r, carry):
            for kk in range(TOP_K):
                p = pos_ref[base + r * TOP_K + kk]
                pltpu.make_async_copy(ys_hbm_ref.at[pl.ds(p, 1)], buf_ref.at[slot, kk, pl.ds(r, 1)],
                                      sem_ref.at[slot]).start(priority=kk % 2)
            return carry
        lax.fori_loop(0, tm, body, 0, unroll=4)

    @pl.when(i == 0)
    def _():
        start(0, 0)

    @pl.when(i + 1 < n)
    def _():
        start(i + 1, (i + 1) % 2)

    slot = i % 2
    for kk in range(TOP_K):
        pltpu.make_async_copy(ys_hbm_ref.at[pl.ds(0, tm)], buf_ref.at[slot, kk], sem_ref.at[slot]).wait()
    y = x_ref[...]
    for kk in range(TOP_K):
        y = y + gate_ref[:, kk:kk + 1] * buf_ref[slot, kk]
    y = _rms(y, g_ref[...])

    @pl.when(i < n_a)
    def _():
        ya_ref[...] = y

    @pl.when(i >= n_a)
    def _():
        yb_ref[...] = y


def _combine(x, gates, pos, ys, g, tm, m_a):
    m, d = x.shape
    n_a = m_a // tm
    assert n_a * tm == m_a and m % tm == 0
    return pl.pallas_call(
        functools.partial(_combine_kernel, tm=tm, n_a=n_a),
        grid_spec=pltpu.PrefetchScalarGridSpec(
            num_scalar_prefetch=1,
            grid=(m // tm,),
            in_specs=[pl.BlockSpec((tm, d), lambda i, pos: (i, 0)),
                      pl.BlockSpec((tm, TOP_K), lambda i, pos: (i, 0)),
                      pl.BlockSpec((1, d), lambda i, pos: (0, 0)),
                      pl.BlockSpec(memory_space=pl.ANY)],
            out_specs=[pl.BlockSpec((tm, d), lambda i, pos: (jnp.minimum(i, n_a - 1), 0)),
                       pl.BlockSpec((tm, d), lambda i, pos: (jnp.maximum(i - n_a, 0), 0))],
            scratch_shapes=[pltpu.VMEM((2, TOP_K, tm, d), F32), pltpu.SemaphoreType.DMA((2,))]),
        out_shape=[jax.ShapeDtypeStruct((m_a, d), F32), jax.ShapeDtypeStruct((m - m_a, d), F32)],
        compiler_params=_params("arbitrary"),
        name="moe_combine",
    )(pos, x, gates, g.reshape(1, d), ys)


def _moe_plan(top_i, n_exp, blk, sub, n_blocks, max_tiles):
    m = top_i.shape[0]
    e_flat = top_i.reshape(-1)
    onehot = (jnp.arange(n_exp, dtype=I32)[:, None] == e_flat[None, :]).astype(I32)
    csum = jnp.cumsum(onehot, axis=1)
    rank = jnp.sum((csum - 1) * onehot, axis=0)
    counts = csum[:, -1]
    blocks_per = (counts + blk - 1) // blk
    block_end = jnp.cumsum(blocks_per)
    block_start = block_end - blocks_per
    n_used = block_end[-1]
    pos = block_start[e_flat] * blk + rank

    s = jnp.arange(n_blocks, dtype=I32)
    bexp = jnp.minimum(jnp.sum((s[:, None] >= block_end[None, :]).astype(I32), axis=1), n_exp - 1)
    valid = jnp.clip(counts[bexp] - (s - block_start[bexp]) * blk, 0, blk)
    valid = jnp.where(s < n_used, valid, 0)

    by_expert = jnp.argsort(e_flat, stable=True).astype(I32) // TOP_K
    by_expert = jnp.concatenate([by_expert, jnp.zeros((m,), I32)])
    first_pair = jnp.cumsum(counts) - counts
    src = jnp.zeros((n_blocks * blk + m,), I32)
    for e in range(n_exp):
        run = lax.dynamic_slice(by_expert, (first_pair[e],), (m,))
        src = lax.dynamic_update_slice(src, run, (block_start[e] * blk,))
    src = src[:n_blocks * blk]
    nsub = (valid + sub - 1) // sub
    last_exp = bexp[jnp.maximum(n_used - 1, 0)]
    bexp = jnp.where(s < n_used, bexp, last_exp)

    per_blk = blk // sub
    t = jnp.arange(n_blocks * per_blk, dtype=I32)
    tile_valid = (t % per_blk) < nsub[t // per_blk]
    n_tiles = jnp.sum(tile_valid.astype(I32))
    order = jnp.argsort(jnp.logical_not(tile_valid), stable=True).astype(I32)[:max_tiles]
    last_tile = order[jnp.maximum(n_tiles - 1, 0)]
    tile_ids = jnp.where(jnp.arange(max_tiles, dtype=I32) < n_tiles, order, last_tile)
    return pos, src, bexp, nsub, n_used.reshape(1), tile_ids, n_tiles.reshape(1)


def _moe(x, top_i, gates, g_ffn, g_final, w1, w3, w2, blk, sub, tf, tm_c, m_a):
    m, d = x.shape
    n_exp = w1.shape[0]
    n_blocks = (m * TOP_K + n_exp * (blk - 1)) // blk
    max_tiles = (m * TOP_K + n_exp * (sub - 1)) // sub
    pos, src, bexp, nsub, nused, tile_ids, n_tiles = _moe_plan(top_i, n_exp, blk, sub, n_blocks, max_tiles)
    xs = _dispatch(x, g_ffn, tile_ids, n_tiles, src, n_blocks * blk, sub)
    n_main = min(n_blocks, n_exp * (-(-(m * TOP_K) // (n_exp * blk))) + 2)
    ys = _experts(xs, w1, w3, w2, bexp, nsub, nused, blk, sub, tf, 0, n_main)
    if n_main < n_blocks:
        ys = lax.cond(
            nused[0] > n_main,
            lambda y: _experts(xs, w1, w3, w2, bexp, nsub, nused, blk, sub, tf, n_main, n_blocks - n_main, ys=y),
            lambda y: y, ys)
    return _combine(x, gates, pos, ys, g_final, tm_c, m_a)


def kernel(x_prompt, x_sample, state_conv_a, state_conv_c, state_pool_d, norm_mix_g, norm_ffn_g, norm_final_g, w_in_ab, conv_a, ln_b_g, ln_b_b, w_s, b_s, w_out_ab, ffn_w1, ffn_w3, ffn_w2, w_in_cd, conv_c, conv_c_b, ln_c_g, ln_c_b, pool_w, pool_scale, w_out_cd, router, exp_w1, exp_w3, exp_w2):
    n_seq, seq_len, d = x_prompt.shape
    nseq_s, dl, _ = x_sample.shape
    mp, ms = n_seq * seq_len, nseq_s * dl
    m = mp + ms
    assert norm_mix_g.shape[0] == 2, "two layers: conv/chunk-MLP + dense FFN, then conformer/pool + MoE"
    assert seq_len % CHUNK == 0 and dl <= CHUNK and mp % ms == 0

    big = m * TOP_K >= 4 * 2560
    tm_in = _pick(m, (2176, 1088, 544, 272))
    tm_mid = _pick(m, (1088, 544, 272))
    tm_row = _pick(ms, (512, 256, 128, 64))
    t_rows = _pick(seq_len, (256, 128))
    tn = 1024 if big else 512
    tf_up = _pick(ffn_w1.shape[2], (512, 256))
    tn_down = 256
    tf_e = 256
    blk_e, sub_e = (2560, 256) if big else (512, 128)
    tm_c = _pick(ms, (256, 128, 64))

    xp = x_prompt.reshape(mp, d)
    xs_lm = jnp.transpose(x_sample, (1, 0, 2)).reshape(ms, d)

    h = _norm_in(xp, xs_lm, norm_mix_g[0], tm_row)
    proj = _matmul(h, w_in_ab[0], tm_in, tn)
    mixed, a_p, v_p = _mix_ab_prompt(proj, conv_a[0], ln_b_g[0], ln_b_b[0], w_s[0], b_s[0],
                                     n_seq, seq_len, t_rows)
    mixed, a_s, v_s = _mix_ab_sample(mixed, proj, jnp.transpose(state_conv_a[0], (1, 0, 2)),
                                     conv_a[0], ln_b_g[0], ln_b_b[0], w_s[0], b_s[0], mp, dl, nseq_s)
    x, h = _out_proj(mixed, w_out_ab[0].astype(BF16), xp, xs_lm, norm_ffn_g[0], tm_row)
    act = _ffn_up(h, ffn_w1, ffn_w3, tm_in, tf_up)
    x = _ffn_down(act, ffn_w2, x, tm_mid, tn_down)

    h = _norm_rows(x, norm_mix_g[1], tm_mid)
    proj = _matmul(h, w_in_cd[0], tm_in, tn)
    mixed, c_p, d_p = _mix_cd_prompt(proj, conv_c[0], conv_c_b[0], ln_c_g[0], ln_c_b[0], pool_w[0],
                                     pool_scale[0], n_seq, seq_len, t_rows)
    mixed, g_s = _mix_cd_sample(mixed, proj, jnp.transpose(state_conv_c[0], (1, 0, 2)),
                                jnp.transpose(state_pool_d[0], (1, 0, 2)), conv_c[0], conv_c_b[0],
                                ln_c_g[0], ln_c_b[0], pool_w[0], pool_scale[0], mp, dl, nseq_s)
    (x,) = _out_proj(mixed, w_out_cd[0].astype(BF16), x, x, None, tm_row)
    top_i, gates = _router(x, norm_ffn_g[1], router[0], tm_mid)
    y_p, y_s = _moe(x, top_i, gates, norm_ffn_g[1], norm_final_g, exp_w1[0], exp_w3[0], exp_w2[0],
                    blk_e, sub_e, tf_e, tm_c, mp)

    dc = conv_c.shape[2]
    y_prompt = y_p.reshape(n_seq, seq_len, d)
    y_sample = jnp.transpose(y_s.reshape(dl, nseq_s, d), (1, 0, 2))
    new_a_s = jnp.transpose(a_s, (1, 0, 2))
    new_v_s = jnp.transpose(v_s.reshape(dl, nseq_s, -1), (1, 0, 2))
    g_rows = jnp.transpose(g_s.reshape(dl, nseq_s, dc), (1, 0, 2))
    p_rows = jnp.transpose(proj[mp:, 2 * dc:].astype(F32).reshape(dl, nseq_s, -1), (1, 0, 2))
    n_c = state_conv_c.shape[2]
    n_d = state_pool_d.shape[2]
    new_c_s = jnp.concatenate([state_conv_c[0], g_rows], axis=1)[:, -n_c:]
    new_d_s = jnp.concatenate([state_pool_d[0], p_rows], axis=1)[:, -n_d:]
    return (y_prompt, y_sample,
            a_p[None], new_a_s[None],
            v_p[None], new_v_s[None],
            c_p[None], new_c_s[None],
            d_p[None], new_d_s[None])
```
